```python
import jax
import jax.numpy as jnp
from jax import lax
import numpy as np

D_MODEL = 1024
BATCH = 2
SEQ = 8192
DEPTH = 4
DEC_BATCH = 32
DEC_SEQ = 1
PAST_LEN = 8192
PAGE_SIZE = 128

N_MIXERS = 3
N_SGU_LAYERS = (DEPTH + 2) // 3
N_SB_LAYERS = (DEPTH + 1) // 3
N_MOBA_LAYERS = DEPTH // 3
PLE_DIM = 256
SGU_CHUNK = 128
SGU_WIDTH = 2 * D_MODEL
SGU_GROUPS = 8
SGU_GROUP_DIM = SGU_WIDTH // SGU_GROUPS
N_HEADS = 16
HEAD_DIM = D_MODEL // N_HEADS
SB_Q_BLOCK = 128
MOBA_BLOCK = 256
MOBA_TOPK = 3
MOBA_Q_BLOCK = 64
ROT_DIM = HEAD_DIM // 4
ROPE_THETA = 500000.0
D_FF = 2816
CONV_W = 3
EPS = 1e-6

kernel_name = 'hybrid_sgu_stickbreak_moba_decoder_step'


def _rmsnorm(x, g):
    xf = x.astype(jnp.float32)
    y = xf * lax.rsqrt(jnp.mean(xf * xf, axis=-1, keepdims=True) + EPS)
    return (y * g.astype(jnp.float32)).astype(x.dtype)


def _rotary(x, pos):
    half = ROT_DIM // 2
    inv_freq = ROPE_THETA ** (-jnp.arange(0, ROT_DIM, 2, dtype=jnp.float32) / ROT_DIM)
    ang = pos.astype(jnp.float32)[:, None] * inv_freq[None, :]
    cos = jnp.cos(ang)[None, :, None, :]
    sin = jnp.sin(ang)[None, :, None, :]
    xr = x[..., :ROT_DIM].astype(jnp.float32)
    x1, x2 = xr[..., :half], xr[..., half:]
    rot = jnp.concatenate([x1 * cos - x2 * sin, x2 * cos + x1 * sin], axis=-1)
    return jnp.concatenate([rot.astype(x.dtype), x[..., ROT_DIM:]], axis=-1)


def _heads(t):
    return t.reshape(t.shape[0], t.shape[1], N_HEADS, HEAD_DIM)


def _merge(o):
    return o.reshape(o.shape[0], o.shape[1], D_MODEL)


def _qkv(h, w_qkv):
    q, k, v = jnp.split(h @ w_qkv, 3, axis=-1)
    return _heads(q), _heads(k), _heads(v)


def _gather_past(cache, page_table, layer):
    rows = cache[page_table, layer]
    return rows.reshape(page_table.shape[0], -1, N_HEADS, HEAD_DIM)


def _sgu_project(h, w_in, v_gain):
    z = jax.nn.gelu(h @ w_in)
    u, v = jnp.split(z, 2, axis=-1)
    return u, _rmsnorm(v, v_gain)


def _sgu_prompt(u, v, w_s, b_s):
    b, t, _ = v.shape
    nc = t // SGU_CHUNK
    vc = v.reshape(b, nc, SGU_CHUNK, SGU_GROUPS, SGU_GROUP_DIM)
    w = w_s * jnp.tril(jnp.ones((SGU_CHUNK, SGU_CHUNK), w_s.dtype))
    mixed = jnp.einsum('gij,bcjgd->bcigd', w, vc) + b_s.T[None, None, :, :, None]
    return u * mixed.reshape(b, t, SGU_WIDTH)


def _sgu_sample(u, v, w_s, b_s):
    b, t, _ = v.shape
    idx = jnp.arange(t)
    off = idx % SGU_CHUNK
    cid = idx // SGU_CHUNK
    mask = (cid[:, None] == cid[None, :]) & (idx[None, :] <= idx[:, None])
    w = jnp.where(mask[None], w_s[:, off[:, None], off[None, :]], 0.0)
    vg = v.reshape(b, t, SGU_GROUPS, SGU_GROUP_DIM)
    mixed = jnp.einsum('gij,bjgd->bigd', w, vg) + b_s[:, off].T[None, :, :, None]
    return u * mixed.reshape(b, t, SGU_WIDTH)


def _stick_breaking(q, k, v, q_pos, k_pos):
    z = jnp.einsum('bqhd,bkhd->bhqk', q, k).astype(jnp.float32) * (HEAD_DIM ** -0.5)
    mask = (k_pos[None, :] < q_pos[:, None])[None, None]
    log_keep = jnp.where(mask, jax.nn.log_sigmoid(-z), 0.0)
    later = lax.cumsum(log_keep, axis=3, reverse=True) - log_keep
    a = jnp.where(mask, jnp.exp(jax.nn.log_sigmoid(z) + later), 0.0)
    return jnp.einsum('bhqk,bkhd->bqhd', a.astype(v.dtype), v)


def _sb_prompt(q, k, v):
    b, t, h, d = q.shape
    nb = t // SB_Q_BLOCK
    k_pos = jnp.arange(t, dtype=jnp.int32)
    qb = jnp.moveaxis(q.reshape(b, nb, SB_Q_BLOCK, h, d), 1, 0)
    pb = k_pos.reshape(nb, SB_Q_BLOCK)
    out = lax.map(lambda a: _stick_breaking(a[0], k, v, a[1], k_pos), (qb, pb))
    return jnp.moveaxis(out, 0, 1).reshape(b, t, h, d)


def _moba_blocks(k, v):
    b, l, h, d = k.shape
    nb = -(-l // MOBA_BLOCK)
    pad = nb * MOBA_BLOCK - l

    def blk(t):
        t = jnp.pad(t, ((0, 0), (0, pad), (0, 0), (0, 0)))
        return t.reshape(b, nb, MOBA_BLOCK, h, d).transpose(0, 3, 1, 2, 4)

    kb, vb = blk(k), blk(v)
    k_mean = jnp.mean(kb.astype(jnp.float32), axis=3).astype(k.dtype)
    return kb, vb, k_mean


def _moba_attend(q, q_pos, kb, vb, k_mean):
    b, tq, h, d = q.shape
    nb = kb.shape[2]
    k_eff = min(MOBA_TOPK, nb)
    own = q_pos // MOBA_BLOCK
    gate = jnp.einsum('bqhd,bhnd->bqhn', q, k_mean).astype(jnp.float32)
    full_past = jnp.arange(nb)[None, :] < own[:, None]
    gate = jnp.where(full_past[None, :, None, :], gate, -jnp.inf)
    _, sel = lax.top_k(gate, k_eff)
    b_ix = jnp.arange(b)[:, None, None, None]
    h_ix = jnp.arange(h)[None, None, :, None]
    k_sel = kb[b_ix, h_ix, sel]
    v_sel = vb[b_ix, h_ix, sel]
    b3, h3, own3 = b_ix[..., 0], h_ix[..., 0], own[None, :, None]
    k_own = kb[b3, h3, own3]
    v_own = vb[b3, h3, own3]
    scale = HEAD_DIM ** -0.5
    s_sel = jnp.einsum('bqhd,bqhkcd->bqhkc', q, k_sel).astype(jnp.float32) * scale
    s_own = jnp.einsum('bqhd,bqhcd->bqhc', q, k_own).astype(jnp.float32) * scale
    sel_ok = jnp.arange(k_eff)[None, :] < own[:, None]
    s_sel = jnp.where(sel_ok[None, :, None, :, None], s_sel, -jnp.inf)
    own_pos = own[:, None] * MOBA_BLOCK + jnp.arange(MOBA_BLOCK)[None, :]
    own_ok = own_pos <= q_pos[:, None]
    s_own = jnp.where(own_ok[None, :, None, :], s_own, -jnp.inf)
    s = jnp.concatenate([s_sel.reshape(b, tq, h, k_eff * MOBA_BLOCK), s_own], axis=-1)
    p = jax.nn.softmax(s, axis=-1).astype(q.dtype)
    p_sel = p[..., :k_eff * MOBA_BLOCK].reshape(b, tq, h, k_eff, MOBA_BLOCK)
    p_own = p[..., k_eff * MOBA_BLOCK:]
    return (jnp.einsum('bqhkc,bqhkcd->bqhd', p_sel, v_sel)
            + jnp.einsum('bqhc,bqhcd->bqhd', p_own, v_own))


def _moba_prompt(q, k, v):
    b, t, h, d = q.shape
    kb, vb, km = _moba_blocks(k, v)
    nq = t // MOBA_Q_BLOCK
    qb = jnp.moveaxis(q.reshape(b, nq, MOBA_Q_BLOCK, h, d), 1, 0)
    pb = jnp.arange(t, dtype=jnp.int32).reshape(nq, MOBA_Q_BLOCK)
    out = lax.map(lambda a: _moba_attend(a[0], a[1], kb, vb, km), (qb, pb))
    return jnp.moveaxis(out, 0, 1).reshape(b, t, h, d)


def _conv_ffn(h, conv_prev, w_in, conv_w, conv_b, w_out):
    g, val = jnp.split(h @ w_in, 2, axis=-1)
    g_ext = jnp.concatenate([conv_prev, g], axis=1)
    t = g.shape[1]
    c = conv_b
    for kk in range(CONV_W):
        c = c + conv_w[kk] * g_ext[:, kk:kk + t]
    out = (jax.nn.silu(c) * val) @ w_out
    return out, g_ext[:, g_ext.shape[1] - (CONV_W - 1):]


def setup_inputs(seed: int = 0) -> dict:
    key = jax.random.key(seed)
    ks = jax.random.split(key, 32)
    n_pages = PAST_LEN // PAGE_SIZE
    n_used = DEC_BATCH * n_pages
    n_pool = n_used + n_used // 4

    def nrm(k, shape, scale=1.0):
        return jax.random.normal(k, shape, jnp.float32) * scale

    def gain(k, shape):
        return 1.0 + 0.05 * jax.random.normal(k, shape, jnp.float32)

    page_table = jax.random.permutation(ks[8], n_pool)[:n_used].reshape(DEC_BATCH, n_pages).astype(jnp.int32)
    return {
        'x_prompt': nrm(ks[0], (BATCH, SEQ, D_MODEL)),
        'x_sample': nrm(ks[1], (DEC_BATCH, DEC_SEQ, D_MODEL)),
        'cache_sb_k': nrm(ks[2], (n_pool, N_SB_LAYERS, PAGE_SIZE, N_HEADS, HEAD_DIM)),
        'cache_sb_v': nrm(ks[3], (n_pool, N_SB_LAYERS, PAGE_SIZE, N_HEADS, HEAD_DIM)),
        'cache_moba_k': nrm(ks[4], (n_pool, N_MOBA_LAYERS, PAGE_SIZE, N_HEADS, HEAD_DIM)),
        'cache_moba_v': nrm(ks[5], (n_pool, N_MOBA_LAYERS, PAGE_SIZE, N_HEADS, HEAD_DIM)),
        'state_ffn_conv': nrm(ks[6], (DEPTH, DEC_BATCH, CONV_W - 1, D_FF)),
        'page_table': page_table,
        'p_prompt': nrm(ks[7], (DEPTH, BATCH, SEQ, PLE_DIM)),
        'p_sample': nrm(ks[9], (DEPTH, DEC_BATCH, DEC_SEQ, PLE_DIM)),
        'mix_norm': gain(ks[10], (DEPTH, D_MODEL)),
        'ffn_norm': gain(ks[11], (DEPTH, D_MODEL)),
        'ple_norm': gain(ks[12], (DEPTH, D_MODEL)),
        'sgu_w_in': nrm(ks[13], (N_SGU_LAYERS, D_MODEL, 2 * SGU_WIDTH), D_MODEL ** -0.5),
        'sgu_v_norm': gain(ks[14], (N_SGU_LAYERS, SGU_WIDTH)),
        'sgu_w_s': nrm(ks[15], (N_SGU_LAYERS, SGU_GROUPS, SGU_CHUNK, SGU_CHUNK), SGU_CHUNK ** -0.5),
        'sgu_b_s': gain(ks[16], (N_SGU_LAYERS, SGU_GROUPS, SGU_CHUNK)),
        'sgu_w_out': nrm(ks[17], (N_SGU_LAYERS, SGU_WIDTH, D_MODEL), SGU_WIDTH ** -0.5),
        'sb_w_qkv': nrm(ks[18], (N_SB_LAYERS, D_MODEL, 3 * D_MODEL), D_MODEL ** -0.5),
        'sb_w_out': nrm(ks[19], (N_SB_LAYERS, D_MODEL, D_MODEL), D_MODEL ** -0.5),
        'moba_w_qkv': nrm(ks[20], (N_MOBA_LAYERS, D_MODEL, 3 * D_MODEL), D_MODEL ** -0.5),
        'moba_q_norm': gain(ks[21], (N_MOBA_LAYERS, HEAD_DIM)),
        'moba_k_norm': gain(ks[22], (N_MOBA_LAYERS, HEAD_DIM)),
        'moba_w_out': nrm(ks[23], (N_MOBA_LAYERS, D_MODEL, D_MODEL), D_MODEL ** -0.5),
        'ffn_w_in': nrm(ks[24], (DEPTH, D_MODEL, 2 * D_FF), D_MODEL ** -0.5),
        'ffn_conv_w': nrm(ks[25], (DEPTH, CONV_W, D_FF), CONV_W ** -0.5),
        'ffn_conv_b': nrm(ks[26], (DEPTH, D_FF), 0.02),
        'ffn_w_out': nrm(ks[27], (DEPTH, D_FF, D_MODEL), D_FF ** -0.5),
        'ple_w_in': nrm(ks[28], (DEPTH, PLE_DIM, D_MODEL), PLE_DIM ** -0.5),
        'ple_w_gate': nrm(ks[29], (DEPTH, D_MODEL, D_MODEL), D_MODEL ** -0.5),
    }


def reference(x_prompt, x_sample, cache_sb_k, cache_sb_v, cache_moba_k, cache_moba_v,
              state_ffn_conv, page_table, p_prompt, p_sample,
              mix_norm, ffn_norm, ple_norm,
              sgu_w_in, sgu_v_norm, sgu_w_s, sgu_b_s, sgu_w_out,
              sb_w_qkv, sb_w_out,
              moba_w_qkv, moba_q_norm, moba_k_norm, moba_w_out,
              ffn_w_in, ffn_conv_w, ffn_conv_b, ffn_w_out,
              ple_w_in, ple_w_gate):
    xp, xs = x_prompt, x_sample
    bp, tp = xp.shape[0], xp.shape[1]
    bs, ts = xs.shape[0], xs.shape[1]
    past_len = page_table.shape[1] * PAGE_SIZE
    pos_p = jnp.arange(tp, dtype=jnp.int32)
    pos_s = past_len + jnp.arange(ts, dtype=jnp.int32)
    pos_all = jnp.arange(past_len + ts, dtype=jnp.int32)
    sb_kp, sb_vp, sb_ks, sb_vs = [], [], [], []
    mb_kp, mb_vp, mb_ks, mb_vs = [], [], [], []
    sgu_vs, conv_p, conv_s = [], [], []

    for i in range(DEPTH):
        kind, j = i % N_MIXERS, i // N_MIXERS
        hp = _rmsnorm(xp, mix_norm[i])
        hs = _rmsnorm(xs, mix_norm[i])
        if kind == 0:
            u_p, v_p = _sgu_project(hp, sgu_w_in[j], sgu_v_norm[j])
            u_s, v_s = _sgu_project(hs, sgu_w_in[j], sgu_v_norm[j])
            mp = _sgu_prompt(u_p, v_p, sgu_w_s[j], sgu_b_s[j]) @ sgu_w_out[j]
            ms = _sgu_sample(u_s, v_s, sgu_w_s[j], sgu_b_s[j]) @ sgu_w_out[j]
            sgu_vs.append(v_s)
        elif kind == 1:
            q_p, k_p, v_p = _qkv(hp, sb_w_qkv[j])
            q_s, k_s, v_s = _qkv(hs, sb_w_qkv[j])
            mp = _merge(_sb_prompt(q_p, k_p, v_p)) @ sb_w_out[j]
            k_all = jnp.concatenate([_gather_past(cache_sb_k, page_table, j), k_s], axis=1)
            v_all = jnp.concatenate([_gather_past(cache_sb_v, page_table, j), v_s], axis=1)
            ms = _merge(_stick_breaking(q_s, k_all, v_all, pos_s, pos_all)) @ sb_w_out[j]
            sb_kp.append(k_p)
            sb_vp.append(v_p)
            sb_ks.append(k_s)
            sb_vs.append(v_s)
        else:
            q_p, k_p, v_p = _qkv(hp, moba_w_qkv[j])
            q_s, k_s, v_s = _qkv(hs, moba_w_qkv[j])
            q_p = _rotary(_rmsnorm(q_p, moba_q_norm[j]), pos_p)
            k_p = _rotary(_rmsnorm(k_p, moba_k_norm[j]), pos_p)
            q_s = _rotary(_rmsnorm(q_s, moba_q_norm[j]), pos_s)
            k_s = _rotary(_rmsnorm(k_s, moba_k_norm[j]), pos_s)
            mp = _merge(_moba_prompt(q_p, k_p, v_p)) @ moba_w_out[j]
            k_all = jnp.concatenate([_gather_past(cache_moba_k, page_table, j), k_s], axis=1)
            v_all = jnp.concatenate([_gather_past(cache_moba_v, page_table, j), v_s], axis=1)
            kb, vb, km = _moba_blocks(k_all, v_all)
            ms = _merge(_moba_attend(q_s, pos_s, kb, vb, km)) @ moba_w_out[j]
            mb_kp.append(k_p)
            mb_vp.append(v_p)
            mb_ks.append(k_s)
            mb_vs.append(v_s)
        xp = xp + mp
        xs = xs + ms

        zero_prev = jnp.zeros((bp, CONV_W - 1, D_FF), xp.dtype)
        fp, cp = _conv_ffn(_rmsnorm(xp, ffn_norm[i]), zero_prev,
                           ffn_w_in[i], ffn_conv_w[i], ffn_conv_b[i], ffn_w_out[i])
        fs, cs = _conv_ffn(_rmsnorm(xs, ffn_norm[i]), state_ffn_conv[i],
                           ffn_w_in[i], ffn_conv_w[i], ffn_conv_b[i], ffn_w_out[i])
        xp = xp + fp
        xs = xs + fs
        conv_p.append(cp)
        conv_s.append(cs)

        gp = jax.nn.sigmoid(_rmsnorm(xp, ple_norm[i]) @ ple_w_gate[i])
        gs = jax.nn.sigmoid(_rmsnorm(xs, ple_norm[i]) @ ple_w_gate[i])
        xp = xp + gp * (p_prompt[i] @ ple_w_in[i])
        xs = xs + gs * (p_sample[i] @ ple_w_in[i])

    sb_k_prompt = jnp.stack(sb_kp, axis=1)
    sb_v_prompt = jnp.stack(sb_vp, axis=1)
    sb_k_sample = jnp.stack(sb_ks, axis=1)
    sb_v_sample = jnp.stack(sb_vs, axis=1)
    moba_k_prompt = jnp.stack(mb_kp, axis=1)
    moba_v_prompt = jnp.stack(mb_vp, axis=1)
    moba_k_sample = jnp.stack(mb_ks, axis=1)
    moba_v_sample = jnp.stack(mb_vs, axis=1)
    sgu_v_sample = jnp.stack(sgu_vs, axis=0)
    ffn_conv_prompt = jnp.stack(conv_p, axis=0)
    ffn_conv_sample = jnp.stack(conv_s, axis=0)
    return (xp, xs, sb_k_prompt, sb_v_prompt, sb_k_sample, sb_v_sample,
            moba_k_prompt, moba_v_prompt, moba_k_sample, moba_v_sample,
            sgu_v_sample, ffn_conv_prompt, ffn_conv_sample)
```

```python
import functools

import jax
import jax.numpy as jnp
from jax import lax
from jax.experimental import pallas as pl
from jax.experimental.pallas import tpu as pltpu

F32 = jnp.float32
BF16 = jnp.bfloat16

D_MODEL = 1024
N_HEADS = 16
HEAD_DIM = 64
PAGE_SIZE = 128
SGU_CHUNK = 128
SGU_GROUPS = 8
SGU_WIDTH = 2 * D_MODEL
SGU_GROUP_DIM = SGU_WIDTH // SGU_GROUPS
MOBA_BLOCK = 256
MOBA_TOPK = 3
ROT_DIM = HEAD_DIM // 4
ROPE_THETA = 500000.0
D_FF = 2816
CONV_W = 3
EPS = 1e-6
N_MIXERS = 3

LANES = 128
HEADS_PER_LANE_TILE = LANES // HEAD_DIM
N_HEAD_PAIRS = D_MODEL // LANES
QK_SCALE = HEAD_DIM ** -0.5
MASK_BIAS = -1e30
VMEM_LIMIT = 48 * 1024 * 1024


def _cparams(*sem):
    return pltpu.CompilerParams(dimension_semantics=sem, vmem_limit_bytes=VMEM_LIMIT)


def _rms(x, g):
    return x * lax.rsqrt(jnp.mean(x * x, axis=-1, keepdims=True) + EPS) * g


def _dot(a, b):
    return jnp.dot(a, b, preferred_element_type=F32)


def _dot_nt(a, b):
    return lax.dot_general(a, b, (((1,), (1,)), ((), ())), preferred_element_type=F32)


def _split_bf16(x):
    hi = x.astype(BF16)
    lo = (x - hi.astype(F32)).astype(BF16)
    return jnp.concatenate([hi, lo], axis=1)


def _strict_upper_sum_matrix(n):
    j = lax.broadcasted_iota(jnp.int32, (n, n), 0)
    s = lax.broadcasted_iota(jnp.int32, (n, n), 1)
    u = jnp.where(j > s, 1.0, 0.0).astype(BF16)
    return jnp.concatenate([u, u], axis=0)


def _log_sigmoid_pair(z):
    lsp = jnp.minimum(z, 0.0) - jnp.log(1.0 + jnp.exp(-jnp.abs(z)))
    return lsp, lsp - z


def _norm_matmul_kernel(x_ref, g_ref, w_ref, o_ref, xn_ref, *, act):
    @pl.when(pl.program_id(1) == 0)
    def _():
        xn_ref[...] = _rms(x_ref[...], g_ref[...]).astype(BF16)

    y = _dot(xn_ref[...], w_ref[...])
    if act == "gelu":
        y = jax.nn.gelu(y)
    o_ref[...] = y


def _lane_tile(n, cap):
    return max(t for t in range(LANES, cap + 1, LANES) if n % t == 0)


def _norm_matmul(x, gain, w, n_split, act=None, tm=512, tn_cap=1536):
    m, k = x.shape
    n_out = w.shape[1]
    per = n_out // n_split
    tm = min(tm, m)
    tn = _lane_tile(per, tn_cap)
    nj = per // tn
    assert m % tm == 0 and per % tn == 0
    return pl.pallas_call(
        functools.partial(_norm_matmul_kernel, act=act),
        out_shape=jax.ShapeDtypeStruct((n_split, m, per), F32),
        grid=(m // tm, n_out // tn),
        in_specs=[
            pl.BlockSpec((tm, k), lambda i, j: (i, 0)),
            pl.BlockSpec((1, k), lambda i, j: (0, 0)),
            pl.BlockSpec((k, tn), lambda i, j: (0, j)),
        ],
        out_specs=pl.BlockSpec((None, tm, tn), lambda i, j: (j // nj, i, j % nj)),
        scratch_shapes=[pltpu.VMEM((tm, k), BF16)],
        compiler_params=_cparams("parallel", "arbitrary"),
        name="norm_matmul",
    )(x, gain.reshape(1, k), w)


def _proj_residual_kernel(h_ref, w_ref, x_ref, o_ref):
    o_ref[...] = x_ref[...] + _dot(h_ref[...].astype(BF16), w_ref[...])


def _proj_residual(h, w, x, tm=1024):
    m, k = h.shape
    n = w.shape[1]
    tm = min(tm, m)
    return pl.pallas_call(
        _proj_residual_kernel,
        out_shape=jax.ShapeDtypeStruct((m, n), F32),
        grid=(m // tm,),
        in_specs=[
            pl.BlockSpec((tm, k), lambda i: (i, 0)),
            pl.BlockSpec((k, n), lambda i: (0, 0)),
            pl.BlockSpec((tm, n), lambda i: (i, 0)),
        ],
        out_specs=pl.BlockSpec((tm, n), lambda i: (i, 0)),
        compiler_params=_cparams("parallel"),
        name="proj_residual",
    )(h, w, x)


def _ple_kernel(x_ref, p_ref, g_ref, wg_ref, wp_ref, o_ref):
    x = x_ref[...]
    gate = jax.nn.sigmoid(_dot(_rms(x, g_ref[...]).astype(BF16), wg_ref[...]))
    o_ref[...] = x + gate * _dot(p_ref[...].astype(BF16), wp_ref[...])


def _ple(x, p, gain, w_gate, w_in, tm=512):
    m, d = x.shape
    pd = p.shape[1]
    tm = min(tm, m)
    return pl.pallas_call(
        _ple_kernel,
        out_shape=jax.ShapeDtypeStruct((m, d), F32),
        grid=(m // tm,),
        in_specs=[
            pl.BlockSpec((tm, d), lambda i: (i, 0)),
            pl.BlockSpec((tm, pd), lambda i: (i, 0)),
            pl.BlockSpec((1, d), lambda i: (0, 0)),
            pl.BlockSpec((d, d), lambda i: (0, 0)),
            pl.BlockSpec((pd, d), lambda i: (0, 0)),
        ],
        out_specs=pl.BlockSpec((tm, d), lambda i: (i, 0)),
        compiler_params=_cparams("parallel"),
        name="ple",
    )(x, p, gain.reshape(1, d), w_gate, w_in)


def _sgu_prompt_kernel(u_ref, v_ref, vg_ref, ws_ref, bst_ref, wo_ref, x_ref, o_ref, *, chunks):
    v = _rms(v_ref[...], vg_ref[...]).astype(BF16)
    i = lax.broadcasted_iota(jnp.int32, (SGU_CHUNK, SGU_CHUNK), 0)
    j = lax.broadcasted_iota(jnp.int32, (SGU_CHUNK, SGU_CHUNK), 1)
    cols = []
    for g in range(SGU_GROUPS):
        w = jnp.where(i >= j, ws_ref[g], 0.0).astype(BF16)
        bias = bst_ref[:, g:g + 1]
        lo, hi = g * SGU_GROUP_DIM, (g + 1) * SGU_GROUP_DIM
        rows = [_dot(w, v[c * SGU_CHUNK:(c + 1) * SGU_CHUNK, lo:hi]) + bias for c in range(chunks)]
        cols.append(jnp.concatenate(rows, axis=0))
    mixed = jnp.concatenate(cols, axis=1)
    gated = (u_ref[...] * mixed).astype(BF16)
    o_ref[...] = x_ref[...] + _dot(gated, wo_ref[...])


def _sgu_prompt(z2, v_gain, w_s, b_s, w_out, x, chunks=2):
    m = x.shape[0]
    tm = chunks * SGU_CHUNK
    return pl.pallas_call(
        functools.partial(_sgu_prompt_kernel, chunks=chunks),
        out_shape=jax.ShapeDtypeStruct((m, D_MODEL), F32),
        grid=(m // tm,),
        in_specs=[
            pl.BlockSpec((None, tm, SGU_WIDTH), lambda i: (0, i, 0)),
            pl.BlockSpec((None, tm, SGU_WIDTH), lambda i: (1, i, 0)),
            pl.BlockSpec((1, SGU_WIDTH), lambda i: (0, 0)),
            pl.BlockSpec((SGU_GROUPS, SGU_CHUNK, SGU_CHUNK), lambda i: (0, 0, 0)),
            pl.BlockSpec((SGU_CHUNK, SGU_GROUPS), lambda i: (0, 0)),
            pl.BlockSpec((SGU_WIDTH, D_MODEL), lambda i: (0, 0)),
            pl.BlockSpec((tm, D_MODEL), lambda i: (i, 0)),
        ],
        out_specs=pl.BlockSpec((tm, D_MODEL), lambda i: (i, 0)),
        compiler_params=_cparams("parallel"),
        name="sgu_prompt",
    )(z2, z2, v_gain.reshape(1, SGU_WIDTH), w_s, b_s.T, w_out, x)


def _sgu_sample_kernel(u_ref, v_ref, vg_ref, w0_ref, b0_ref, wo_ref, x_ref, o_ref, vn_ref):
    v = _rms(v_ref[...], vg_ref[...])
    vn_ref[...] = v
    gated = u_ref[...] * (w0_ref[...] * v.astype(BF16).astype(F32) + b0_ref[...])
    o_ref[...] = x_ref[...] + _dot(gated.astype(BF16), wo_ref[...])


def _sgu_sample(z2, v_gain, w_s, b_s, w_out, x):
    m = x.shape[0]
    w0 = jnp.repeat(w_s[:, 0, 0].astype(BF16).astype(F32), SGU_GROUP_DIM).reshape(1, SGU_WIDTH)
    b0 = jnp.repeat(b_s[:, 0], SGU_GROUP_DIM).reshape(1, SGU_WIDTH)
    row = lambda i: (0, 0)
    return pl.pallas_call(
        _sgu_sample_kernel,
        out_shape=(jax.ShapeDtypeStruct((m, D_MODEL), F32), jax.ShapeDtypeStruct((m, SGU_WIDTH), F32)),
        grid=(1,),
        in_specs=[
            pl.BlockSpec((None, m, SGU_WIDTH), lambda i: (0, 0, 0)),
            pl.BlockSpec((None, m, SGU_WIDTH), lambda i: (1, 0, 0)),
            pl.BlockSpec((1, SGU_WIDTH), row),
            pl.BlockSpec((1, SGU_WIDTH), row),
            pl.BlockSpec((1, SGU_WIDTH), row),
            pl.BlockSpec((SGU_WIDTH, D_MODEL), row),
            pl.BlockSpec((m, D_MODEL), row),
        ],
        out_specs=(pl.BlockSpec((m, D_MODEL), row), pl.BlockSpec((m, SGU_WIDTH), row)),
        compiler_params=_cparams("arbitrary"),
        name="sgu_sample",
    )(z2, z2, v_gain.reshape(1, SGU_WIDTH), w0, b0, w_out, x)


def _ffn_gate(c, val):
    return (jax.nn.silu(c) * val).astype(BF16)


def _ffn_prompt_kernel(g_ref, val_ref, prev_ref, cw_ref, cb_ref, wo_ref, x_ref, o_ref, *, tiles_per_seq):
    g = g_ref[...]
    tm = g.shape[0]
    seq_start = (pl.program_id(0) % tiles_per_seq) == 0
    keep = jnp.where(seq_start, 0.0, 1.0)
    prev = prev_ref[...] * keep
    row = lax.broadcasted_iota(jnp.int32, (tm, 1), 0)
    g1 = jnp.where(row == 0, prev[7:8], pltpu.roll(g, 1, axis=0))
    g2 = jnp.where(row == 0, prev[6:7], jnp.where(row == 1, prev[7:8], pltpu.roll(g, 2, axis=0)))
    c = cb_ref[...] + cw_ref[0:1] * g2 + cw_ref[1:2] * g1 + cw_ref[2:3] * g
    o_ref[...] = x_ref[...] + _dot(_ffn_gate(c, val_ref[...]), wo_ref[...])


def _ffn_prompt(gv2, conv_w, conv_b, w_out, x, seq_len, tm=256):
    m = x.shape[0]
    sub = 8
    return pl.pallas_call(
        functools.partial(_ffn_prompt_kernel, tiles_per_seq=seq_len // tm),
        out_shape=jax.ShapeDtypeStruct((m, D_MODEL), F32),
        grid=(m // tm,),
        in_specs=[
            pl.BlockSpec((None, tm, D_FF), lambda i: (0, i, 0)),
            pl.BlockSpec((None, tm, D_FF), lambda i: (1, i, 0)),
            pl.BlockSpec((None, sub, D_FF), lambda i: (0, jnp.maximum(i * (tm // sub) - 1, 0), 0)),
            pl.BlockSpec((CONV_W, D_FF), lambda i: (0, 0)),
            pl.BlockSpec((1, D_FF), lambda i: (0, 0)),
            pl.BlockSpec((D_FF, D_MODEL), lambda i: (0, 0)),
            pl.BlockSpec((tm, D_MODEL), lambda i: (i, 0)),
        ],
        out_specs=pl.BlockSpec((tm, D_MODEL), lambda i: (i, 0)),
        compiler_params=_cparams("parallel"),
        name="ffn_prompt",
    )(gv2, gv2, gv2, conv_w, conv_b.reshape(1, D_FF), w_out, x)


def _ffn_sample_kernel(g_ref, val_ref, s0_ref, s1_ref, cw_ref, cb_ref, wo_ref, x_ref, o_ref):
    c = cb_ref[...] + cw_ref[0:1] * s0_ref[...] + cw_ref[1:2] * s1_ref[...] + cw_ref[2:3] * g_ref[...]
    o_ref[...] = x_ref[...] + _dot(_ffn_gate(c, val_ref[...]), wo_ref[...])


def _ffn_sample(gv2, state, conv_w, conv_b, w_out, x):
    m = x.shape[0]
    full = lambda i: (0, 0)
    return pl.pallas_call(
        _ffn_sample_kernel,
        out_shape=jax.ShapeDtypeStruct((m, D_MODEL), F32),
        grid=(1,),
        in_specs=[
            pl.BlockSpec((None, m, D_FF), lambda i: (0, 0, 0)),
            pl.BlockSpec((None, m, D_FF), lambda i: (1, 0, 0)),
            pl.BlockSpec((m, D_FF), full),
            pl.BlockSpec((m, D_FF), full),
            pl.BlockSpec((CONV_W, D_FF), full),
            pl.BlockSpec((1, D_FF), full),
            pl.BlockSpec((D_FF, D_MODEL), full),
            pl.BlockSpec((m, D_MODEL), full),
        ],
        out_specs=pl.BlockSpec((m, D_MODEL), full),
        compiler_params=_cparams("arbitrary"),
        name="ffn_sample",
    )(gv2, gv2, state[:, 0], state[:, 1], conv_w, conv_b.reshape(1, D_FF), w_out, x)


def _qknorm_rope_kernel(x_ref, gain_ref, c_ref, s1_ref, s2_ref, bd_ref, o_ref):
    x = x_ref[...]
    head_sumsq = _dot(_split_bf16(x * x), bd_ref[...])
    y = x * lax.rsqrt(head_sumsq * (1.0 / HEAD_DIM) + EPS) * gain_ref[...]
    reps = D_MODEL // LANES
    tile = lambda t: jnp.concatenate([t] * reps, axis=1)
    half = ROT_DIM // 2
    o_ref[...] = (y * tile(c_ref[...])
                  + pltpu.roll(y, D_MODEL - half, axis=1) * tile(s1_ref[...])
                  + pltpu.roll(y, half, axis=1) * tile(s2_ref[...]))


def _rope_tables(pos):
    half = ROT_DIM // 2
    inv_freq = ROPE_THETA ** (-jnp.arange(0, ROT_DIM, 2, dtype=F32) / ROT_DIM)
    ang = pos.astype(F32)[:, None] * inv_freq[None, :]
    cos, sin = jnp.cos(ang), jnp.sin(ang)
    n = pos.shape[0]
    pad = jnp.zeros((n, HEAD_DIM - ROT_DIM), F32)
    zeros = jnp.zeros((n, half), F32)
    c = jnp.concatenate([cos, cos, pad + 1.0], axis=1)
    s1 = jnp.concatenate([-sin, zeros, pad], axis=1)
    s2 = jnp.concatenate([zeros, sin, pad], axis=1)
    rep = lambda t: jnp.concatenate([t] * HEADS_PER_LANE_TILE, axis=1)
    return rep(c), rep(s1), rep(s2)


def _qknorm_rope(qkv3, q_gain, k_gain, pos, tm=512):
    m = qkv3.shape[1]
    t = pos.shape[0]
    tm = min(tm, m, t)
    tiles_per_seq = t // tm
    gains = jnp.stack([jnp.tile(q_gain, N_HEADS), jnp.tile(k_gain, N_HEADS)]).reshape(2, 1, D_MODEL)
    c, s1, s2 = _rope_tables(pos)
    head = jnp.arange(D_MODEL) // HEAD_DIM
    bd = (head[:, None] == head[None, :]).astype(BF16)
    bd2 = jnp.concatenate([bd, bd], axis=0)
    tab = pl.BlockSpec((tm, LANES), lambda s, i: (i % tiles_per_seq, 0))
    return pl.pallas_call(
        _qknorm_rope_kernel,
        out_shape=jax.ShapeDtypeStruct((2, m, D_MODEL), F32),
        grid=(2, m // tm),
        in_specs=[
            pl.BlockSpec((None, tm, D_MODEL), lambda s, i: (s, i, 0)),
            pl.BlockSpec((None, 1, D_MODEL), lambda s, i: (s, 0, 0)),
            tab, tab, tab,
            pl.BlockSpec((2 * D_MODEL, D_MODEL), lambda s, i: (0, 0)),
        ],
        out_specs=pl.BlockSpec((None, tm, D_MODEL), lambda s, i: (s, i, 0)),
        compiler_params=_cparams("parallel", "parallel"),
        name="qknorm_rope",
    )(qkv3, gains, c, s1, s2, bd2)


def _head_lane_masks():
    lane = lax.broadcasted_iota(jnp.int32, (1, LANES), 1)
    return [(lane // HEAD_DIM) == h for h in range(HEADS_PER_LANE_TILE)]


def _sb_prompt_kernel(q_ref, k_ref, v_ref, o_ref, *, tq):
    qi = pl.program_id(2)
    q = q_ref[...] * QK_SCALE
    u2 = _strict_upper_sum_matrix(tq)
    row = lax.broadcasted_iota(jnp.int32, (tq, tq), 0)
    col = lax.broadcasted_iota(jnp.int32, (tq, tq), 1)
    causal = col < row
    out = jnp.zeros((tq, LANES), F32)
    for hm in _head_lane_masks():
        qh = jnp.where(hm, q, 0.0).astype(BF16)

        def tile(kj, carry, diagonal):
            r, acc = carry
            start = pl.multiple_of(kj * tq, tq)
            ks = k_ref[pl.ds(start, tq), :].astype(BF16)
            vs = v_ref[pl.ds(start, tq), :].astype(BF16)
            lsp, lk = _log_sigmoid_pair(_dot_nt(qh, ks))
            if diagonal:
                lk = jnp.where(causal, lk, 0.0)
            later = _dot(_split_bf16(lk), u2) + r
            a = jnp.exp(lsp + later)
            if diagonal:
                a = jnp.where(causal, a, 0.0)
            acc = acc + _dot(a.astype(BF16), vs)
            return r + jnp.sum(lk, axis=-1, keepdims=True), acc

        carry = tile(qi, (jnp.zeros((tq, 1), F32), jnp.zeros((tq, LANES), F32)), True)
        _, acc = lax.fori_loop(0, qi, lambda it, c: tile(qi - 1 - it, c, False), carry)
        out = out + jnp.where(hm, acc, 0.0)
    o_ref[...] = out


def _attention_specs(batch, seq_len, tq, q_sel, k_sel, v_sel):
    nq = seq_len // tq
    return dict(
        grid=(batch, N_HEAD_PAIRS, nq),
        in_specs=[
            pl.BlockSpec((None, tq, LANES), lambda b, hp, qi: (q_sel, b * nq + qi, hp)),
            pl.BlockSpec((None, seq_len, LANES), lambda b, hp, qi: (k_sel, b, hp)),
            pl.BlockSpec((None, seq_len, LANES), lambda b, hp, qi: (v_sel, b, hp)),
        ],
        out_specs=pl.BlockSpec((tq, LANES), lambda b, hp, qi: (b * nq + qi, hp)),
        out_shape=jax.ShapeDtypeStruct((batch * seq_len, D_MODEL), F32),
    )


def _sb_prompt(qkv3, batch, seq_len, tq=256):
    tq = min(tq, seq_len)
    return pl.pallas_call(
        functools.partial(_sb_prompt_kernel, tq=tq),
        compiler_params=_cparams("parallel", "parallel", "arbitrary"),
        name="sb_prompt",
        **_attention_specs(batch, seq_len, tq, 0, 1, 2),
    )(qkv3, qkv3, qkv3)


def _moba_prompt_kernel(q_ref, k_ref, v_ref, o_ref, kmean_ref, *, n_blocks):
    qi = pl.program_id(2)
    blk = MOBA_BLOCK

    @pl.when(qi == 0)
    def _():
        kmean_ref[...] = jnp.zeros_like(kmean_ref)
        for n in range(n_blocks):
            kmean_ref[n:n + 1, :] = jnp.mean(k_ref[n * blk:(n + 1) * blk, :], axis=0, keepdims=True)

    q = q_ref[...] * QK_SCALE
    kmean = kmean_ref[...].astype(BF16)
    lane = lax.broadcasted_iota(jnp.int32, (blk, LANES), 1)
    row = lax.broadcasted_iota(jnp.int32, (blk, blk), 0)
    col = lax.broadcasted_iota(jnp.int32, (blk, blk), 1)
    neg_inf = -jnp.inf
    out = jnp.zeros((blk, LANES), F32)
    for hm in _head_lane_masks():
        qh = jnp.where(hm, q, 0.0).astype(BF16)
        gate = jnp.where(lane < qi, _dot_nt(qh, kmean), neg_inf)
        sel = jnp.zeros((blk, LANES), jnp.bool_)
        for r in range(MOBA_TOPK):
            best = jnp.max(gate, axis=-1, keepdims=True)
            idx = jnp.min(jnp.where(gate == best, lane, LANES), axis=-1, keepdims=True)
            pick = lane == idx
            sel = sel | (pick & (r < qi))
            gate = jnp.where(pick, neg_inf, gate)
        q_aug = jnp.concatenate([qh, jnp.where(sel, 0.0, MASK_BIAS).astype(BF16)], axis=1)

        start = pl.multiple_of(qi * blk, blk)
        s = jnp.where(col <= row, _dot_nt(qh, k_ref[pl.ds(start, blk), :].astype(BF16)), neg_inf)
        m = jnp.max(s, axis=-1, keepdims=True)
        p = jnp.exp(s - m)
        l = jnp.sum(p, axis=-1, keepdims=True)
        acc = _dot(p.astype(BF16), v_ref[pl.ds(start, blk), :].astype(BF16))

        def past_block(n, carry):
            m, l, acc = carry
            start = pl.multiple_of(n * blk, blk)
            k_aug = jnp.concatenate([k_ref[pl.ds(start, blk), :].astype(BF16),
                                     jnp.where(lane == n, 1.0, 0.0).astype(BF16)], axis=1)
            s = _dot_nt(q_aug, k_aug)
            m_new = jnp.maximum(m, jnp.max(s, axis=-1, keepdims=True))
            alpha = jnp.exp(m - m_new)
            p = jnp.exp(s - m_new)
            l = alpha * l + jnp.sum(p, axis=-1, keepdims=True)
            acc = alpha * acc + _dot(p.astype(BF16), v_ref[pl.ds(start, blk), :].astype(BF16))
            return m_new, l, acc

        m, l, acc = lax.fori_loop(0, qi, past_block, (m, l, acc))
        out = out + jnp.where(hm, acc / l, 0.0)
    o_ref[...] = out


def _moba_prompt(qk2, qkv3, batch, seq_len):
    n_blocks = seq_len // MOBA_BLOCK
    assert n_blocks <= LANES
    specs = _attention_specs(batch, seq_len, MOBA_BLOCK, 0, 1, 2)
    return pl.pallas_call(
        functools.partial(_moba_prompt_kernel, n_blocks=n_blocks),
        scratch_shapes=[pltpu.VMEM((LANES, LANES), F32)],
        compiler_params=_cparams("parallel", "arbitrary", "arbitrary"),
        name="moba_prompt",
        **specs,
    )(qk2, qk2, qkv3)


PAGES_PER_STEP = 2
KEYS_PER_STEP = PAGES_PER_STEP * PAGE_SIZE


PAGE_SHAPE = (N_HEADS, HEAD_DIM, PAGE_SIZE)
HEAD_ROW = (N_HEADS, 1, PAGE_SIZE)


def _page_view(cache):
    return jnp.transpose(cache, (0, 1, 3, 4, 2))


def _head_columns(x):
    return x.reshape(x.shape[0], N_HEADS, HEAD_DIM, 1)


def _page_scores(k_page, qb):
    return jnp.sum(k_page * qb, axis=1, keepdims=True)


def _lane_sum(x):
    return jnp.sum(x, axis=-1, keepdims=True)


def _suffix_sum_exclusive(x):
    lane = lax.broadcasted_iota(jnp.int32, x.shape, x.ndim - 1)
    inc = x
    d = 1
    while d < LANES:
        inc = inc + jnp.where(lane + d < LANES, pltpu.roll(inc, LANES - d, axis=x.ndim - 1), 0.0)
        d *= 2
    return inc - x


def _sb_decode_kernel(pt_ref, pos_ref, q_ref, kown_ref, vown_ref, k0_ref, k1_ref, v0_ref, v1_ref,
                      o_ref, qb_ref, r_ref, acc_ref):
    n = pl.program_id(1)

    @pl.when(n == 0)
    def _():
        q = q_ref[...] * QK_SCALE
        qb_ref[...] = jnp.broadcast_to(q, PAGE_SHAPE)
        valid = jnp.where(pos_ref[1] < pos_ref[0], 1.0, 0.0)
        lsp, lk = _log_sigmoid_pair(jnp.sum(q * kown_ref[...], axis=1, keepdims=True))
        r_ref[...] = jnp.broadcast_to(lk * valid, HEAD_ROW)
        lane = lax.broadcasted_iota(jnp.int32, PAGE_SHAPE, 2)
        acc_ref[...] = jnp.where(lane == 0, (jnp.exp(lsp) * valid) * vown_ref[...], 0.0)

    qb = qb_ref[...]
    r = r_ref[...]
    acc = acc_ref[...]
    for k_ref, v_ref in ((k1_ref, v1_ref), (k0_ref, v0_ref)):
        lsp, lk = _log_sigmoid_pair(_page_scores(k_ref[...], qb))
        a = jnp.exp(lsp + _suffix_sum_exclusive(lk) + r)
        acc = acc + a * v_ref[...]
        r = r + _lane_sum(lk)
    r_ref[...] = r
    acc_ref[...] = acc

    @pl.when(n == pl.num_programs(1) - 1)
    def _():
        o_ref[...] = _lane_sum(acc)


def _decode_specs(n_steps, layer, newest_first):
    def page(j):
        def index_map(b, n, pt, *_):
            step = (n_steps - 1 - n) if newest_first else n
            return (pt[b, PAGES_PER_STEP * step + j], layer, 0, 0, 0)
        return pl.BlockSpec((None, None) + PAGE_SHAPE, index_map)

    col = pl.BlockSpec((None, N_HEADS, HEAD_DIM, 1), lambda b, n, *_: (b, 0, 0, 0))
    return col, page(0), page(1)


def _sb_decode(q, k_own, v_own, cache_k, cache_v, page_table, layer, q_pos, k_own_pos):
    batch = q.shape[0]
    n_steps = page_table.shape[1] // PAGES_PER_STEP
    col, p0, p1 = _decode_specs(n_steps, layer, True)
    pos = jnp.array([q_pos, k_own_pos], jnp.int32)
    ck, cv = _page_view(cache_k), _page_view(cache_v)
    out = pl.pallas_call(
        _sb_decode_kernel,
        out_shape=jax.ShapeDtypeStruct((batch, N_HEADS, HEAD_DIM, 1), F32),
        grid_spec=pltpu.PrefetchScalarGridSpec(
            num_scalar_prefetch=2,
            grid=(batch, n_steps),
            in_specs=[col, col, col, p0, p1, p0, p1],
            out_specs=col,
            scratch_shapes=[pltpu.VMEM(PAGE_SHAPE, F32), pltpu.VMEM(HEAD_ROW, F32), pltpu.VMEM(PAGE_SHAPE, F32)],
        ),
        compiler_params=_cparams("parallel", "arbitrary"),
        name="sb_decode",
    )(page_table, pos, _head_columns(q), _head_columns(k_own), _head_columns(v_own), ck, ck, cv, cv)
    return out.reshape(batch, D_MODEL)


def _moba_decode_kernel(pt_ref, q_ref, kown_ref, vown_ref, k0_ref, k1_ref, v0_ref, v1_ref,
                        o_ref, qb_ref, gate_ref, m_ref, l_ref, pv_ref):
    n = pl.program_id(1)
    n_blocks = pl.num_programs(1)

    @pl.when(n == 0)
    def _():
        qb_ref[...] = jnp.broadcast_to(q_ref[...] * QK_SCALE, PAGE_SHAPE)

    qb = qb_ref[...]
    s0 = _page_scores(k0_ref[...], qb)
    s1 = _page_scores(k1_ref[...], qb)
    m = jnp.maximum(jnp.max(s0, axis=-1, keepdims=True), jnp.max(s1, axis=-1, keepdims=True))
    p0 = jnp.exp(s0 - m)
    p1 = jnp.exp(s1 - m)
    gate_ref[n] = jnp.broadcast_to((_lane_sum(s0) + _lane_sum(s1)) * (1.0 / MOBA_BLOCK), HEAD_ROW)
    m_ref[n] = jnp.broadcast_to(m, HEAD_ROW)
    l_ref[n] = jnp.broadcast_to(_lane_sum(p0) + _lane_sum(p1), HEAD_ROW)
    pv_ref[n] = p0 * v0_ref[...] + p1 * v1_ref[...]

    @pl.when(n == n_blocks - 1)
    def _():
        shape = gate_ref.shape
        blk = lax.broadcasted_iota(jnp.int32, shape, 0)
        gate = gate_ref[...]
        sel = jnp.zeros(shape, jnp.bool_)
        for r in range(MOBA_TOPK):
            best = jnp.max(gate, axis=0, keepdims=True)
            idx = jnp.min(jnp.where(gate == best, blk, shape[0]), axis=0, keepdims=True)
            pick = blk == idx
            sel = sel | (pick & (r < n_blocks))
            gate = jnp.where(pick, -jnp.inf, gate)
        q = q_ref[...] * QK_SCALE
        s_own = jnp.broadcast_to(jnp.sum(q * kown_ref[...], axis=1, keepdims=True), HEAD_ROW)
        m_all = m_ref[...]
        m_fin = jnp.maximum(jnp.max(jnp.where(sel, m_all, -jnp.inf), axis=0), s_own)
        w = jnp.where(sel, jnp.exp(m_all - m_fin[None]), 0.0)
        w_own = jnp.exp(s_own - m_fin)
        denom = jnp.sum(w * l_ref[...], axis=0) + w_own
        gate_ref[...] = w
        numer = lax.fori_loop(0, n_blocks, lambda i, acc: acc + gate_ref[i] * pv_ref[i],
                              jnp.zeros(PAGE_SHAPE, F32))
        o_ref[...] = (_lane_sum(numer) + w_own[:, :, 0:1] * vown_ref[...]) / denom[:, :, 0:1]


def _moba_decode(q, k_own, v_own, cache_k, cache_v, page_table, layer):
    batch = q.shape[0]
    assert MOBA_BLOCK == KEYS_PER_STEP
    n_blocks = page_table.shape[1] // PAGES_PER_STEP
    col, p0, p1 = _decode_specs(n_blocks, layer, False)
    ck, cv = _page_view(cache_k), _page_view(cache_v)
    stat = pltpu.VMEM((n_blocks,) + HEAD_ROW, F32)
    out = pl.pallas_call(
        _moba_decode_kernel,
        out_shape=jax.ShapeDtypeStruct((batch, N_HEADS, HEAD_DIM, 1), F32),
        grid_spec=pltpu.PrefetchScalarGridSpec(
            num_scalar_prefetch=1,
            grid=(batch, n_blocks),
            in_specs=[col, col, col, p0, p1, p0, p1],
            out_specs=col,
            scratch_shapes=[pltpu.VMEM(PAGE_SHAPE, F32), stat, stat, stat,
                            pltpu.VMEM((n_blocks,) + PAGE_SHAPE, F32)],
        ),
        compiler_params=_cparams("parallel", "arbitrary"),
        name="moba_decode",
    )(page_table, _head_columns(q), _head_columns(k_own), _head_columns(v_own), ck, ck, cv, cv)
    return out.reshape(batch, D_MODEL)


def kernel(x_prompt, x_sample, cache_sb_k, cache_sb_v, cache_moba_k, cache_moba_v, state_ffn_conv, page_table, p_prompt, p_sample, mix_norm, ffn_norm, ple_norm, sgu_w_in, sgu_v_norm, sgu_w_s, sgu_b_s, sgu_w_out, sb_w_qkv, sb_w_out, moba_w_qkv, moba_q_norm, moba_k_norm, moba_w_out, ffn_w_in, ffn_conv_w, ffn_conv_b, ffn_w_out, ple_w_in, ple_w_gate):
    bp, tp, _ = x_prompt.shape
    bs, ts, _ = x_sample.shape
    assert ts == 1 and tp % (2 * SGU_CHUNK) == 0 and tp % MOBA_BLOCK == 0
    depth = mix_norm.shape[0]
    past_len = page_table.shape[1] * PAGE_SIZE
    assert past_len % MOBA_BLOCK == 0 and past_len % SGU_CHUNK == 0
    xp = x_prompt.reshape(bp * tp, D_MODEL)
    xs = x_sample.reshape(bs, D_MODEL)
    bf = lambda w: w.astype(BF16)
    heads_p = lambda t: t.reshape(bp, tp, N_HEADS, HEAD_DIM)
    heads_s = lambda t: t.reshape(bs, ts, N_HEADS, HEAD_DIM)

    sb_kp, sb_vp, sb_ks, sb_vs = [], [], [], []
    mb_kp, mb_vp, mb_ks, mb_vs = [], [], [], []
    sgu_vs, conv_p, conv_s = [], [], []

    for i in range(depth):
        kind, j = i % N_MIXERS, i // N_MIXERS
        if kind == 0:
            w_in, w_out = bf(sgu_w_in[j]), bf(sgu_w_out[j])
            zp = _norm_matmul(xp, mix_norm[i], w_in, 2, act="gelu")
            zs = _norm_matmul(xs, mix_norm[i], w_in, 2, act="gelu")
            xp = _sgu_prompt(zp, sgu_v_norm[j], sgu_w_s[j], sgu_b_s[j], w_out, xp)
            xs, v_s = _sgu_sample(zs, sgu_v_norm[j], sgu_w_s[j], sgu_b_s[j], w_out, xs)
            sgu_vs.append(v_s.reshape(bs, ts, SGU_WIDTH))
        elif kind == 1:
            w_qkv, w_out = bf(sb_w_qkv[j]), bf(sb_w_out[j])
            qkv_p = _norm_matmul(xp, mix_norm[i], w_qkv, 3)
            qkv_s = _norm_matmul(xs, mix_norm[i], w_qkv, 3)
            xp = _proj_residual(_sb_prompt(qkv_p, bp, tp), w_out, xp)
            att = _sb_decode(qkv_s[0], qkv_s[1], qkv_s[2], cache_sb_k, cache_sb_v,
                             page_table, j, past_len, past_len)
            xs = _proj_residual(att, w_out, xs)
            sb_kp.append(heads_p(qkv_p[1]))
            sb_vp.append(heads_p(qkv_p[2]))
            sb_ks.append(heads_s(qkv_s[1]))
            sb_vs.append(heads_s(qkv_s[2]))
        else:
            w_qkv, w_out = bf(moba_w_qkv[j]), bf(moba_w_out[j])
            qkv_p = _norm_matmul(xp, mix_norm[i], w_qkv, 3)
            qkv_s = _norm_matmul(xs, mix_norm[i], w_qkv, 3)
            qk_p = _qknorm_rope(qkv_p, moba_q_norm[j], moba_k_norm[j], jnp.arange(tp, dtype=jnp.int32))
            qk_s = _qknorm_rope(qkv_s, moba_q_norm[j], moba_k_norm[j], jnp.full((bs,), past_len, jnp.int32))
            xp = _proj_residual(_moba_prompt(qk_p, qkv_p, bp, tp), w_out, xp)
            att = _moba_decode(qk_s[0], qk_s[1], qkv_s[2], cache_moba_k, cache_moba_v, page_table, j)
            xs = _proj_residual(att, w_out, xs)
            mb_kp.append(heads_p(qk_p[1]))
            mb_vp.append(heads_p(qkv_p[2]))
            mb_ks.append(heads_s(qk_s[1]))
            mb_vs.append(heads_s(qkv_s[2]))

        w_in, w_out = bf(ffn_w_in[i]), bf(ffn_w_out[i])
        gv_p = _norm_matmul(xp, ffn_norm[i], w_in, 2)
        gv_s = _norm_matmul(xs, ffn_norm[i], w_in, 2)
        xp = _ffn_prompt(gv_p, ffn_conv_w[i], ffn_conv_b[i], w_out, xp, tp)
        xs = _ffn_sample(gv_s, state_ffn_conv[i], ffn_conv_w[i], ffn_conv_b[i], w_out, xs)
        conv_p.append(gv_p[0].reshape(bp, tp, D_FF)[:, tp - (CONV_W - 1):])
        conv_s.append(jnp.concatenate([state_ffn_conv[i][:, 1:], gv_s[0][:, None]], axis=1))

        w_gate, w_pin = bf(ple_w_gate[i]), bf(ple_w_in[i])
        xp = _ple(xp, p_prompt[i].reshape(bp * tp, -1), ple_norm[i], w_gate, w_pin)
        xs = _ple(xs, p_sample[i].reshape(bs, -1), ple_norm[i], w_gate, w_pin)

    return (xp.reshape(bp, tp, D_MODEL), xs.reshape(bs, ts, D_MODEL),
            jnp.stack(sb_kp, axis=1), jnp.stack(sb_vp, axis=1),
            jnp.stack(sb_ks, axis=1), jnp.stack(sb_vs, axis=1),
            jnp.stack(mb_kp, axis=1), jnp.stack(mb_vp, axis=1),
            jnp.stack(mb_ks, axis=1), jnp.stack(mb_vs, axis=1),
            jnp.stack(sgu_vs, axis=0), jnp.stack(conv_p, axis=0), jnp.stack(conv_s, axis=0))
```

```python
import functools

import jax
import jax.numpy as jnp
from jax import lax
from jax.experimental import pallas as pl
from jax.experimental.pallas import tpu as pltpu

F32 = jnp.float32
BF16 = jnp.bfloat16

D_MODEL = 1024
N_HEADS = 16
HEAD_DIM = 64
PAGE_SIZE = 128
SGU_CHUNK = 128
SGU_GROUPS = 8
SGU_WIDTH = 2 * D_MODEL
SGU_GROUP_DIM = SGU_WIDTH // SGU_GROUPS
MOBA_BLOCK = 256
MOBA_TOPK = 3
ROT_DIM = HEAD_DIM // 4
ROPE_THETA = 500000.0
D_FF = 2816
CONV_W = 3
EPS = 1e-6
N_MIXERS = 3

LANES = 128
HEADS_PER_LANE_TILE = LANES // HEAD_DIM
N_HEAD_PAIRS = D_MODEL // LANES
QK_SCALE = HEAD_DIM ** -0.5
MASK_BIAS = -1e30
VMEM_LIMIT = 48 * 1024 * 1024


def _cparams(*sem):
    return pltpu.CompilerParams(dimension_semantics=sem, vmem_limit_bytes=VMEM_LIMIT)


def _rms(x, g):
    return x * lax.rsqrt(jnp.mean(x * x, axis=-1, keepdims=True) + EPS) * g


def _dot(a, b):
    return jnp.dot(a, b, preferred_element_type=F32)


def _dot_nt(a, b):
    return lax.dot_general(a, b, (((1,), (1,)), ((), ())), preferred_element_type=F32)


def _split_bf16(x):
    hi = x.astype(BF16)
    lo = (x - hi.astype(F32)).astype(BF16)
    return jnp.concatenate([hi, lo], axis=1)


def _strict_upper_sum_matrix(n):
    j = lax.broadcasted_iota(jnp.int32, (n, n), 0)
    s = lax.broadcasted_iota(jnp.int32, (n, n), 1)
    u = jnp.where(j > s, 1.0, 0.0).astype(BF16)
    return jnp.concatenate([u, u], axis=0)


def _log_sigmoid_pair(z):
    lsp = jnp.minimum(z, 0.0) - jnp.log(1.0 + jnp.exp(-jnp.abs(z)))
    return lsp, lsp - z


def _norm_matmul_kernel(x_ref, g_ref, w_ref, o_ref, xn_ref, *, act):
    @pl.when(pl.program_id(1) == 0)
    def _():
        xn_ref[...] = _rms(x_ref[...], g_ref[...]).astype(BF16)

    y = _dot(xn_ref[...], w_ref[...])
    if act == "gelu":
        y = jax.nn.gelu(y)
    o_ref[...] = y


def _lane_tile(n, cap):
    return max(t for t in range(LANES, cap + 1, LANES) if n % t == 0)


def _norm_matmul(x, gain, w, n_split, act=None, tm=512, tn_cap=1536):
    m, k = x.shape
    n_out = w.shape[1]
    per = n_out // n_split
    tm = min(tm, m)
    tn = _lane_tile(per, tn_cap)
    nj = per // tn
    assert m % tm == 0 and per % tn == 0
    return pl.pallas_call(
        functools.partial(_norm_matmul_kernel, act=act),
        out_shape=jax.ShapeDtypeStruct((n_split, m, per), F32),
        grid=(m // tm, n_out // tn),
        in_specs=[
            pl.BlockSpec((tm, k), lambda i, j: (i, 0)),
            pl.BlockSpec((1, k), lambda i, j: (0, 0)),
            pl.BlockSpec((k, tn), lambda i, j: (0, j)),
        ],
        out_specs=pl.BlockSpec((None, tm, tn), lambda i, j: (j // nj, i, j % nj)),
        scratch_shapes=[pltpu.VMEM((tm, k), BF16)],
        compiler_params=_cparams("parallel", "arbitrary"),
        name="norm_matmul",
    )(x, gain.reshape(1, k), w)


def _proj_residual_kernel(h_ref, w_ref, x_ref, o_ref):
    o_ref[...] = x_ref[...] + _dot(h_ref[...].astype(BF16), w_ref[...])


def _proj_residual(h, w, x, tm=1024):
    m, k = h.shape
    n = w.shape[1]
    tm = min(tm, m)
    return pl.pallas_call(
        _proj_residual_kernel,
        out_shape=jax.ShapeDtypeStruct((m, n), F32),
        grid=(m // tm,),
        in_specs=[
            pl.BlockSpec((tm, k), lambda i: (i, 0)),
            pl.BlockSpec((k, n), lambda i: (0, 0)),
            pl.BlockSpec((tm, n), lambda i: (i, 0)),
        ],
        out_specs=pl.BlockSpec((tm, n), lambda i: (i, 0)),
        compiler_params=_cparams("parallel"),
        name="proj_residual",
    )(h, w, x)


def _ple_kernel(x_ref, p_ref, g_ref, wg_ref, wp_ref, o_ref):
    x = x_ref[...]
    gate = jax.nn.sigmoid(_dot(_rms(x, g_ref[...]).astype(BF16), wg_ref[...]))
    o_ref[...] = x + gate * _dot(p_ref[...].astype(BF16), wp_ref[...])


def _ple(x, p, gain, w_gate, w_in, tm=512):
    m, d = x.shape
    pd = p.shape[1]
    tm = min(tm, m)
    return pl.pallas_call(
        _ple_kernel,
        out_shape=jax.ShapeDtypeStruct((m, d), F32),
        grid=(m // tm,),
        in_specs=[
            pl.BlockSpec((tm, d), lambda i: (i, 0)),
            pl.BlockSpec((tm, pd), lambda i: (i, 0)),
            pl.BlockSpec((1, d), lambda i: (0, 0)),
            pl.BlockSpec((d, d), lambda i: (0, 0)),
            pl.BlockSpec((pd, d), lambda i: (0, 0)),
        ],
        out_specs=pl.BlockSpec((tm, d), lambda i: (i, 0)),
        compiler_params=_cparams("parallel"),
        name="ple",
    )(x, p, gain.reshape(1, d), w_gate, w_in)


def _sgu_prompt_kernel(u_ref, v_ref, vg_ref, ws_ref, bst_ref, wo_ref, x_ref, o_ref, *, chunks):
    v = _rms(v_ref[...], vg_ref[...]).astype(BF16)
    i = lax.broadcasted_iota(jnp.int32, (SGU_CHUNK, SGU_CHUNK), 0)
    j = lax.broadcasted_iota(jnp.int32, (SGU_CHUNK, SGU_CHUNK), 1)
    cols = []
    for g in range(SGU_GROUPS):
        w = jnp.where(i >= j, ws_ref[g], 0.0).astype(BF16)
        bias = bst_ref[:, g:g + 1]
        lo, hi = g * SGU_GROUP_DIM, (g + 1) * SGU_GROUP_DIM
        rows = [_dot(w, v[c * SGU_CHUNK:(c + 1) * SGU_CHUNK, lo:hi]) + bias for c in range(chunks)]
        cols.append(jnp.concatenate(rows, axis=0))
    mixed = jnp.concatenate(cols, axis=1)
    gated = (u_ref[...] * mixed).astype(BF16)
    o_ref[...] = x_ref[...] + _dot(gated, wo_ref[...])


def _sgu_prompt(z2, v_gain, w_s, b_s, w_out, x, chunks=2):
    m = x.shape[0]
    tm = chunks * SGU_CHUNK
    return pl.pallas_call(
        functools.partial(_sgu_prompt_kernel, chunks=chunks),
        out_shape=jax.ShapeDtypeStruct((m, D_MODEL), F32),
        grid=(m // tm,),
        in_specs=[
            pl.BlockSpec((None, tm, SGU_WIDTH), lambda i: (0, i, 0)),
            pl.BlockSpec((None, tm, SGU_WIDTH), lambda i: (1, i, 0)),
            pl.BlockSpec((1, SGU_WIDTH), lambda i: (0, 0)),
            pl.BlockSpec((SGU_GROUPS, SGU_CHUNK, SGU_CHUNK), lambda i: (0, 0, 0)),
            pl.BlockSpec((SGU_CHUNK, SGU_GROUPS), lambda i: (0, 0)),
            pl.BlockSpec((SGU_WIDTH, D_MODEL), lambda i: (0, 0)),
            pl.BlockSpec((tm, D_MODEL), lambda i: (i, 0)),
        ],
        out_specs=pl.BlockSpec((tm, D_MODEL), lambda i: (i, 0)),
        compiler_params=_cparams("parallel"),
        name="sgu_prompt",
    )(z2, z2, v_gain.reshape(1, SGU_WIDTH), w_s, b_s.T, w_out, x)


def _sgu_sample_kernel(u_ref, v_ref, vg_ref, w0_ref, b0_ref, wo_ref, x_ref, o_ref, vn_ref):
    v = _rms(v_ref[...], vg_ref[...])
    vn_ref[...] = v
    gated = u_ref[...] * (w0_ref[...] * v.astype(BF16).astype(F32) + b0_ref[...])
    o_ref[...] = x_ref[...] + _dot(gated.astype(BF16), wo_ref[...])


def _sgu_sample(z2, v_gain, w_s, b_s, w_out, x):
    m = x.shape[0]
    w0 = jnp.repeat(w_s[:, 0, 0].astype(BF16).astype(F32), SGU_GROUP_DIM).reshape(1, SGU_WIDTH)
    b0 = jnp.repeat(b_s[:, 0], SGU_GROUP_DIM).reshape(1, SGU_WIDTH)
    row = lambda i: (0, 0)
    return pl.pallas_call(
        _sgu_sample_kernel,
        out_shape=(jax.ShapeDtypeStruct((m, D_MODEL), F32), jax.ShapeDtypeStruct((m, SGU_WIDTH), F32)),
        grid=(1,),
        in_specs=[
            pl.BlockSpec((None, m, SGU_WIDTH), lambda i: (0, 0, 0)),
            pl.BlockSpec((None, m, SGU_WIDTH), lambda i: (1, 0, 0)),
            pl.BlockSpec((1, SGU_WIDTH), row),
            pl.BlockSpec((1, SGU_WIDTH), row),
            pl.BlockSpec((1, SGU_WIDTH), row),
            pl.BlockSpec((SGU_WIDTH, D_MODEL), row),
            pl.BlockSpec((m, D_MODEL), row),
        ],
        out_specs=(pl.BlockSpec((m, D_MODEL), row), pl.BlockSpec((m, SGU_WIDTH), row)),
        compiler_params=_cparams("arbitrary"),
        name="sgu_sample",
    )(z2, z2, v_gain.reshape(1, SGU_WIDTH), w0, b0, w_out, x)


def _ffn_gate(c, val):
    return (jax.nn.silu(c) * val).astype(BF16)


def _ffn_prompt_kernel(g_ref, val_ref, prev_ref, cw_ref, cb_ref, wo_ref, x_ref, o_ref, *, tiles_per_seq):
    g = g_ref[...]
    tm = g.shape[0]
    seq_start = (pl.program_id(0) % tiles_per_seq) == 0
    keep = jnp.where(seq_start, 0.0, 1.0)
    prev = prev_ref[...] * keep
    row = lax.broadcasted_iota(jnp.int32, (tm, 1), 0)
    g1 = jnp.where(row == 0, prev[7:8], pltpu.roll(g, 1, axis=0))
    g2 = jnp.where(row == 0, prev[6:7], jnp.where(row == 1, prev[7:8], pltpu.roll(g, 2, axis=0)))
    c = cb_ref[...] + cw_ref[0:1] * g2 + cw_ref[1:2] * g1 + cw_ref[2:3] * g
    o_ref[...] = x_ref[...] + _dot(_ffn_gate(c, val_ref[...]), wo_ref[...])


def _ffn_prompt(gv2, conv_w, conv_b, w_out, x, seq_len, tm=256):
    m = x.shape[0]
    sub = 8
    return pl.pallas_call(
        functools.partial(_ffn_prompt_kernel, tiles_per_seq=seq_len // tm),
        out_shape=jax.ShapeDtypeStruct((m, D_MODEL), F32),
        grid=(m // tm,),
        in_specs=[
            pl.BlockSpec((None, tm, D_FF), lambda i: (0, i, 0)),
            pl.BlockSpec((None, tm, D_FF), lambda i: (1, i, 0)),
            pl.BlockSpec((None, sub, D_FF), lambda i: (0, jnp.maximum(i * (tm // sub) - 1, 0), 0)),
            pl.BlockSpec((CONV_W, D_FF), lambda i: (0, 0)),
            pl.BlockSpec((1, D_FF), lambda i: (0, 0)),
            pl.BlockSpec((D_FF, D_MODEL), lambda i: (0, 0)),
            pl.BlockSpec((tm, D_MODEL), lambda i: (i, 0)),
        ],
        out_specs=pl.BlockSpec((tm, D_MODEL), lambda i: (i, 0)),
        compiler_params=_cparams("parallel"),
        name="ffn_prompt",
    )(gv2, gv2, gv2, conv_w, conv_b.reshape(1, D_FF), w_out, x)


def _ffn_sample_kernel(g_ref, val_ref, s0_ref, s1_ref, cw_ref, cb_ref, wo_ref, x_ref, o_ref):
    c = cb_ref[...] + cw_ref[0:1] * s0_ref[...] + cw_ref[1:2] * s1_ref[...] + cw_ref[2:3] * g_ref[...]
    o_ref[...] = x_ref[...] + _dot(_ffn_gate(c, val_ref[...]), wo_ref[...])


def _ffn_sample(gv2, state, conv_w, conv_b, w_out, x):
    m = x.shape[0]
    full = lambda i: (0, 0)
    return pl.pallas_call(
        _ffn_sample_kernel,
        out_shape=jax.ShapeDtypeStruct((m, D_MODEL), F32),
        grid=(1,),
        in_specs=[
            pl.BlockSpec((None, m, D_FF), lambda i: (0, 0, 0)),
            pl.BlockSpec((None, m, D_FF), lambda i: (1, 0, 0)),
            pl.BlockSpec((m, D_FF), full),
            pl.BlockSpec((m, D_FF), full),
            pl.BlockSpec((CONV_W, D_FF), full),
            pl.BlockSpec((1, D_FF), full),
            pl.BlockSpec((D_FF, D_MODEL), full),
            pl.BlockSpec((m, D_MODEL), full),
        ],
        out_specs=pl.BlockSpec((m, D_MODEL), full),
        compiler_params=_cparams("arbitrary"),
        name="ffn_sample",
    )(gv2, gv2, state[:, 0], state[:, 1], conv_w, conv_b.reshape(1, D_FF), w_out, x)


def _qknorm_rope_kernel(x_ref, gain_ref, c_ref, s1_ref, s2_ref, bd_ref, o_ref):
    x = x_ref[...]
    head_sumsq = _dot(_split_bf16(x * x), bd_ref[...])
    y = x * lax.rsqrt(head_sumsq * (1.0 / HEAD_DIM) + EPS) * gain_ref[...]
    reps = D_MODEL // LANES
    tile = lambda t: jnp.concatenate([t] * reps, axis=1)
    half = ROT_DIM // 2
    o_ref[...] = (y * tile(c_ref[...])
                  + pltpu.roll(y, D_MODEL - half, axis=1) * tile(s1_ref[...])
                  + pltpu.roll(y, half, axis=1) * tile(s2_ref[...]))


def _rope_tables(pos):
    half = ROT_DIM // 2
    inv_freq = ROPE_THETA ** (-jnp.arange(0, ROT_DIM, 2, dtype=F32) / ROT_DIM)
    ang = pos.astype(F32)[:, None] * inv_freq[None, :]
    cos, sin = jnp.cos(ang), jnp.sin(ang)
    n = pos.shape[0]
    pad = jnp.zeros((n, HEAD_DIM - ROT_DIM), F32)
    zeros = jnp.zeros((n, half), F32)
    c = jnp.concatenate([cos, cos, pad + 1.0], axis=1)
    s1 = jnp.concatenate([-sin, zeros, pad], axis=1)
    s2 = jnp.concatenate([zeros, sin, pad], axis=1)
    rep = lambda t: jnp.concatenate([t] * HEADS_PER_LANE_TILE, axis=1)
    return rep(c), rep(s1), rep(s2)


def _qknorm_rope(qkv3, q_gain, k_gain, pos, tm=512):
    m = qkv3.shape[1]
    t = pos.shape[0]
    tm = min(tm, m, t)
    tiles_per_seq = t // tm
    gains = jnp.stack([jnp.tile(q_gain, N_HEADS), jnp.tile(k_gain, N_HEADS)]).reshape(2, 1, D_MODEL)
    c, s1, s2 = _rope_tables(pos)
    head = jnp.arange(D_MODEL) // HEAD_DIM
    bd = (head[:, None] == head[None, :]).astype(BF16)
    bd2 = jnp.concatenate([bd, bd], axis=0)
    tab = pl.BlockSpec((tm, LANES), lambda s, i: (i % tiles_per_seq, 0))
    return pl.pallas_call(
        _qknorm_rope_kernel,
        out_shape=jax.ShapeDtypeStruct((2, m, D_MODEL), F32),
        grid=(2, m // tm),
        in_specs=[
            pl.BlockSpec((None, tm, D_MODEL), lambda s, i: (s, i, 0)),
            pl.BlockSpec((None, 1, D_MODEL), lambda s, i: (s, 0, 0)),
            tab, tab, tab,
            pl.BlockSpec((2 * D_MODEL, D_MODEL), lambda s, i: (0, 0)),
        ],
        out_specs=pl.BlockSpec((None, tm, D_MODEL), lambda s, i: (s, i, 0)),
        compiler_params=_cparams("parallel", "parallel"),
        name="qknorm_rope",
    )(qkv3, gains, c, s1, s2, bd2)


def _head_lane_masks():
    lane = lax.broadcasted_iota(jnp.int32, (1, LANES), 1)
    return [(lane // HEAD_DIM) == h for h in range(HEADS_PER_LANE_TILE)]


def _stack_heads(q):
    return jnp.concatenate([jnp.where(hm, q, 0.0) for hm in _head_lane_masks()], axis=0).astype(BF16)


def _unstack_heads(acc, tq):
    return sum(jnp.where(hm, acc[h * tq:(h + 1) * tq], 0.0) for h, hm in enumerate(_head_lane_masks()))


def _sb_prompt_kernel(q_ref, k_ref, v_ref, o_ref, *, tq, tk):
    qi = pl.program_id(2)
    qs = _stack_heads(q_ref[...] * QK_SCALE)
    rows = qs.shape[0]
    u2 = _strict_upper_sum_matrix(tk)
    q_pos = lax.broadcasted_iota(jnp.int32, (rows, tk), 0) & (tq - 1)
    k_pos = lax.broadcasted_iota(jnp.int32, (rows, tk), 1)
    tiles_per_q = tq // tk

    def tile(kj, carry, diagonal_offset=None):
        r, acc = carry
        start = pl.multiple_of(kj * tk, tk)
        ks = k_ref[pl.ds(start, tk), :].astype(BF16)
        vs = v_ref[pl.ds(start, tk), :].astype(BF16)
        lsp, lk = _log_sigmoid_pair(_dot_nt(qs, ks))
        if diagonal_offset is not None:
            causal = k_pos + diagonal_offset < q_pos
            lk = jnp.where(causal, lk, 0.0)
        a = jnp.exp(lsp + _dot(_split_bf16(lk), u2) + r)
        if diagonal_offset is not None:
            a = jnp.where(causal, a, 0.0)
        return r + jnp.sum(lk, axis=-1, keepdims=True), acc + _dot(a.astype(BF16), vs)

    carry = jnp.zeros((rows, 1), F32), jnp.zeros((rows, LANES), F32)
    first_past = qi * tiles_per_q
    for d in reversed(range(tiles_per_q)):
        carry = tile(first_past + d, carry, d * tk)
    _, acc = lax.fori_loop(0, first_past, lambda it, c: tile(first_past - 1 - it, c), carry)
    o_ref[...] = _unstack_heads(acc, tq)


def _attention_specs(batch, seq_len, tq, q_sel, k_sel, v_sel):
    nq = seq_len // tq
    return dict(
        grid=(batch, N_HEAD_PAIRS, nq),
        in_specs=[
            pl.BlockSpec((None, tq, LANES), lambda b, hp, qi: (q_sel, b * nq + qi, hp)),
            pl.BlockSpec((None, seq_len, LANES), lambda b, hp, qi: (k_sel, b, hp)),
            pl.BlockSpec((None, seq_len, LANES), lambda b, hp, qi: (v_sel, b, hp)),
        ],
        out_specs=pl.BlockSpec((tq, LANES), lambda b, hp, qi: (b * nq + qi, hp)),
        out_shape=jax.ShapeDtypeStruct((batch * seq_len, D_MODEL), F32),
    )


def _sb_prompt(qkv3, batch, seq_len, tq=512, tk=256):
    tq = min(tq, seq_len)
    assert tq % tk == 0 and tq & (tq - 1) == 0
    return pl.pallas_call(
        functools.partial(_sb_prompt_kernel, tq=tq, tk=tk),
        compiler_params=_cparams("parallel", "parallel", "arbitrary"),
        name="sb_prompt",
        **_attention_specs(batch, seq_len, tq, 0, 1, 2),
    )(qkv3, qkv3, qkv3)


def _moba_prompt_kernel(q_ref, k_ref, v_ref, o_ref, kmean_ref, *, n_blocks):
    qi = pl.program_id(2)
    blk = MOBA_BLOCK

    @pl.when(qi == 0)
    def _():
        kmean_ref[...] = jnp.zeros_like(kmean_ref)
        for n in range(n_blocks):
            kmean_ref[n:n + 1, :] = jnp.mean(k_ref[n * blk:(n + 1) * blk, :], axis=0, keepdims=True)

    qs = _stack_heads(q_ref[...] * QK_SCALE)
    rows = qs.shape[0]
    neg_inf = -jnp.inf
    lane = lax.broadcasted_iota(jnp.int32, (rows, LANES), 1)
    gate = jnp.where(lane < qi, _dot_nt(qs, kmean_ref[...].astype(BF16)), neg_inf)
    sel = jnp.zeros((rows, LANES), jnp.bool_)
    for r in range(MOBA_TOPK):
        best = jnp.max(gate, axis=-1, keepdims=True)
        idx = jnp.min(jnp.where(gate == best, lane, LANES), axis=-1, keepdims=True)
        pick = lane == idx
        sel = sel | (pick & (r < qi))
        gate = jnp.where(pick, neg_inf, gate)
    q_aug = jnp.concatenate([qs, jnp.where(sel, 0.0, MASK_BIAS).astype(BF16)], axis=1)

    q_pos = lax.broadcasted_iota(jnp.int32, (rows, blk), 0) & (blk - 1)
    k_pos = lax.broadcasted_iota(jnp.int32, (rows, blk), 1)
    start = pl.multiple_of(qi * blk, blk)
    s = jnp.where(k_pos <= q_pos, _dot_nt(qs, k_ref[pl.ds(start, blk), :].astype(BF16)), neg_inf)
    m = jnp.max(s, axis=-1, keepdims=True)
    p = jnp.exp(s - m)
    l = jnp.sum(p, axis=-1, keepdims=True)
    acc = _dot(p.astype(BF16), v_ref[pl.ds(start, blk), :].astype(BF16))

    pair = 2 * blk
    k_row = lax.broadcasted_iota(jnp.int32, (pair, LANES), 0)
    k_lane = lax.broadcasted_iota(jnp.int32, (pair, LANES), 1)
    block_in_pair = jnp.where(k_row >= blk, 1, 0)

    def past_pair(i, carry):
        m, l, acc = carry
        start = pl.multiple_of(i * pair, pair)
        one_hot = jnp.where(k_lane == 2 * i + block_in_pair, 1.0, 0.0).astype(BF16)
        k_aug = jnp.concatenate([k_ref[pl.ds(start, pair), :].astype(BF16), one_hot], axis=1)
        s = _dot_nt(q_aug, k_aug)
        m_new = jnp.maximum(m, jnp.max(s, axis=-1, keepdims=True))
        alpha = jnp.exp(m - m_new)
        p = jnp.exp(s - m_new)
        l = alpha * l + jnp.sum(p, axis=-1, keepdims=True)
        acc = alpha * acc + _dot(p.astype(BF16), v_ref[pl.ds(start, pair), :].astype(BF16))
        return m_new, l, acc

    m, l, acc = lax.fori_loop(0, (qi + 1) // 2, past_pair, (m, l, acc))
    o_ref[...] = _unstack_heads(acc / l, blk)


def _moba_prompt(qk2, qkv3, batch, seq_len):
    n_blocks = seq_len // MOBA_BLOCK
    assert n_blocks <= LANES
    specs = _attention_specs(batch, seq_len, MOBA_BLOCK, 0, 1, 2)
    return pl.pallas_call(
        functools.partial(_moba_prompt_kernel, n_blocks=n_blocks),
        scratch_shapes=[pltpu.VMEM((LANES, LANES), F32)],
        compiler_params=_cparams("parallel", "arbitrary", "arbitrary"),
        name="moba_prompt",
        **specs,
    )(qk2, qk2, qkv3)


PAGES_PER_STEP = 2
KEYS_PER_STEP = PAGES_PER_STEP * PAGE_SIZE


PAGE_SHAPE = (N_HEADS, HEAD_DIM, PAGE_SIZE)
HEAD_ROW = (N_HEADS, PAGE_SIZE)


def _page_view(cache):
    return jnp.transpose(cache, (0, 1, 3, 4, 2))


def _head_columns(x):
    return x.reshape(x.shape[0], N_HEADS, HEAD_DIM, 1)


def _page_scores(k_page, qb):
    return jnp.sum(k_page * qb, axis=1)


def _over_dims(x):
    return x[:, None, :]


def _lane_sum(x):
    return jnp.sum(x, axis=-1, keepdims=True)


def _suffix_sum_exclusive(x):
    lane = lax.broadcasted_iota(jnp.int32, x.shape, x.ndim - 1)
    inc = x
    d = 1
    while d < LANES:
        inc = inc + jnp.where(lane + d < LANES, pltpu.roll(inc, LANES - d, axis=x.ndim - 1), 0.0)
        d *= 2
    return inc - x


def _sb_decode_kernel(pt_ref, pos_ref, q_ref, kown_ref, vown_ref, k0_ref, k1_ref, v0_ref, v1_ref,
                      o_ref, qb_ref, r_ref, acc_ref):
    n = pl.program_id(1)

    @pl.when(n == 0)
    def _():
        q = q_ref[...] * QK_SCALE
        qb_ref[...] = jnp.broadcast_to(q, PAGE_SHAPE)
        valid = jnp.where(pos_ref[1] < pos_ref[0], 1.0, 0.0)
        lsp, lk = _log_sigmoid_pair(jnp.sum(q * kown_ref[...], axis=1))
        r_ref[...] = jnp.broadcast_to(lk * valid, HEAD_ROW)
        lane = lax.broadcasted_iota(jnp.int32, PAGE_SHAPE, 2)
        acc_ref[...] = jnp.where(lane == 0, _over_dims(jnp.exp(lsp) * valid) * vown_ref[...], 0.0)

    qb = qb_ref[...]
    r = r_ref[...]
    acc = acc_ref[...]
    for k_ref, v_ref in ((k1_ref, v1_ref), (k0_ref, v0_ref)):
        lsp, lk = _log_sigmoid_pair(_page_scores(k_ref[...], qb))
        a = jnp.exp(lsp + _suffix_sum_exclusive(lk) + r)
        acc = acc + _over_dims(a) * v_ref[...]
        r = r + _lane_sum(lk)
    r_ref[...] = r
    acc_ref[...] = acc

    @pl.when(n == pl.num_programs(1) - 1)
    def _():
        o_ref[...] = _lane_sum(acc)


def _decode_specs(n_steps, layer, newest_first):
    def page(j):
        def index_map(b, n, pt, *_):
            step = (n_steps - 1 - n) if newest_first else n
            return (pt[b, PAGES_PER_STEP * step + j], layer, 0, 0, 0)
        return pl.BlockSpec((None, None) + PAGE_SHAPE, index_map)

    col = pl.BlockSpec((None, N_HEADS, HEAD_DIM, 1), lambda b, n, *_: (b, 0, 0, 0))
    return col, page(0), page(1)


def _sb_decode(q, k_own, v_own, cache_k, cache_v, page_table, layer, q_pos, k_own_pos):
    batch = q.shape[0]
    n_steps = page_table.shape[1] // PAGES_PER_STEP
    col, p0, p1 = _decode_specs(n_steps, layer, True)
    pos = jnp.array([q_pos, k_own_pos], jnp.int32)
    ck, cv = _page_view(cache_k), _page_view(cache_v)
    out = pl.pallas_call(
        _sb_decode_kernel,
        out_shape=jax.ShapeDtypeStruct((batch, N_HEADS, HEAD_DIM, 1), F32),
        grid_spec=pltpu.PrefetchScalarGridSpec(
            num_scalar_prefetch=2,
            grid=(batch, n_steps),
            in_specs=[col, col, col, p0, p1, p0, p1],
            out_specs=col,
            scratch_shapes=[pltpu.VMEM(PAGE_SHAPE, F32), pltpu.VMEM(HEAD_ROW, F32), pltpu.VMEM(PAGE_SHAPE, F32)],
        ),
        compiler_params=_cparams("parallel", "arbitrary"),
        name="sb_decode",
    )(page_table, pos, _head_columns(q), _head_columns(k_own), _head_columns(v_own), ck, ck, cv, cv)
    return out.reshape(batch, D_MODEL)


def _moba_decode_kernel(pt_ref, q_ref, kown_ref, vown_ref, k0_ref, k1_ref, v0_ref, v1_ref,
                        o_ref, qb_ref, gate_ref, m_ref, l_ref, pv_ref):
    n = pl.program_id(1)
    n_blocks = pl.num_programs(1)

    @pl.when(n == 0)
    def _():
        qb_ref[...] = jnp.broadcast_to(q_ref[...] * QK_SCALE, PAGE_SHAPE)

    qb = qb_ref[...]
    s0 = _page_scores(k0_ref[...], qb)
    s1 = _page_scores(k1_ref[...], qb)
    m = jnp.maximum(jnp.max(s0, axis=-1, keepdims=True), jnp.max(s1, axis=-1, keepdims=True))
    p0 = jnp.exp(s0 - m)
    p1 = jnp.exp(s1 - m)
    gate_ref[n] = jnp.broadcast_to((_lane_sum(s0) + _lane_sum(s1)) * (1.0 / MOBA_BLOCK), HEAD_ROW)
    m_ref[n] = jnp.broadcast_to(m, HEAD_ROW)
    l_ref[n] = jnp.broadcast_to(_lane_sum(p0) + _lane_sum(p1), HEAD_ROW)
    pv_ref[n] = _over_dims(p0) * v0_ref[...] + _over_dims(p1) * v1_ref[...]

    @pl.when(n == n_blocks - 1)
    def _():
        shape = gate_ref.shape
        blk = lax.broadcasted_iota(jnp.int32, shape, 0)
        gate = gate_ref[...]
        sel = jnp.zeros(shape, jnp.bool_)
        for r in range(MOBA_TOPK):
            best = jnp.max(gate, axis=0, keepdims=True)
            idx = jnp.min(jnp.where(gate == best, blk, shape[0]), axis=0, keepdims=True)
            pick = blk == idx
            sel = sel | (pick & (r < n_blocks))
            gate = jnp.where(pick, -jnp.inf, gate)
        q = q_ref[...] * QK_SCALE
        s_own = jnp.broadcast_to(jnp.sum(q * kown_ref[...], axis=1), HEAD_ROW)
        m_all = m_ref[...]
        m_fin = jnp.maximum(jnp.max(jnp.where(sel, m_all, -jnp.inf), axis=0), s_own)
        w = jnp.where(sel, jnp.exp(m_all - m_fin[None]), 0.0)
        w_own = jnp.exp(s_own - m_fin)
        denom = jnp.sum(w * l_ref[...], axis=0) + w_own
        gate_ref[...] = w
        numer = lax.fori_loop(0, n_blocks, lambda i, acc: acc + _over_dims(gate_ref[i]) * pv_ref[i],
                              jnp.zeros(PAGE_SHAPE, F32))
        first_lane = lambda t: _over_dims(t)[:, :, 0:1]
        o_ref[...] = (_lane_sum(numer) + first_lane(w_own) * vown_ref[...]) / first_lane(denom)


def _moba_decode(q, k_own, v_own, cache_k, cache_v, page_table, layer):
    batch = q.shape[0]
    assert MOBA_BLOCK == KEYS_PER_STEP
    n_blocks = page_table.shape[1] // PAGES_PER_STEP
    col, p0, p1 = _decode_specs(n_blocks, layer, False)
    ck, cv = _page_view(cache_k), _page_view(cache_v)
    stat = pltpu.VMEM((n_blocks,) + HEAD_ROW, F32)
    out = pl.pallas_call(
        _moba_decode_kernel,
        out_shape=jax.ShapeDtypeStruct((batch, N_HEADS, HEAD_DIM, 1), F32),
        grid_spec=pltpu.PrefetchScalarGridSpec(
            num_scalar_prefetch=1,
            grid=(batch, n_blocks),
            in_specs=[col, col, col, p0, p1, p0, p1],
            out_specs=col,
            scratch_shapes=[pltpu.VMEM(PAGE_SHAPE, F32), stat, stat, stat,
                            pltpu.VMEM((n_blocks,) + PAGE_SHAPE, F32)],
        ),
        compiler_params=_cparams("parallel", "arbitrary"),
        name="moba_decode",
    )(page_table, _head_columns(q), _head_columns(k_own), _head_columns(v_own), ck, ck, cv, cv)
    return out.reshape(batch, D_MODEL)


def kernel(x_prompt, x_sample, cache_sb_k, cache_sb_v, cache_moba_k, cache_moba_v, state_ffn_conv, page_table, p_prompt, p_sample, mix_norm, ffn_norm, ple_norm, sgu_w_in, sgu_v_norm, sgu_w_s, sgu_b_s, sgu_w_out, sb_w_qkv, sb_w_out, moba_w_qkv, moba_q_norm, moba_k_norm, moba_w_out, ffn_w_in, ffn_conv_w, ffn_conv_b, ffn_w_out, ple_w_in, ple_w_gate):
    bp, tp, _ = x_prompt.shape
    bs, ts, _ = x_sample.shape
    assert ts == 1 and tp % (2 * SGU_CHUNK) == 0 and tp % MOBA_BLOCK == 0
    depth = mix_norm.shape[0]
    past_len = page_table.shape[1] * PAGE_SIZE
    assert past_len % MOBA_BLOCK == 0 and past_len % SGU_CHUNK == 0
    xp = x_prompt.reshape(bp * tp, D_MODEL)
    xs = x_sample.reshape(bs, D_MODEL)
    bf = lambda w: w.astype(BF16)
    heads_p = lambda t: t.reshape(bp, tp, N_HEADS, HEAD_DIM)
    heads_s = lambda t: t.reshape(bs, ts, N_HEADS, HEAD_DIM)

    sb_kp, sb_vp, sb_ks, sb_vs = [], [], [], []
    mb_kp, mb_vp, mb_ks, mb_vs = [], [], [], []
    sgu_vs, conv_p, conv_s = [], [], []

    for i in range(depth):
        kind, j = i % N_MIXERS, i // N_MIXERS
        if kind == 0:
            w_in, w_out = bf(sgu_w_in[j]), bf(sgu_w_out[j])
            zp = _norm_matmul(xp, mix_norm[i], w_in, 2, act="gelu")
            zs = _norm_matmul(xs, mix_norm[i], w_in, 2, act="gelu")
            xp = _sgu_prompt(zp, sgu_v_norm[j], sgu_w_s[j], sgu_b_s[j], w_out, xp)
            xs, v_s = _sgu_sample(zs, sgu_v_norm[j], sgu_w_s[j], sgu_b_s[j], w_out, xs)
            sgu_vs.append(v_s.reshape(bs, ts, SGU_WIDTH))
        elif kind == 1:
            w_qkv, w_out = bf(sb_w_qkv[j]), bf(sb_w_out[j])
            qkv_p = _norm_matmul(xp, mix_norm[i], w_qkv, 3)
            qkv_s = _norm_matmul(xs, mix_norm[i], w_qkv, 3)
            xp = _proj_residual(_sb_prompt(qkv_p, bp, tp), w_out, xp)
            att = _sb_decode(qkv_s[0], qkv_s[1], qkv_s[2], cache_sb_k, cache_sb_v,
                             page_table, j, past_len, past_len)
            xs = _proj_residual(att, w_out, xs)
            sb_kp.append(heads_p(qkv_p[1]))
            sb_vp.append(heads_p(qkv_p[2]))
            sb_ks.append(heads_s(qkv_s[1]))
            sb_vs.append(heads_s(qkv_s[2]))
        else:
            w_qkv, w_out = bf(moba_w_qkv[j]), bf(moba_w_out[j])
            qkv_p = _norm_matmul(xp, mix_norm[i], w_qkv, 3)
            qkv_s = _norm_matmul(xs, mix_norm[i], w_qkv, 3)
            qk_p = _qknorm_rope(qkv_p, moba_q_norm[j], moba_k_norm[j], jnp.arange(tp, dtype=jnp.int32))
            qk_s = _qknorm_rope(qkv_s, moba_q_norm[j], moba_k_norm[j], jnp.full((bs,), past_len, jnp.int32))
            xp = _proj_residual(_moba_prompt(qk_p, qkv_p, bp, tp), w_out, xp)
            att = _moba_decode(qk_s[0], qk_s[1], qkv_s[2], cache_moba_k, cache_moba_v, page_table, j)
            xs = _proj_residual(att, w_out, xs)
            mb_kp.append(heads_p(qk_p[1]))
            mb_vp.append(heads_p(qkv_p[2]))
            mb_ks.append(heads_s(qk_s[1]))
            mb_vs.append(heads_s(qkv_s[2]))

        w_in, w_out = bf(ffn_w_in[i]), bf(ffn_w_out[i])
        gv_p = _norm_matmul(xp, ffn_norm[i], w_in, 2)
        gv_s = _norm_matmul(xs, ffn_norm[i], w_in, 2)
        xp = _ffn_prompt(gv_p, ffn_conv_w[i], ffn_conv_b[i], w_out, xp, tp)
        xs = _ffn_sample(gv_s, state_ffn_conv[i], ffn_conv_w[i], ffn_conv_b[i], w_out, xs)
        conv_p.append(gv_p[0].reshape(bp, tp, D_FF)[:, tp - (CONV_W - 1):])
        conv_s.append(jnp.concatenate([state_ffn_conv[i][:, 1:], gv_s[0][:, None]], axis=1))

        w_gate, w_pin = bf(ple_w_gate[i]), bf(ple_w_in[i])
        xp = _ple(xp, p_prompt[i].reshape(bp * tp, -1), ple_norm[i], w_gate, w_pin)
        xs = _ple(xs, p_sample[i].reshape(bs, -1), ple_norm[i], w_gate, w_pin)

    return (xp.reshape(bp, tp, D_MODEL), xs.reshape(bs, ts, D_MODEL),
            jnp.stack(sb_kp, axis=1), jnp.stack(sb_vp, axis=1),
            jnp.stack(sb_ks, axis=1), jnp.stack(sb_vs, axis=1),
            jnp.stack(mb_kp, axis=1), jnp.stack(mb_vp, axis=1),
            jnp.stack(mb_ks, axis=1), jnp.stack(mb_vs, axis=1),
            jnp.stack(sgu_vs, axis=0), jnp.stack(conv_p, axis=0), jnp.stack(conv_s, axis=0))
```

```python
import functools

import jax
import jax.numpy as jnp
from jax import lax
from jax.experimental import pallas as pl
from jax.experimental.pallas import tpu as pltpu

F32 = jnp.float32
BF16 = jnp.bfloat16

D_MODEL = 1024
N_HEADS = 16
HEAD_DIM = 64
PAGE_SIZE = 128
SGU_CHUNK = 128
SGU_GROUPS = 8
SGU_WIDTH = 2 * D_MODEL
SGU_GROUP_DIM = SGU_WIDTH // SGU_GROUPS
MOBA_BLOCK = 256
MOBA_TOPK = 3
ROT_DIM = HEAD_DIM // 4
ROPE_THETA = 500000.0
D_FF = 2816
CONV_W = 3
EPS = 1e-6
N_MIXERS = 3

LANES = 128
HEADS_PER_LANE_TILE = LANES // HEAD_DIM
N_HEAD_PAIRS = D_MODEL // LANES
QK_SCALE = HEAD_DIM ** -0.5
MASK_BIAS = -1e30
VMEM_LIMIT = 48 * 1024 * 1024


def _cparams(*sem):
    return pltpu.CompilerParams(dimension_semantics=sem, vmem_limit_bytes=VMEM_LIMIT)


def _rms(x, g):
    return x * lax.rsqrt(jnp.mean(x * x, axis=-1, keepdims=True) + EPS) * g


def _dot(a, b):
    return jnp.dot(a, b, preferred_element_type=F32)


def _dot_nt(a, b):
    return lax.dot_general(a, b, (((1,), (1,)), ((), ())), preferred_element_type=F32)


def _split_bf16(x):
    hi = x.astype(BF16)
    lo = (x - hi.astype(F32)).astype(BF16)
    return jnp.concatenate([hi, lo], axis=1)


def _strict_upper_sum_matrix(n):
    j = lax.broadcasted_iota(jnp.int32, (n, n + LANES), 0)
    s = lax.broadcasted_iota(jnp.int32, (n, n + LANES), 1)
    return jnp.where((j > s) | (s >= n), 1.0, 0.0).astype(BF16)


def _log_sigmoid_pair(z):
    lsp = jnp.minimum(z, 0.0) - jnp.log(1.0 + jnp.exp(-jnp.abs(z)))
    return lsp, lsp - z


def _norm_matmul_kernel(x_ref, g_ref, w_ref, o_ref, xn_ref, *, act):
    @pl.when(pl.program_id(1) == 0)
    def _():
        xn_ref[...] = _rms(x_ref[...], g_ref[...]).astype(BF16)

    y = _dot(xn_ref[...], w_ref[...])
    if act == "gelu":
        y = jax.nn.gelu(y)
    o_ref[...] = y


def _lane_tile(n, cap):
    return max(t for t in range(LANES, cap + 1, LANES) if n % t == 0)


def _norm_matmul(x, gain, w, n_split, act=None, tm=512, tn_cap=1536):
    m, k = x.shape
    n_out = w.shape[1]
    per = n_out // n_split
    tm = min(tm, m)
    tn = _lane_tile(per, tn_cap)
    nj = per // tn
    assert m % tm == 0 and per % tn == 0
    return pl.pallas_call(
        functools.partial(_norm_matmul_kernel, act=act),
        out_shape=jax.ShapeDtypeStruct((n_split, m, per), F32),
        grid=(m // tm, n_out // tn),
        in_specs=[
            pl.BlockSpec((tm, k), lambda i, j: (i, 0)),
            pl.BlockSpec((1, k), lambda i, j: (0, 0)),
            pl.BlockSpec((k, tn), lambda i, j: (0, j)),
        ],
        out_specs=pl.BlockSpec((None, tm, tn), lambda i, j: (j // nj, i, j % nj)),
        scratch_shapes=[pltpu.VMEM((tm, k), BF16)],
        compiler_params=_cparams("parallel", "arbitrary"),
        name="norm_matmul",
    )(x, gain.reshape(1, k), w)


def _proj_residual_kernel(h_ref, w_ref, x_ref, o_ref):
    o_ref[...] = x_ref[...] + _dot(h_ref[...].astype(BF16), w_ref[...])


def _proj_residual(h, w, x, tm=1024):
    m, k = h.shape
    n = w.shape[1]
    tm = min(tm, m)
    return pl.pallas_call(
        _proj_residual_kernel,
        out_shape=jax.ShapeDtypeStruct((m, n), F32),
        grid=(m // tm,),
        in_specs=[
            pl.BlockSpec((tm, k), lambda i: (i, 0)),
            pl.BlockSpec((k, n), lambda i: (0, 0)),
            pl.BlockSpec((tm, n), lambda i: (i, 0)),
        ],
        out_specs=pl.BlockSpec((tm, n), lambda i: (i, 0)),
        compiler_params=_cparams("parallel"),
        name="proj_residual",
    )(h, w, x)


def _ple_kernel(x_ref, p_ref, g_ref, wg_ref, wp_ref, o_ref):
    x = x_ref[...]
    gate = jax.nn.sigmoid(_dot(_rms(x, g_ref[...]).astype(BF16), wg_ref[...]))
    o_ref[...] = x + gate * _dot(p_ref[...].astype(BF16), wp_ref[...])


def _ple(x, p, gain, w_gate, w_in, tm=512):
    m, d = x.shape
    pd = p.shape[1]
    tm = min(tm, m)
    return pl.pallas_call(
        _ple_kernel,
        out_shape=jax.ShapeDtypeStruct((m, d), F32),
        grid=(m // tm,),
        in_specs=[
            pl.BlockSpec((tm, d), lambda i: (i, 0)),
            pl.BlockSpec((tm, pd), lambda i: (i, 0)),
            pl.BlockSpec((1, d), lambda i: (0, 0)),
            pl.BlockSpec((d, d), lambda i: (0, 0)),
            pl.BlockSpec((pd, d), lambda i: (0, 0)),
        ],
        out_specs=pl.BlockSpec((tm, d), lambda i: (i, 0)),
        compiler_params=_cparams("parallel"),
        name="ple",
    )(x, p, gain.reshape(1, d), w_gate, w_in)


def _sgu_prompt_kernel(u_ref, v_ref, vg_ref, ws_ref, bst_ref, wo_ref, x_ref, o_ref, *, chunks):
    v = _rms(v_ref[...], vg_ref[...]).astype(BF16)
    i = lax.broadcasted_iota(jnp.int32, (SGU_CHUNK, SGU_CHUNK), 0)
    j = lax.broadcasted_iota(jnp.int32, (SGU_CHUNK, SGU_CHUNK), 1)
    cols = []
    for g in range(SGU_GROUPS):
        w = jnp.where(i >= j, ws_ref[g], 0.0).astype(BF16)
        bias = bst_ref[:, g:g + 1]
        lo, hi = g * SGU_GROUP_DIM, (g + 1) * SGU_GROUP_DIM
        rows = [_dot(w, v[c * SGU_CHUNK:(c + 1) * SGU_CHUNK, lo:hi]) + bias for c in range(chunks)]
        cols.append(jnp.concatenate(rows, axis=0))
    mixed = jnp.concatenate(cols, axis=1)
    gated = (u_ref[...] * mixed).astype(BF16)
    o_ref[...] = x_ref[...] + _dot(gated, wo_ref[...])


def _sgu_prompt(z2, v_gain, w_s, b_s, w_out, x, chunks=2):
    m = x.shape[0]
    tm = chunks * SGU_CHUNK
    return pl.pallas_call(
        functools.partial(_sgu_prompt_kernel, chunks=chunks),
        out_shape=jax.ShapeDtypeStruct((m, D_MODEL), F32),
        grid=(m // tm,),
        in_specs=[
            pl.BlockSpec((None, tm, SGU_WIDTH), lambda i: (0, i, 0)),
            pl.BlockSpec((None, tm, SGU_WIDTH), lambda i: (1, i, 0)),
            pl.BlockSpec((1, SGU_WIDTH), lambda i: (0, 0)),
            pl.BlockSpec((SGU_GROUPS, SGU_CHUNK, SGU_CHUNK), lambda i: (0, 0, 0)),
            pl.BlockSpec((SGU_CHUNK, SGU_GROUPS), lambda i: (0, 0)),
            pl.BlockSpec((SGU_WIDTH, D_MODEL), lambda i: (0, 0)),
            pl.BlockSpec((tm, D_MODEL), lambda i: (i, 0)),
        ],
        out_specs=pl.BlockSpec((tm, D_MODEL), lambda i: (i, 0)),
        compiler_params=_cparams("parallel"),
        name="sgu_prompt",
    )(z2, z2, v_gain.reshape(1, SGU_WIDTH), w_s, b_s.T, w_out, x)


def _sgu_sample_kernel(u_ref, v_ref, vg_ref, w0_ref, b0_ref, wo_ref, x_ref, o_ref, vn_ref):
    v = _rms(v_ref[...], vg_ref[...])
    vn_ref[...] = v
    gated = u_ref[...] * (w0_ref[...] * v.astype(BF16).astype(F32) + b0_ref[...])
    o_ref[...] = x_ref[...] + _dot(gated.astype(BF16), wo_ref[...])


def _sgu_sample(z2, v_gain, w_s, b_s, w_out, x):
    m = x.shape[0]
    w0 = jnp.repeat(w_s[:, 0, 0].astype(BF16).astype(F32), SGU_GROUP_DIM).reshape(1, SGU_WIDTH)
    b0 = jnp.repeat(b_s[:, 0], SGU_GROUP_DIM).reshape(1, SGU_WIDTH)
    row = lambda i: (0, 0)
    return pl.pallas_call(
        _sgu_sample_kernel,
        out_shape=(jax.ShapeDtypeStruct((m, D_MODEL), F32), jax.ShapeDtypeStruct((m, SGU_WIDTH), F32)),
        grid=(1,),
        in_specs=[
            pl.BlockSpec((None, m, SGU_WIDTH), lambda i: (0, 0, 0)),
            pl.BlockSpec((None, m, SGU_WIDTH), lambda i: (1, 0, 0)),
            pl.BlockSpec((1, SGU_WIDTH), row),
            pl.BlockSpec((1, SGU_WIDTH), row),
            pl.BlockSpec((1, SGU_WIDTH), row),
            pl.BlockSpec((SGU_WIDTH, D_MODEL), row),
            pl.BlockSpec((m, D_MODEL), row),
        ],
        out_specs=(pl.BlockSpec((m, D_MODEL), row), pl.BlockSpec((m, SGU_WIDTH), row)),
        compiler_params=_cparams("arbitrary"),
        name="sgu_sample",
    )(z2, z2, v_gain.reshape(1, SGU_WIDTH), w0, b0, w_out, x)


def _ffn_gate(c, val):
    return (jax.nn.silu(c) * val).astype(BF16)


def _ffn_prompt_kernel(g_ref, val_ref, prev_ref, cw_ref, cb_ref, wo_ref, x_ref, o_ref, *, tiles_per_seq):
    g = g_ref[...]
    tm = g.shape[0]
    seq_start = (pl.program_id(0) % tiles_per_seq) == 0
    keep = jnp.where(seq_start, 0.0, 1.0)
    prev = prev_ref[...] * keep
    row = lax.broadcasted_iota(jnp.int32, (tm, 1), 0)
    g1 = jnp.where(row == 0, prev[7:8], pltpu.roll(g, 1, axis=0))
    g2 = jnp.where(row == 0, prev[6:7], jnp.where(row == 1, prev[7:8], pltpu.roll(g, 2, axis=0)))
    c = cb_ref[...] + cw_ref[0:1] * g2 + cw_ref[1:2] * g1 + cw_ref[2:3] * g
    o_ref[...] = x_ref[...] + _dot(_ffn_gate(c, val_ref[...]), wo_ref[...])


def _ffn_prompt(gv2, conv_w, conv_b, w_out, x, seq_len, tm=256):
    m = x.shape[0]
    sub = 8
    return pl.pallas_call(
        functools.partial(_ffn_prompt_kernel, tiles_per_seq=seq_len // tm),
        out_shape=jax.ShapeDtypeStruct((m, D_MODEL), F32),
        grid=(m // tm,),
        in_specs=[
            pl.BlockSpec((None, tm, D_FF), lambda i: (0, i, 0)),
            pl.BlockSpec((None, tm, D_FF), lambda i: (1, i, 0)),
            pl.BlockSpec((None, sub, D_FF), lambda i: (0, jnp.maximum(i * (tm // sub) - 1, 0), 0)),
            pl.BlockSpec((CONV_W, D_FF), lambda i: (0, 0)),
            pl.BlockSpec((1, D_FF), lambda i: (0, 0)),
            pl.BlockSpec((D_FF, D_MODEL), lambda i: (0, 0)),
            pl.BlockSpec((tm, D_MODEL), lambda i: (i, 0)),
        ],
        out_specs=pl.BlockSpec((tm, D_MODEL), lambda i: (i, 0)),
        compiler_params=_cparams("parallel"),
        name="ffn_prompt",
    )(gv2, gv2, gv2, conv_w, conv_b.reshape(1, D_FF), w_out, x)


def _ffn_sample_kernel(g_ref, val_ref, s0_ref, s1_ref, cw_ref, cb_ref, wo_ref, x_ref, o_ref):
    c = cb_ref[...] + cw_ref[0:1] * s0_ref[...] + cw_ref[1:2] * s1_ref[...] + cw_ref[2:3] * g_ref[...]
    o_ref[...] = x_ref[...] + _dot(_ffn_gate(c, val_ref[...]), wo_ref[...])


def _ffn_sample(gv2, state, conv_w, conv_b, w_out, x):
    m = x.shape[0]
    full = lambda i: (0, 0)
    return pl.pallas_call(
        _ffn_sample_kernel,
        out_shape=jax.ShapeDtypeStruct((m, D_MODEL), F32),
        grid=(1,),
        in_specs=[
            pl.BlockSpec((None, m, D_FF), lambda i: (0, 0, 0)),
            pl.BlockSpec((None, m, D_FF), lambda i: (1, 0, 0)),
            pl.BlockSpec((m, D_FF), full),
            pl.BlockSpec((m, D_FF), full),
            pl.BlockSpec((CONV_W, D_FF), full),
            pl.BlockSpec((1, D_FF), full),
            pl.BlockSpec((D_FF, D_MODEL), full),
            pl.BlockSpec((m, D_MODEL), full),
        ],
        out_specs=pl.BlockSpec((m, D_MODEL), full),
        compiler_params=_cparams("arbitrary"),
        name="ffn_sample",
    )(gv2, gv2, state[:, 0], state[:, 1], conv_w, conv_b.reshape(1, D_FF), w_out, x)


def _qknorm_rope_kernel(x_ref, gain_ref, c_ref, s1_ref, s2_ref, bd_ref, o_ref):
    x = x_ref[...]
    head_sumsq = _dot(_split_bf16(x * x), bd_ref[...])
    y = x * lax.rsqrt(head_sumsq * (1.0 / HEAD_DIM) + EPS) * gain_ref[...]
    reps = D_MODEL // LANES
    tile = lambda t: jnp.concatenate([t] * reps, axis=1)
    half = ROT_DIM // 2
    o_ref[...] = (y * tile(c_ref[...])
                  + pltpu.roll(y, D_MODEL - half, axis=1) * tile(s1_ref[...])
                  + pltpu.roll(y, half, axis=1) * tile(s2_ref[...]))


def _rope_tables(pos):
    half = ROT_DIM // 2
    inv_freq = ROPE_THETA ** (-jnp.arange(0, ROT_DIM, 2, dtype=F32) / ROT_DIM)
    ang = pos.astype(F32)[:, None] * inv_freq[None, :]
    cos, sin = jnp.cos(ang), jnp.sin(ang)
    n = pos.shape[0]
    pad = jnp.zeros((n, HEAD_DIM - ROT_DIM), F32)
    zeros = jnp.zeros((n, half), F32)
    c = jnp.concatenate([cos, cos, pad + 1.0], axis=1)
    s1 = jnp.concatenate([-sin, zeros, pad], axis=1)
    s2 = jnp.concatenate([zeros, sin, pad], axis=1)
    rep = lambda t: jnp.concatenate([t] * HEADS_PER_LANE_TILE, axis=1)
    return rep(c), rep(s1), rep(s2)


def _qknorm_rope(qkv3, q_gain, k_gain, pos, tm=512):
    m = qkv3.shape[1]
    t = pos.shape[0]
    tm = min(tm, m, t)
    tiles_per_seq = t // tm
    gains = jnp.stack([jnp.tile(q_gain, N_HEADS), jnp.tile(k_gain, N_HEADS)]).reshape(2, 1, D_MODEL)
    c, s1, s2 = _rope_tables(pos)
    head = jnp.arange(D_MODEL) // HEAD_DIM
    bd = (head[:, None] == head[None, :]).astype(BF16)
    bd2 = jnp.concatenate([bd, bd], axis=0)
    tab = pl.BlockSpec((tm, LANES), lambda s, i: (i % tiles_per_seq, 0))
    return pl.pallas_call(
        _qknorm_rope_kernel,
        out_shape=jax.ShapeDtypeStruct((2, m, D_MODEL), F32),
        grid=(2, m // tm),
        in_specs=[
            pl.BlockSpec((None, tm, D_MODEL), lambda s, i: (s, i, 0)),
            pl.BlockSpec((None, 1, D_MODEL), lambda s, i: (s, 0, 0)),
            tab, tab, tab,
            pl.BlockSpec((2 * D_MODEL, D_MODEL), lambda s, i: (0, 0)),
        ],
        out_specs=pl.BlockSpec((None, tm, D_MODEL), lambda s, i: (s, i, 0)),
        compiler_params=_cparams("parallel", "parallel"),
        name="qknorm_rope",
    )(qkv3, gains, c, s1, s2, bd2)


def _head_lane_masks():
    lane = lax.broadcasted_iota(jnp.int32, (1, LANES), 1)
    return [(lane // HEAD_DIM) == h for h in range(HEADS_PER_LANE_TILE)]


def _stack_heads(q):
    return jnp.concatenate([jnp.where(hm, q, 0.0) for hm in _head_lane_masks()], axis=0).astype(BF16)


def _unstack_heads(acc, tq):
    return sum(jnp.where(hm, acc[h * tq:(h + 1) * tq], 0.0) for h, hm in enumerate(_head_lane_masks()))


def _sb_prompt_kernel(q_ref, k_ref, v_ref, o_ref, r_ref, acc_ref, *, tq, tk):
    qi = pl.program_id(2)
    qs = _stack_heads(q_ref[...] * QK_SCALE)
    rows = qs.shape[0]
    u = _strict_upper_sum_matrix(tk)
    q_pos = lax.broadcasted_iota(jnp.int32, (rows, tq), 0) & (tq - 1)
    k_pos = lax.broadcasted_iota(jnp.int32, (rows, tq), 1)
    r_ref[...] = jnp.zeros_like(r_ref)
    acc_ref[...] = jnp.zeros_like(acc_ref)

    def chunk(c, diagonal):
        start = pl.multiple_of(c * tq, tq)
        lsp, lk = _log_sigmoid_pair(_dot_nt(qs, k_ref[pl.ds(start, tq), :].astype(BF16)))
        if diagonal:
            causal = k_pos < q_pos
            lsp, lk = jnp.where(causal, lsp, MASK_BIAS), jnp.where(causal, lk, 0.0)
        r = r_ref[...]
        weights = []
        for j in reversed(range(tq // tk)):
            cols = slice(j * tk, (j + 1) * tk)
            sums = _dot(lk[:, cols].astype(BF16), u)
            later = sums[:, :tk] + jnp.concatenate([r] * (tk // LANES), axis=1)
            weights.append(jnp.exp(lsp[:, cols] + later).astype(BF16))
            r = r + sums[:, tk:]
        r_ref[...] = r
        a = jnp.concatenate(weights[::-1], axis=1)
        acc_ref[...] += _dot(a, v_ref[pl.ds(start, tq), :].astype(BF16))

    chunk(qi, True)

    def past(it, carry):
        chunk(qi - 1 - it, False)
        return carry

    lax.fori_loop(0, qi, past, 0)
    o_ref[...] = _unstack_heads(acc_ref[...], tq)


def _attention_specs(batch, seq_len, tq, q_sel, k_sel, v_sel):
    nq = seq_len // tq
    return dict(
        grid=(batch, N_HEAD_PAIRS, nq),
        in_specs=[
            pl.BlockSpec((None, tq, LANES), lambda b, hp, qi: (q_sel, b * nq + qi, hp)),
            pl.BlockSpec((None, seq_len, LANES), lambda b, hp, qi: (k_sel, b, hp)),
            pl.BlockSpec((None, seq_len, LANES), lambda b, hp, qi: (v_sel, b, hp)),
        ],
        out_specs=pl.BlockSpec((tq, LANES), lambda b, hp, qi: (b * nq + qi, hp)),
        out_shape=jax.ShapeDtypeStruct((batch * seq_len, D_MODEL), F32),
    )


def _sb_prompt(qkv3, batch, seq_len, tq=512, tk=256):
    tq = min(tq, seq_len)
    assert tq % tk == 0 and tq & (tq - 1) == 0
    return pl.pallas_call(
        functools.partial(_sb_prompt_kernel, tq=tq, tk=tk),
        scratch_shapes=[pltpu.VMEM((HEADS_PER_LANE_TILE * tq, LANES), F32)] * 2,
        compiler_params=_cparams("parallel", "parallel", "arbitrary"),
        name="sb_prompt",
        **_attention_specs(batch, seq_len, tq, 0, 1, 2),
    )(qkv3, qkv3, qkv3)


def _moba_prompt_kernel(q_ref, k_ref, v_ref, o_ref, kmean_ref, m_ref, l_ref, acc_ref, s_ref, *, n_blocks, group):
    qi = pl.program_id(2)
    blk = MOBA_BLOCK

    @pl.when(qi == 0)
    def _():
        kmean_ref[...] = jnp.zeros_like(kmean_ref)
        for n in range(n_blocks):
            kmean_ref[n:n + 1, :] = jnp.mean(k_ref[n * blk:(n + 1) * blk, :], axis=0, keepdims=True)

    qs = _stack_heads(q_ref[...] * QK_SCALE)
    rows = qs.shape[0]
    neg_inf = -jnp.inf
    lane = lax.broadcasted_iota(jnp.int32, (rows, LANES), 1)
    gate = jnp.where(lane < qi, _dot_nt(qs, kmean_ref[...].astype(BF16)), neg_inf)
    sel = jnp.zeros((rows, LANES), jnp.bool_)
    for r in range(MOBA_TOPK):
        best = jnp.max(gate, axis=-1, keepdims=True)
        idx = jnp.min(jnp.where(gate == best, lane, LANES), axis=-1, keepdims=True)
        pick = lane == idx
        sel = sel | (pick & (r < qi))
        gate = jnp.where(pick, neg_inf, gate)
    q_aug = jnp.concatenate([qs, jnp.where(sel, 0.0, MASK_BIAS).astype(BF16)], axis=1)

    q_pos = lax.broadcasted_iota(jnp.int32, (rows, blk), 0) & (blk - 1)
    k_pos = lax.broadcasted_iota(jnp.int32, (rows, blk), 1)
    start = pl.multiple_of(qi * blk, blk)
    def fold(scores, m, first_key):
        m_new = m
        for s in scores:
            m_new = jnp.maximum(m_new, jnp.max(s, axis=-1, keepdims=True))
        shift = jnp.concatenate([m_new] * (blk // LANES), axis=1)
        pv = 0.0
        for j, s in enumerate(scores):
            v = v_ref[pl.ds(pl.multiple_of(first_key + j * blk, blk), blk), :].astype(BF16)
            pv = pv + _dot(jnp.exp(s - shift).astype(BF16), jnp.concatenate([v, jnp.ones_like(v)], axis=1))
        return m_new, pv[:, :LANES], pv[:, LANES:]

    s = jnp.where(k_pos <= q_pos, _dot_nt(qs, k_ref[pl.ds(start, blk), :].astype(BF16)), neg_inf)
    m_ref[...], acc_ref[...], l_ref[...] = fold([s], jnp.full((rows, LANES), neg_inf, F32), start)

    k_lane = lax.broadcasted_iota(jnp.int32, (blk, LANES), 1)

    def score_group(i, slot):
        for j in range(group):
            keys = k_ref[pl.ds(pl.multiple_of((group * i + j) * blk, blk), blk), :].astype(BF16)
            one_hot = jnp.where(k_lane == group * i + j, 1.0, 0.0).astype(BF16)
            s_ref[slot, j] = _dot_nt(q_aug, jnp.concatenate([keys, one_hot], axis=1))

    n_groups = (qi + group - 1) // group

    def past_group(i, carry):
        slot = i % 2
        m = m_ref[...]
        m_new, pv, p_sum = fold([s_ref[slot, j] for j in range(group)], m, i * (group * blk))
        alpha = jnp.exp(m - m_new)
        m_ref[...] = m_new
        l_ref[...] = alpha * l_ref[...] + p_sum
        acc_ref[...] = alpha * acc_ref[...] + pv
        score_group(jnp.minimum(i + 1, n_groups - 1), 1 - slot)
        return carry

    score_group(0, 0)
    lax.fori_loop(0, n_groups, past_group, 0)
    o_ref[...] = _unstack_heads(acc_ref[...] / l_ref[...], blk)


def _moba_prompt(qk2, qkv3, batch, seq_len, group=4):
    n_blocks = seq_len // MOBA_BLOCK
    assert n_blocks <= LANES and n_blocks % group == 0
    specs = _attention_specs(batch, seq_len, MOBA_BLOCK, 0, 1, 2)
    rows = HEADS_PER_LANE_TILE * MOBA_BLOCK
    return pl.pallas_call(
        functools.partial(_moba_prompt_kernel, n_blocks=n_blocks, group=group),
        scratch_shapes=[pltpu.VMEM((LANES, LANES), F32)] + [pltpu.VMEM((rows, LANES), F32)] * 3
                       + [pltpu.VMEM((2, group, rows, MOBA_BLOCK), F32)],
        compiler_params=_cparams("parallel", "arbitrary", "arbitrary"),
        name="moba_prompt",
        **specs,
    )(qk2, qk2, qkv3)


PAGES_PER_STEP = 2
KEYS_PER_STEP = PAGES_PER_STEP * PAGE_SIZE


PAGE_SHAPE = (N_HEADS, HEAD_DIM, PAGE_SIZE)
HEAD_ROW = (N_HEADS, PAGE_SIZE)


def _page_view(cache):
    return jnp.transpose(cache, (0, 1, 3, 4, 2))


def _head_columns(x):
    return x.reshape(x.shape[0], N_HEADS, HEAD_DIM, 1)


def _page_scores(k_page, qb):
    return jnp.sum(k_page * qb, axis=1)


def _over_dims(x):
    return x[:, None, :]


def _lane_sum(x):
    return jnp.sum(x, axis=-1, keepdims=True)


def _suffix_sum_exclusive(x):
    lane = lax.broadcasted_iota(jnp.int32, x.shape, x.ndim - 1)
    inc = x
    d = 1
    while d < LANES:
        inc = inc + jnp.where(lane + d < LANES, pltpu.roll(inc, LANES - d, axis=x.ndim - 1), 0.0)
        d *= 2
    return inc - x


def _sb_decode_kernel(pt_ref, pos_ref, q_ref, kown_ref, vown_ref, k0_ref, k1_ref, v0_ref, v1_ref,
                      o_ref, qb_ref, r_ref, acc_ref):
    n = pl.program_id(1)

    @pl.when(n == 0)
    def _():
        q = q_ref[...] * QK_SCALE
        qb_ref[...] = jnp.broadcast_to(q, PAGE_SHAPE)
        valid = jnp.where(pos_ref[1] < pos_ref[0], 1.0, 0.0)
        lsp, lk = _log_sigmoid_pair(jnp.sum(q * kown_ref[...], axis=1))
        r_ref[...] = jnp.broadcast_to(lk * valid, HEAD_ROW)
        lane = lax.broadcasted_iota(jnp.int32, PAGE_SHAPE, 2)
        acc_ref[...] = jnp.where(lane == 0, _over_dims(jnp.exp(lsp) * valid) * vown_ref[...], 0.0)

    qb = qb_ref[...]
    r = r_ref[...]
    acc = acc_ref[...]
    for k_ref, v_ref in ((k1_ref, v1_ref), (k0_ref, v0_ref)):
        lsp, lk = _log_sigmoid_pair(_page_scores(k_ref[...], qb))
        a = jnp.exp(lsp + _suffix_sum_exclusive(lk) + r)
        acc = acc + _over_dims(a) * v_ref[...]
        r = r + _lane_sum(lk)
    r_ref[...] = r
    acc_ref[...] = acc

    @pl.when(n == pl.num_programs(1) - 1)
    def _():
        o_ref[...] = _lane_sum(acc)


def _decode_specs(n_steps, layer, newest_first):
    def page(j):
        def index_map(b, n, pt, *_):
            step = (n_steps - 1 - n) if newest_first else n
            return (pt[b, PAGES_PER_STEP * step + j], layer, 0, 0, 0)
        return pl.BlockSpec((None, None) + PAGE_SHAPE, index_map)

    col = pl.BlockSpec((None, N_HEADS, HEAD_DIM, 1), lambda b, n, *_: (b, 0, 0, 0))
    return col, page(0), page(1)


def _sb_decode(q, k_own, v_own, cache_k, cache_v, page_table, layer, q_pos, k_own_pos):
    batch = q.shape[0]
    n_steps = page_table.shape[1] // PAGES_PER_STEP
    col, p0, p1 = _decode_specs(n_steps, layer, True)
    pos = jnp.array([q_pos, k_own_pos], jnp.int32)
    ck, cv = _page_view(cache_k), _page_view(cache_v)
    out = pl.pallas_call(
        _sb_decode_kernel,
        out_shape=jax.ShapeDtypeStruct((batch, N_HEADS, HEAD_DIM, 1), F32),
        grid_spec=pltpu.PrefetchScalarGridSpec(
            num_scalar_prefetch=2,
            grid=(batch, n_steps),
            in_specs=[col, col, col, p0, p1, p0, p1],
            out_specs=col,
            scratch_shapes=[pltpu.VMEM(PAGE_SHAPE, F32), pltpu.VMEM(HEAD_ROW, F32), pltpu.VMEM(PAGE_SHAPE, F32)],
        ),
        compiler_params=_cparams("parallel", "arbitrary"),
        name="sb_decode",
    )(page_table, pos, _head_columns(q), _head_columns(k_own), _head_columns(v_own), ck, ck, cv, cv)
    return out.reshape(batch, D_MODEL)


def _moba_decode_kernel(pt_ref, q_ref, kown_ref, vown_ref, k0_ref, k1_ref, v0_ref, v1_ref,
                        o_ref, qb_ref, gate_ref, m_ref, l_ref, pv_ref):
    n = pl.program_id(1)
    n_blocks = pl.num_programs(1)

    @pl.when(n == 0)
    def _():
        qb_ref[...] = jnp.broadcast_to(q_ref[...] * QK_SCALE, PAGE_SHAPE)

    qb = qb_ref[...]
    s0 = _page_scores(k0_ref[...], qb)
    s1 = _page_scores(k1_ref[...], qb)
    m = jnp.maximum(jnp.max(s0, axis=-1, keepdims=True), jnp.max(s1, axis=-1, keepdims=True))
    p0 = jnp.exp(s0 - m)
    p1 = jnp.exp(s1 - m)
    gate_ref[n] = jnp.broadcast_to((_lane_sum(s0) + _lane_sum(s1)) * (1.0 / MOBA_BLOCK), HEAD_ROW)
    m_ref[n] = jnp.broadcast_to(m, HEAD_ROW)
    l_ref[n] = jnp.broadcast_to(_lane_sum(p0) + _lane_sum(p1), HEAD_ROW)
    pv_ref[n] = _over_dims(p0) * v0_ref[...] + _over_dims(p1) * v1_ref[...]

    @pl.when(n == n_blocks - 1)
    def _():
        shape = gate_ref.shape
        blk = lax.broadcasted_iota(jnp.int32, shape, 0)
        gate = gate_ref[...]
        sel = jnp.zeros(shape, jnp.bool_)
        for r in range(MOBA_TOPK):
            best = jnp.max(gate, axis=0, keepdims=True)
            idx = jnp.min(jnp.where(gate == best, blk, shape[0]), axis=0, keepdims=True)
            pick = blk == idx
            sel = sel | (pick & (r < n_blocks))
            gate = jnp.where(pick, -jnp.inf, gate)
        q = q_ref[...] * QK_SCALE
        s_own = jnp.broadcast_to(jnp.sum(q * kown_ref[...], axis=1), HEAD_ROW)
        m_all = m_ref[...]
        m_fin = jnp.maximum(jnp.max(jnp.where(sel, m_all, -jnp.inf), axis=0), s_own)
        w = jnp.where(sel, jnp.exp(m_all - m_fin[None]), 0.0)
        w_own = jnp.exp(s_own - m_fin)
        denom = jnp.sum(w * l_ref[...], axis=0) + w_own
        gate_ref[...] = w
        numer = lax.fori_loop(0, n_blocks, lambda i, acc: acc + _over_dims(gate_ref[i]) * pv_ref[i],
                              jnp.zeros(PAGE_SHAPE, F32))
        first_lane = lambda t: _over_dims(t)[:, :, 0:1]
        o_ref[...] = (_lane_sum(numer) + first_lane(w_own) * vown_ref[...]) / first_lane(denom)


def _moba_decode(q, k_own, v_own, cache_k, cache_v, page_table, layer):
    batch = q.shape[0]
    assert MOBA_BLOCK == KEYS_PER_STEP
    n_blocks = page_table.shape[1] // PAGES_PER_STEP
    col, p0, p1 = _decode_specs(n_blocks, layer, False)
    ck, cv = _page_view(cache_k), _page_view(cache_v)
    stat = pltpu.VMEM((n_blocks,) + HEAD_ROW, F32)
    out = pl.pallas_call(
        _moba_decode_kernel,
        out_shape=jax.ShapeDtypeStruct((batch, N_HEADS, HEAD_DIM, 1), F32),
        grid_spec=pltpu.PrefetchScalarGridSpec(
            num_scalar_prefetch=1,
            grid=(batch, n_blocks),
            in_specs=[col, col, col, p0, p1, p0, p1],
            out_specs=col,
            scratch_shapes=[pltpu.VMEM(PAGE_SHAPE, F32), stat, stat, stat,
                            pltpu.VMEM((n_blocks,) + PAGE_SHAPE, F32)],
        ),
        compiler_params=_cparams("parallel", "arbitrary"),
        name="moba_decode",
    )(page_table, _head_columns(q), _head_columns(k_own), _head_columns(v_own), ck, ck, cv, cv)
    return out.reshape(batch, D_MODEL)


def kernel(x_prompt, x_sample, cache_sb_k, cache_sb_v, cache_moba_k, cache_moba_v, state_ffn_conv, page_table, p_prompt, p_sample, mix_norm, ffn_norm, ple_norm, sgu_w_in, sgu_v_norm, sgu_w_s, sgu_b_s, sgu_w_out, sb_w_qkv, sb_w_out, moba_w_qkv, moba_q_norm, moba_k_norm, moba_w_out, ffn_w_in, ffn_conv_w, ffn_conv_b, ffn_w_out, ple_w_in, ple_w_gate):
    bp, tp, _ = x_prompt.shape
    bs, ts, _ = x_sample.shape
    assert ts == 1 and tp % (2 * SGU_CHUNK) == 0 and tp % MOBA_BLOCK == 0
    depth = mix_norm.shape[0]
    past_len = page_table.shape[1] * PAGE_SIZE
    assert past_len % MOBA_BLOCK == 0 and past_len % SGU_CHUNK == 0
    xp = x_prompt.reshape(bp * tp, D_MODEL)
    xs = x_sample.reshape(bs, D_MODEL)
    bf = lambda w: w.astype(BF16)
    heads_p = lambda t: t.reshape(bp, tp, N_HEADS, HEAD_DIM)
    heads_s = lambda t: t.reshape(bs, ts, N_HEADS, HEAD_DIM)

    sb_kp, sb_vp, sb_ks, sb_vs = [], [], [], []
    mb_kp, mb_vp, mb_ks, mb_vs = [], [], [], []
    sgu_vs, conv_p, conv_s = [], [], []

    for i in range(depth):
        kind, j = i % N_MIXERS, i // N_MIXERS
        if kind == 0:
            w_in, w_out = bf(sgu_w_in[j]), bf(sgu_w_out[j])
            zp = _norm_matmul(xp, mix_norm[i], w_in, 2, act="gelu")
            zs = _norm_matmul(xs, mix_norm[i], w_in, 2, act="gelu")
            xp = _sgu_prompt(zp, sgu_v_norm[j], sgu_w_s[j], sgu_b_s[j], w_out, xp)
            xs, v_s = _sgu_sample(zs, sgu_v_norm[j], sgu_w_s[j], sgu_b_s[j], w_out, xs)
            sgu_vs.append(v_s.reshape(bs, ts, SGU_WIDTH))
        elif kind == 1:
            w_qkv, w_out = bf(sb_w_qkv[j]), bf(sb_w_out[j])
            qkv_p = _norm_matmul(xp, mix_norm[i], w_qkv, 3)
            qkv_s = _norm_matmul(xs, mix_norm[i], w_qkv, 3)
            xp = _proj_residual(_sb_prompt(qkv_p, bp, tp), w_out, xp)
            att = _sb_decode(qkv_s[0], qkv_s[1], qkv_s[2], cache_sb_k, cache_sb_v,
                             page_table, j, past_len, past_len)
            xs = _proj_residual(att, w_out, xs)
            sb_kp.append(heads_p(qkv_p[1]))
            sb_vp.append(heads_p(qkv_p[2]))
            sb_ks.append(heads_s(qkv_s[1]))
            sb_vs.append(heads_s(qkv_s[2]))
        else:
            w_qkv, w_out = bf(moba_w_qkv[j]), bf(moba_w_out[j])
            qkv_p = _norm_matmul(xp, mix_norm[i], w_qkv, 3)
            qkv_s = _norm_matmul(xs, mix_norm[i], w_qkv, 3)
            qk_p = _qknorm_rope(qkv_p, moba_q_norm[j], moba_k_norm[j], jnp.arange(tp, dtype=jnp.int32))
            qk_s = _qknorm_rope(qkv_s, moba_q_norm[j], moba_k_norm[j], jnp.full((bs,), past_len, jnp.int32))
            xp = _proj_residual(_moba_prompt(qk_p, qkv_p, bp, tp), w_out, xp)
            att = _moba_decode(qk_s[0], qk_s[1], qkv_s[2], cache_moba_k, cache_moba_v, page_table, j)
            xs = _proj_residual(att, w_out, xs)
            mb_kp.append(heads_p(qk_p[1]))
            mb_vp.append(heads_p(qkv_p[2]))
            mb_ks.append(heads_s(qk_s[1]))
            mb_vs.append(heads_s(qkv_s[2]))

        w_in, w_out = bf(ffn_w_in[i]), bf(ffn_w_out[i])
        gv_p = _norm_matmul(xp, ffn_norm[i], w_in, 2)
        gv_s = _norm_matmul(xs, ffn_norm[i], w_in, 2)
        xp = _ffn_prompt(gv_p, ffn_conv_w[i], ffn_conv_b[i], w_out, xp, tp)
        xs = _ffn_sample(gv_s, state_ffn_conv[i], ffn_conv_w[i], ffn_conv_b[i], w_out, xs)
        conv_p.append(gv_p[0].reshape(bp, tp, D_FF)[:, tp - (CONV_W - 1):])
        conv_s.append(jnp.concatenate([state_ffn_conv[i][:, 1:], gv_s[0][:, None]], axis=1))

        w_gate, w_pin = bf(ple_w_gate[i]), bf(ple_w_in[i])
        xp = _ple(xp, p_prompt[i].reshape(bp * tp, -1), ple_norm[i], w_gate, w_pin)
        xs = _ple(xs, p_sample[i].reshape(bs, -1), ple_norm[i], w_gate, w_pin)

    return (xp.reshape(bp, tp, D_MODEL), xs.reshape(bs, ts, D_MODEL),
            jnp.stack(sb_kp, axis=1), jnp.stack(sb_vp, axis=1),
            jnp.stack(sb_ks, axis=1), jnp.stack(sb_vs, axis=1),
            jnp.stack(mb_kp, axis=1), jnp.stack(mb_vp, axis=1),
            jnp.stack(mb_ks, axis=1), jnp.stack(mb_vs, axis=1),
            jnp.stack(sgu_vs, axis=0), jnp.stack(conv_p, axis=0), jnp.stack(conv_s, axis=0))
```

```python
import functools

import jax
import jax.numpy as jnp
from jax import lax
from jax.experimental import pallas as pl
from jax.experimental.pallas import tpu as pltpu

F32 = jnp.float32
BF16 = jnp.bfloat16

D_MODEL = 1024
N_HEADS = 16
HEAD_DIM = 64
PAGE_SIZE = 128
SGU_CHUNK = 128
SGU_GROUPS = 8
SGU_WIDTH = 2 * D_MODEL
SGU_GROUP_DIM = SGU_WIDTH // SGU_GROUPS
MOBA_BLOCK = 256
MOBA_TOPK = 3
ROT_DIM = HEAD_DIM // 4
ROPE_THETA = 500000.0
D_FF = 2816
CONV_W = 3
EPS = 1e-6
N_MIXERS = 3

LANES = 128
HEADS_PER_LANE_TILE = LANES // HEAD_DIM
N_HEAD_PAIRS = D_MODEL // LANES
QK_SCALE = HEAD_DIM ** -0.5
MASK_BIAS = -1e30
VMEM_LIMIT = 48 * 1024 * 1024


def _cparams(*sem):
    return pltpu.CompilerParams(dimension_semantics=sem, vmem_limit_bytes=VMEM_LIMIT)


def _rms(x, g):
    return x * lax.rsqrt(jnp.mean(x * x, axis=-1, keepdims=True) + EPS) * g


def _dot(a, b):
    return jnp.dot(a, b, preferred_element_type=F32)


def _dot_nt(a, b):
    return lax.dot_general(a, b, (((1,), (1,)), ((), ())), preferred_element_type=F32)


def _split_bf16(x):
    hi = x.astype(BF16)
    lo = (x - hi.astype(F32)).astype(BF16)
    return jnp.concatenate([hi, lo], axis=1)


def _strict_upper_sum_matrix(n):
    j = lax.broadcasted_iota(jnp.int32, (n, n + LANES), 0)
    s = lax.broadcasted_iota(jnp.int32, (n, n + LANES), 1)
    return jnp.where((j > s) | (s >= n), 1.0, 0.0).astype(BF16)


def _log_sigmoid_pair(z):
    lsp = jnp.minimum(z, 0.0) - jnp.log(1.0 + jnp.exp(-jnp.abs(z)))
    return lsp, lsp - z


def _norm_matmul_kernel(x_ref, g_ref, w_ref, o_ref, xn_ref, *, act):
    @pl.when(pl.program_id(1) == 0)
    def _():
        xn_ref[...] = _rms(x_ref[...], g_ref[...]).astype(BF16)

    y = _dot(xn_ref[...], w_ref[...])
    if act == "gelu":
        y = jax.nn.gelu(y)
    o_ref[...] = y


def _lane_tile(n, cap):
    return max(t for t in range(LANES, cap + 1, LANES) if n % t == 0)


def _norm_matmul(x, gain, w, n_split, act=None, tm=512, tn_cap=1536):
    m, k = x.shape
    n_out = w.shape[1]
    per = n_out // n_split
    tm = min(tm, m)
    tn = _lane_tile(per, tn_cap)
    nj = per // tn
    assert m % tm == 0 and per % tn == 0
    return pl.pallas_call(
        functools.partial(_norm_matmul_kernel, act=act),
        out_shape=jax.ShapeDtypeStruct((n_split, m, per), F32),
        grid=(m // tm, n_out // tn),
        in_specs=[
            pl.BlockSpec((tm, k), lambda i, j: (i, 0)),
            pl.BlockSpec((1, k), lambda i, j: (0, 0)),
            pl.BlockSpec((k, tn), lambda i, j: (0, j)),
        ],
        out_specs=pl.BlockSpec((None, tm, tn), lambda i, j: (j // nj, i, j % nj)),
        scratch_shapes=[pltpu.VMEM((tm, k), BF16)],
        compiler_params=_cparams("parallel", "arbitrary"),
        name="norm_matmul",
    )(x, gain.reshape(1, k), w)


def _proj_residual_kernel(h_ref, w_ref, x_ref, o_ref):
    o_ref[...] = x_ref[...] + _dot(h_ref[...].astype(BF16), w_ref[...])


def _proj_residual(h, w, x, tm=1024):
    m, k = h.shape
    n = w.shape[1]
    tm = min(tm, m)
    return pl.pallas_call(
        _proj_residual_kernel,
        out_shape=jax.ShapeDtypeStruct((m, n), F32),
        grid=(m // tm,),
        in_specs=[
            pl.BlockSpec((tm, k), lambda i: (i, 0)),
            pl.BlockSpec((k, n), lambda i: (0, 0)),
            pl.BlockSpec((tm, n), lambda i: (i, 0)),
        ],
        out_specs=pl.BlockSpec((tm, n), lambda i: (i, 0)),
        compiler_params=_cparams("parallel"),
        name="proj_residual",
    )(h, w, x)


def _ple_kernel(x_ref, p_ref, g_ref, wg_ref, wp_ref, o_ref):
    x = x_ref[...]
    gate = jax.nn.sigmoid(_dot(_rms(x, g_ref[...]).astype(BF16), wg_ref[...]))
    o_ref[...] = x + gate * _dot(p_ref[...].astype(BF16), wp_ref[...])


def _ple(x, p_layers, layer, gain, w_gate, w_in, tm=512):
    m, d = x.shape
    pd = p_layers.shape[2]
    tm = min(tm, m)
    return pl.pallas_call(
        _ple_kernel,
        out_shape=jax.ShapeDtypeStruct((m, d), F32),
        grid=(m // tm,),
        in_specs=[
            pl.BlockSpec((tm, d), lambda i: (i, 0)),
            pl.BlockSpec((None, tm, pd), lambda i: (layer, i, 0)),
            pl.BlockSpec((1, d), lambda i: (0, 0)),
            pl.BlockSpec((d, d), lambda i: (0, 0)),
            pl.BlockSpec((pd, d), lambda i: (0, 0)),
        ],
        out_specs=pl.BlockSpec((tm, d), lambda i: (i, 0)),
        compiler_params=_cparams("parallel"),
        name="ple",
    )(x, p_layers, gain.reshape(1, d), w_gate, w_in)


def _sgu_prompt_kernel(x_ref, gain_ref, win_ref, vg_ref, ws_ref, bst_ref, wo_ref, o_ref, *, chunks):
    x = x_ref[...]
    xn = _rms(x, gain_ref[...]).astype(BF16)
    v = _rms(jax.nn.gelu(_dot(xn, win_ref[:, SGU_WIDTH:])), vg_ref[...]).astype(BF16)
    i = lax.broadcasted_iota(jnp.int32, (SGU_CHUNK, SGU_CHUNK), 0)
    j = lax.broadcasted_iota(jnp.int32, (SGU_CHUNK, SGU_CHUNK), 1)
    acc = x
    for g in range(SGU_GROUPS):
        cols = slice(g * SGU_GROUP_DIM, (g + 1) * SGU_GROUP_DIM)
        w = jnp.where(i >= j, ws_ref[g], 0.0).astype(BF16)
        bias = bst_ref[:, g:g + 1]
        mixed = jnp.concatenate(
            [_dot(w, v[c * SGU_CHUNK:(c + 1) * SGU_CHUNK, cols]) + bias for c in range(chunks)], axis=0)
        u = jax.nn.gelu(_dot(xn, win_ref[:, cols]))
        acc = acc + _dot((u * mixed).astype(BF16), wo_ref[cols, :])
    o_ref[...] = acc


def _sgu_prompt(x, gain, w_in, v_gain, w_s, b_s, w_out, chunks=4):
    m = x.shape[0]
    tm = chunks * SGU_CHUNK
    return pl.pallas_call(
        functools.partial(_sgu_prompt_kernel, chunks=chunks),
        out_shape=jax.ShapeDtypeStruct((m, D_MODEL), F32),
        grid=(m // tm,),
        in_specs=[
            pl.BlockSpec((tm, D_MODEL), lambda i: (i, 0)),
            _resident((1, D_MODEL)),
            _resident((D_MODEL, 2 * SGU_WIDTH)),
            _resident((1, SGU_WIDTH)),
            _resident((SGU_GROUPS, SGU_CHUNK, SGU_CHUNK)),
            _resident((SGU_CHUNK, SGU_GROUPS)),
            _resident((SGU_WIDTH, D_MODEL)),
        ],
        out_specs=pl.BlockSpec((tm, D_MODEL), lambda i: (i, 0)),
        compiler_params=_cparams("parallel"),
        name="sgu_prompt",
    )(x, gain.reshape(1, D_MODEL), w_in, v_gain.reshape(1, SGU_WIDTH), w_s, b_s.T, w_out)


def _sgu_sample_kernel(u_ref, v_ref, vg_ref, w0_ref, b0_ref, wo_ref, x_ref, o_ref, vn_ref):
    v = _rms(v_ref[...], vg_ref[...])
    vn_ref[...] = v
    gated = u_ref[...] * (w0_ref[...] * v.astype(BF16).astype(F32) + b0_ref[...])
    o_ref[...] = x_ref[...] + _dot(gated.astype(BF16), wo_ref[...])


def _sgu_sample(z2, v_gain, w_s, b_s, w_out, x):
    m = x.shape[0]
    w0 = jnp.repeat(w_s[:, 0, 0].astype(BF16).astype(F32), SGU_GROUP_DIM).reshape(1, SGU_WIDTH)
    b0 = jnp.repeat(b_s[:, 0], SGU_GROUP_DIM).reshape(1, SGU_WIDTH)
    row = lambda i: (0, 0)
    return pl.pallas_call(
        _sgu_sample_kernel,
        out_shape=(jax.ShapeDtypeStruct((m, D_MODEL), F32), jax.ShapeDtypeStruct((m, SGU_WIDTH), F32)),
        grid=(1,),
        in_specs=[
            pl.BlockSpec((None, m, SGU_WIDTH), lambda i: (0, 0, 0)),
            pl.BlockSpec((None, m, SGU_WIDTH), lambda i: (1, 0, 0)),
            pl.BlockSpec((1, SGU_WIDTH), row),
            pl.BlockSpec((1, SGU_WIDTH), row),
            pl.BlockSpec((1, SGU_WIDTH), row),
            pl.BlockSpec((SGU_WIDTH, D_MODEL), row),
            pl.BlockSpec((m, D_MODEL), row),
        ],
        out_specs=(pl.BlockSpec((m, D_MODEL), row), pl.BlockSpec((m, SGU_WIDTH), row)),
        compiler_params=_cparams("arbitrary"),
        name="sgu_sample",
    )(z2, z2, v_gain.reshape(1, SGU_WIDTH), w0, b0, w_out, x)


def _ffn_gate(c, val):
    return (jax.nn.silu(c) * val).astype(BF16)


SUBLANES = 8
FFN_CHUNK = 256


def _resident(shape):
    return pl.BlockSpec(shape, lambda i: (0,) * len(shape), pipeline_mode=pl.Buffered(1))


def _ffn_prompt_kernel(x_ref, gain_ref, win_ref, cw_ref, cb_ref, wo_ref, o_ref, tail_ref, prev_ref, *,
                       tiles_per_seq):
    x = x_ref[...]
    tm = x.shape[0]
    xn = _rms(x, gain_ref[...]).astype(BF16)
    @pl.when(pl.program_id(0) % tiles_per_seq == 0)
    def _():
        prev_ref[...] = jnp.zeros_like(prev_ref)

    row = lax.broadcasted_iota(jnp.int32, (tm, 1), 0)
    acc = x
    for c0 in range(0, D_FF, FFN_CHUNK):
        cols = slice(c0, c0 + FFN_CHUNK)
        g = _dot(xn, win_ref[:, cols])
        val = _dot(xn, win_ref[:, D_FF + c0:D_FF + c0 + FFN_CHUNK])
        prev = prev_ref[:, cols]
        g1 = jnp.where(row == 0, prev[7:8], pltpu.roll(g, 1, axis=0))
        g2 = jnp.where(row == 0, prev[6:7], jnp.where(row == 1, prev[7:8], pltpu.roll(g, 2, axis=0)))
        c = cb_ref[:, cols] + cw_ref[0:1, cols] * g2 + cw_ref[1:2, cols] * g1 + cw_ref[2:3, cols] * g
        acc = acc + _dot(_ffn_gate(c, val), wo_ref[cols, :])
        prev_ref[:, cols] = g[tm - SUBLANES:, :]
        tail_ref[:, cols] = g[tm - SUBLANES:, :]
    o_ref[...] = acc


def _ffn_prompt(x, gain, w_in, conv_w, conv_b, w_out, seq_len, tm=512):
    m = x.shape[0]
    assert seq_len % tm == 0 and D_FF % FFN_CHUNK == 0
    return pl.pallas_call(
        functools.partial(_ffn_prompt_kernel, tiles_per_seq=seq_len // tm),
        out_shape=(jax.ShapeDtypeStruct((m, D_MODEL), F32),
                   jax.ShapeDtypeStruct((m // tm, SUBLANES, D_FF), F32)),
        grid=(m // tm,),
        in_specs=[
            pl.BlockSpec((tm, D_MODEL), lambda i: (i, 0)),
            _resident((1, D_MODEL)),
            _resident((D_MODEL, 2 * D_FF)),
            _resident((CONV_W, D_FF)),
            _resident((1, D_FF)),
            _resident((D_FF, D_MODEL)),
        ],
        out_specs=(pl.BlockSpec((tm, D_MODEL), lambda i: (i, 0)),
                   pl.BlockSpec((None, SUBLANES, D_FF), lambda i: (i, 0, 0))),
        scratch_shapes=[pltpu.VMEM((SUBLANES, D_FF), F32)],
        compiler_params=_cparams("arbitrary"),
        name="ffn_prompt",
    )(x, gain.reshape(1, D_MODEL), w_in, conv_w, conv_b.reshape(1, D_FF), w_out)


def _ffn_sample_kernel(g_ref, val_ref, s0_ref, s1_ref, cw_ref, cb_ref, wo_ref, x_ref, o_ref):
    c = cb_ref[...] + cw_ref[0:1] * s0_ref[...] + cw_ref[1:2] * s1_ref[...] + cw_ref[2:3] * g_ref[...]
    o_ref[...] = x_ref[...] + _dot(_ffn_gate(c, val_ref[...]), wo_ref[...])


def _ffn_sample(gv2, state, conv_w, conv_b, w_out, x):
    m = x.shape[0]
    full = lambda i: (0, 0)
    return pl.pallas_call(
        _ffn_sample_kernel,
        out_shape=jax.ShapeDtypeStruct((m, D_MODEL), F32),
        grid=(1,),
        in_specs=[
            pl.BlockSpec((None, m, D_FF), lambda i: (0, 0, 0)),
            pl.BlockSpec((None, m, D_FF), lambda i: (1, 0, 0)),
            pl.BlockSpec((m, D_FF), full),
            pl.BlockSpec((m, D_FF), full),
            pl.BlockSpec((CONV_W, D_FF), full),
            pl.BlockSpec((1, D_FF), full),
            pl.BlockSpec((D_FF, D_MODEL), full),
            pl.BlockSpec((m, D_MODEL), full),
        ],
        out_specs=pl.BlockSpec((m, D_MODEL), full),
        compiler_params=_cparams("arbitrary"),
        name="ffn_sample",
    )(gv2, gv2, state[:, 0], state[:, 1], conv_w, conv_b.reshape(1, D_FF), w_out, x)


def _qknorm_rope_kernel(x_ref, gain_ref, c_ref, s1_ref, s2_ref, bd_ref, o_ref):
    x = x_ref[...]
    head_sumsq = _dot(_split_bf16(x * x), bd_ref[...])
    y = x * lax.rsqrt(head_sumsq * (1.0 / HEAD_DIM) + EPS) * gain_ref[...]
    reps = D_MODEL // LANES
    tile = lambda t: jnp.concatenate([t] * reps, axis=1)
    half = ROT_DIM // 2
    o_ref[...] = (y * tile(c_ref[...])
                  + pltpu.roll(y, D_MODEL - half, axis=1) * tile(s1_ref[...])
                  + pltpu.roll(y, half, axis=1) * tile(s2_ref[...]))


def _rope_tables(pos):
    half = ROT_DIM // 2
    inv_freq = ROPE_THETA ** (-jnp.arange(0, ROT_DIM, 2, dtype=F32) / ROT_DIM)
    ang = pos.astype(F32)[:, None] * inv_freq[None, :]
    cos, sin = jnp.cos(ang), jnp.sin(ang)
    n = pos.shape[0]
    pad = jnp.zeros((n, HEAD_DIM - ROT_DIM), F32)
    zeros = jnp.zeros((n, half), F32)
    c = jnp.concatenate([cos, cos, pad + 1.0], axis=1)
    s1 = jnp.concatenate([-sin, zeros, pad], axis=1)
    s2 = jnp.concatenate([zeros, sin, pad], axis=1)
    rep = lambda t: jnp.concatenate([t] * HEADS_PER_LANE_TILE, axis=1)
    return rep(c), rep(s1), rep(s2)


def _qknorm_rope(qkv3, q_gain, k_gain, pos, tm=512):
    m = qkv3.shape[1]
    t = pos.shape[0]
    tm = min(tm, m, t)
    tiles_per_seq = t // tm
    gains = jnp.stack([jnp.tile(q_gain, N_HEADS), jnp.tile(k_gain, N_HEADS)]).reshape(2, 1, D_MODEL)
    c, s1, s2 = _rope_tables(pos)
    head = jnp.arange(D_MODEL) // HEAD_DIM
    bd = (head[:, None] == head[None, :]).astype(BF16)
    bd2 = jnp.concatenate([bd, bd], axis=0)
    tab = pl.BlockSpec((tm, LANES), lambda s, i: (i % tiles_per_seq, 0))
    return pl.pallas_call(
        _qknorm_rope_kernel,
        out_shape=jax.ShapeDtypeStruct((2, m, D_MODEL), F32),
        grid=(2, m // tm),
        in_specs=[
            pl.BlockSpec((None, tm, D_MODEL), lambda s, i: (s, i, 0)),
            pl.BlockSpec((None, 1, D_MODEL), lambda s, i: (s, 0, 0)),
            tab, tab, tab,
            pl.BlockSpec((2 * D_MODEL, D_MODEL), lambda s, i: (0, 0)),
        ],
        out_specs=pl.BlockSpec((None, tm, D_MODEL), lambda s, i: (s, i, 0)),
        compiler_params=_cparams("parallel", "parallel"),
        name="qknorm_rope",
    )(qkv3, gains, c, s1, s2, bd2)


def _head_lane_masks():
    lane = lax.broadcasted_iota(jnp.int32, (1, LANES), 1)
    return [(lane // HEAD_DIM) == h for h in range(HEADS_PER_LANE_TILE)]


def _stack_heads(q):
    return jnp.concatenate([jnp.where(hm, q, 0.0) for hm in _head_lane_masks()], axis=0).astype(BF16)


def _unstack_heads(acc, tq):
    return sum(jnp.where(hm, acc[h * tq:(h + 1) * tq], 0.0) for h, hm in enumerate(_head_lane_masks()))


def _sb_prompt_kernel(q_ref, k_ref, v_ref, o_ref, r_ref, acc_ref, *, tq, tk):
    qi = pl.program_id(2)
    qs = _stack_heads(q_ref[...] * QK_SCALE)
    rows = qs.shape[0]
    u = _strict_upper_sum_matrix(tk)
    q_pos = lax.broadcasted_iota(jnp.int32, (rows, tq), 0) & (tq - 1)
    k_pos = lax.broadcasted_iota(jnp.int32, (rows, tq), 1)
    r_ref[...] = jnp.zeros_like(r_ref)
    acc_ref[...] = jnp.zeros_like(acc_ref)

    def chunk(c, diagonal):
        start = pl.multiple_of(c * tq, tq)
        lsp, lk = _log_sigmoid_pair(_dot_nt(qs, k_ref[pl.ds(start, tq), :].astype(BF16)))
        if diagonal:
            causal = k_pos < q_pos
            lsp, lk = jnp.where(causal, lsp, MASK_BIAS), jnp.where(causal, lk, 0.0)
        r = r_ref[...]
        weights = []
        for j in reversed(range(tq // tk)):
            cols = slice(j * tk, (j + 1) * tk)
            sums = _dot(lk[:, cols].astype(BF16), u)
            later = sums[:, :tk] + jnp.concatenate([r] * (tk // LANES), axis=1)
            weights.append(jnp.exp(lsp[:, cols] + later).astype(BF16))
            r = r + sums[:, tk:]
        r_ref[...] = r
        a = jnp.concatenate(weights[::-1], axis=1)
        acc_ref[...] += _dot(a, v_ref[pl.ds(start, tq), :].astype(BF16))

    chunk(qi, True)

    def past(it, carry):
        chunk(qi - 1 - it, False)
        return carry

    lax.fori_loop(0, qi, past, 0)
    o_ref[...] = _unstack_heads(acc_ref[...], tq)


def _attention_specs(batch, seq_len, tq, q_sel, k_sel, v_sel):
    nq = seq_len // tq
    return dict(
        grid=(batch, N_HEAD_PAIRS, nq),
        in_specs=[
            pl.BlockSpec((None, tq, LANES), lambda b, hp, qi: (q_sel, b * nq + qi, hp)),
            pl.BlockSpec((None, seq_len, LANES), lambda b, hp, qi: (k_sel, b, hp)),
            pl.BlockSpec((None, seq_len, LANES), lambda b, hp, qi: (v_sel, b, hp)),
        ],
        out_specs=pl.BlockSpec((tq, LANES), lambda b, hp, qi: (b * nq + qi, hp)),
        out_shape=jax.ShapeDtypeStruct((batch * seq_len, D_MODEL), F32),
    )


def _sb_prompt(qkv3, batch, seq_len, tq=512, tk=256):
    tq = min(tq, seq_len)
    assert tq % tk == 0 and tq & (tq - 1) == 0
    return pl.pallas_call(
        functools.partial(_sb_prompt_kernel, tq=tq, tk=tk),
        scratch_shapes=[pltpu.VMEM((HEADS_PER_LANE_TILE * tq, LANES), F32)] * 2,
        compiler_params=_cparams("parallel", "parallel", "arbitrary"),
        name="sb_prompt",
        **_attention_specs(batch, seq_len, tq, 0, 1, 2),
    )(qkv3, qkv3, qkv3)


def _moba_prompt_kernel(q_ref, k_ref, v_ref, o_ref, kmean_ref, m_ref, l_ref, acc_ref, s_ref, *, n_blocks, group):
    qi = pl.program_id(2)
    blk = MOBA_BLOCK

    @pl.when(qi == 0)
    def _():
        kmean_ref[...] = jnp.zeros_like(kmean_ref)
        for n in range(n_blocks):
            kmean_ref[n:n + 1, :] = jnp.mean(k_ref[n * blk:(n + 1) * blk, :], axis=0, keepdims=True)

    qs = _stack_heads(q_ref[...] * QK_SCALE)
    rows = qs.shape[0]
    neg_inf = -jnp.inf
    lane = lax.broadcasted_iota(jnp.int32, (rows, LANES), 1)
    gate = jnp.where(lane < qi, _dot_nt(qs, kmean_ref[...].astype(BF16)), neg_inf)
    sel = jnp.zeros((rows, LANES), jnp.bool_)
    for r in range(MOBA_TOPK):
        best = jnp.max(gate, axis=-1, keepdims=True)
        idx = jnp.min(jnp.where(gate == best, lane, LANES), axis=-1, keepdims=True)
        pick = lane == idx
        sel = sel | (pick & (r < qi))
        gate = jnp.where(pick, neg_inf, gate)
    q_aug = jnp.concatenate([qs, jnp.where(sel, 0.0, MASK_BIAS).astype(BF16)], axis=1)

    q_pos = lax.broadcasted_iota(jnp.int32, (rows, blk), 0) & (blk - 1)
    k_pos = lax.broadcasted_iota(jnp.int32, (rows, blk), 1)
    start = pl.multiple_of(qi * blk, blk)
    def fold(scores, m, first_key):
        m_new = m
        for s in scores:
            m_new = jnp.maximum(m_new, jnp.max(s, axis=-1, keepdims=True))
        shift = jnp.concatenate([m_new] * (blk // LANES), axis=1)
        pv = 0.0
        for j, s in enumerate(scores):
            v = v_ref[pl.ds(pl.multiple_of(first_key + j * blk, blk), blk), :].astype(BF16)
            pv = pv + _dot(jnp.exp(s - shift).astype(BF16), jnp.concatenate([v, jnp.ones_like(v)], axis=1))
        return m_new, pv[:, :LANES], pv[:, LANES:]

    s = jnp.where(k_pos <= q_pos, _dot_nt(qs, k_ref[pl.ds(start, blk), :].astype(BF16)), neg_inf)
    m_ref[...], acc_ref[...], l_ref[...] = fold([s], jnp.full((rows, LANES), neg_inf, F32), start)

    k_lane = lax.broadcasted_iota(jnp.int32, (blk, LANES), 1)

    def score_group(i, slot):
        for j in range(group):
            keys = k_ref[pl.ds(pl.multiple_of((group * i + j) * blk, blk), blk), :].astype(BF16)
            one_hot = jnp.where(k_lane == group * i + j, 1.0, 0.0).astype(BF16)
            s_ref[slot, j] = _dot_nt(q_aug, jnp.concatenate([keys, one_hot], axis=1))

    n_groups = (qi + group - 1) // group

    def past_group(i, carry):
        slot = i % 2
        m = m_ref[...]
        m_new, pv, p_sum = fold([s_ref[slot, j] for j in range(group)], m, i * (group * blk))
        alpha = jnp.exp(m - m_new)
        m_ref[...] = m_new
        l_ref[...] = alpha * l_ref[...] + p_sum
        acc_ref[...] = alpha * acc_ref[...] + pv
        score_group(jnp.minimum(i + 1, n_groups - 1), 1 - slot)
        return carry

    score_group(0, 0)
    lax.fori_loop(0, n_groups, past_group, 0)
    o_ref[...] = _unstack_heads(acc_ref[...] / l_ref[...], blk)


def _moba_prompt(qk2, qkv3, batch, seq_len, group=4):
    n_blocks = seq_len // MOBA_BLOCK
    assert n_blocks <= LANES and n_blocks % group == 0
    specs = _attention_specs(batch, seq_len, MOBA_BLOCK, 0, 1, 2)
    rows = HEADS_PER_LANE_TILE * MOBA_BLOCK
    return pl.pallas_call(
        functools.partial(_moba_prompt_kernel, n_blocks=n_blocks, group=group),
        scratch_shapes=[pltpu.VMEM((LANES, LANES), F32)] + [pltpu.VMEM((rows, LANES), F32)] * 3
                       + [pltpu.VMEM((2, group, rows, MOBA_BLOCK), F32)],
        compiler_params=_cparams("parallel", "arbitrary", "arbitrary"),
        name="moba_prompt",
        **specs,
    )(qk2, qk2, qkv3)


PAGES_PER_STEP = 4
PAGES_PER_BLOCK = MOBA_BLOCK // PAGE_SIZE


PAGE_SHAPE = (N_HEADS, HEAD_DIM, PAGE_SIZE)
HEAD_ROW = (N_HEADS, PAGE_SIZE)
STEP_ROWS = (PAGES_PER_STEP * N_HEADS, PAGE_SIZE)


def _page_view(cache):
    return jnp.transpose(cache, (0, 1, 3, 4, 2))


def _head_columns(x):
    return x.reshape(x.shape[0], N_HEADS, HEAD_DIM, 1)


def _score_pages(z_ref, qb_ref, k_refs):
    for j, k_ref in enumerate(k_refs):
        for h in range(N_HEADS):
            row = j * N_HEADS + h
            z_ref[row:row + 1, :] = jnp.sum(k_ref[h] * qb_ref[h], axis=0, keepdims=True)


def _weigh_values(w_ref, v_refs, pages, h):
    return sum(w_ref[j * N_HEADS + h:j * N_HEADS + h + 1, :] * v_refs[j][h] for j in pages)


def _page_rows(x, j):
    return x[j * N_HEADS:(j + 1) * N_HEADS]


def _over_dims(x):
    return x[:, None, :]


def _lane_sum(x):
    return jnp.sum(x, axis=-1, keepdims=True)


def _suffix_sum_exclusive(x):
    lane = lax.broadcasted_iota(jnp.int32, x.shape, x.ndim - 1)
    inc = x
    d = 1
    while d < LANES:
        inc = inc + jnp.where(lane + d < LANES, pltpu.roll(inc, LANES - d, axis=x.ndim - 1), 0.0)
        d *= 2
    return inc - x


def _sb_decode_kernel(pt_ref, pos_ref, q_ref, kown_ref, vown_ref, *refs):
    k_refs, v_refs = refs[:PAGES_PER_STEP], refs[PAGES_PER_STEP:2 * PAGES_PER_STEP]
    o_ref, qb_ref, r_ref, acc_ref, z_ref, w_ref = refs[2 * PAGES_PER_STEP:]
    n = pl.program_id(1)

    @pl.when(n == 0)
    def _():
        q = q_ref[...] * QK_SCALE
        qb_ref[...] = jnp.broadcast_to(q, PAGE_SHAPE)
        valid = jnp.where(pos_ref[1] < pos_ref[0], 1.0, 0.0)
        lsp, lk = _log_sigmoid_pair(jnp.sum(q * kown_ref[...], axis=1))
        r_ref[...] = jnp.broadcast_to(lk * valid, HEAD_ROW)
        lane = lax.broadcasted_iota(jnp.int32, PAGE_SHAPE, 2)
        acc_ref[...] = jnp.where(lane == 0, _over_dims(jnp.exp(lsp) * valid) * vown_ref[...], 0.0)

    _score_pages(z_ref, qb_ref, k_refs)
    lsp, lk = _log_sigmoid_pair(z_ref[...])
    totals = _lane_sum(lk)
    r = r_ref[:, 0:1]
    later = [None] * PAGES_PER_STEP
    for j in reversed(range(PAGES_PER_STEP)):
        later[j] = r
        r = r + _page_rows(totals, j)
    r_ref[...] = jnp.broadcast_to(r, HEAD_ROW)
    w_ref[...] = jnp.exp(lsp + _suffix_sum_exclusive(lk) + jnp.concatenate(later, axis=0))
    for h in range(N_HEADS):
        acc_ref[h] += _weigh_values(w_ref, v_refs, range(PAGES_PER_STEP), h)

    @pl.when(n == pl.num_programs(1) - 1)
    def _():
        o_ref[...] = _lane_sum(acc_ref[...])


def _decode_specs(n_steps, layer, newest_first):
    def page(j):
        def index_map(b, n, pt, *_):
            step = (n_steps - 1 - n) if newest_first else n
            return (pt[b, PAGES_PER_STEP * step + j], layer, 0, 0, 0)
        return pl.BlockSpec((None, None) + PAGE_SHAPE, index_map)

    col = pl.BlockSpec((None, N_HEADS, HEAD_DIM, 1), lambda b, n, *_: (b, 0, 0, 0))
    return col, [page(j) for j in range(PAGES_PER_STEP)]


def _sb_decode(q, k_own, v_own, cache_k, cache_v, page_table, layer, q_pos, k_own_pos):
    batch = q.shape[0]
    n_steps = page_table.shape[1] // PAGES_PER_STEP
    col, pages = _decode_specs(n_steps, layer, True)
    pos = jnp.array([q_pos, k_own_pos], jnp.int32)
    ck, cv = _page_view(cache_k), _page_view(cache_v)
    out = pl.pallas_call(
        _sb_decode_kernel,
        out_shape=jax.ShapeDtypeStruct((batch, N_HEADS, HEAD_DIM, 1), F32),
        grid_spec=pltpu.PrefetchScalarGridSpec(
            num_scalar_prefetch=2,
            grid=(batch, n_steps),
            in_specs=[col, col, col] + pages + pages,
            out_specs=col,
            scratch_shapes=[pltpu.VMEM(PAGE_SHAPE, F32), pltpu.VMEM(HEAD_ROW, F32), pltpu.VMEM(PAGE_SHAPE, F32),
                            pltpu.VMEM(STEP_ROWS, F32), pltpu.VMEM(STEP_ROWS, F32)],
        ),
        compiler_params=_cparams("parallel", "arbitrary"),
        name="sb_decode",
    )(page_table, pos, _head_columns(q), _head_columns(k_own), _head_columns(v_own),
      *([ck] * PAGES_PER_STEP), *([cv] * PAGES_PER_STEP))
    return out.reshape(batch, D_MODEL)


def _moba_decode_kernel(pt_ref, q_ref, kown_ref, vown_ref, *refs):
    k_refs, v_refs = refs[:PAGES_PER_STEP], refs[PAGES_PER_STEP:2 * PAGES_PER_STEP]
    o_ref, qb_ref, gate_ref, m_ref, l_ref, pv_ref, z_ref, w_ref = refs[2 * PAGES_PER_STEP:]
    n = pl.program_id(1)
    n_blocks = gate_ref.shape[0]

    @pl.when(n == 0)
    def _():
        qb_ref[...] = jnp.broadcast_to(q_ref[...] * QK_SCALE, PAGE_SHAPE)

    _score_pages(z_ref, qb_ref, k_refs)
    s = z_ref[...]
    page_max = jnp.max(s, axis=-1, keepdims=True)
    page_sum = _lane_sum(s)
    blocks = [range(b * PAGES_PER_BLOCK, (b + 1) * PAGES_PER_BLOCK) for b in range(PAGES_PER_STEP // PAGES_PER_BLOCK)]
    over_block = lambda x, pages, op: functools.reduce(op, [_page_rows(x, j) for j in pages])
    block_max = [over_block(page_max, pages, jnp.maximum) for pages in blocks]
    w_ref[...] = jnp.exp(s - jnp.concatenate([block_max[j // PAGES_PER_BLOCK] for j in range(PAGES_PER_STEP)], axis=0))
    weight_sum = _lane_sum(w_ref[...])
    for b, pages in enumerate(blocks):
        idx = n * len(blocks) + b
        gate_ref[idx] = jnp.broadcast_to(over_block(page_sum, pages, jnp.add) * (1.0 / MOBA_BLOCK), HEAD_ROW)
        m_ref[idx] = jnp.broadcast_to(block_max[b], HEAD_ROW)
        l_ref[idx] = jnp.broadcast_to(over_block(weight_sum, pages, jnp.add), HEAD_ROW)
        for h in range(N_HEADS):
            pv_ref[idx, h] = _weigh_values(w_ref, v_refs, pages, h)

    @pl.when(n == pl.num_programs(1) - 1)
    def _():
        shape = gate_ref.shape
        blk = lax.broadcasted_iota(jnp.int32, shape, 0)
        gate = gate_ref[...]
        sel = jnp.zeros(shape, jnp.bool_)
        for r in range(MOBA_TOPK):
            best = jnp.max(gate, axis=0, keepdims=True)
            idx = jnp.min(jnp.where(gate == best, blk, shape[0]), axis=0, keepdims=True)
            pick = blk == idx
            sel = sel | (pick & (r < n_blocks))
            gate = jnp.where(pick, -jnp.inf, gate)
        q = q_ref[...] * QK_SCALE
        s_own = jnp.broadcast_to(jnp.sum(q * kown_ref[...], axis=1), HEAD_ROW)
        m_all = m_ref[...]
        m_fin = jnp.maximum(jnp.max(jnp.where(sel, m_all, -jnp.inf), axis=0), s_own)
        w = jnp.where(sel, jnp.exp(m_all - m_fin[None]), 0.0)
        w_own = jnp.exp(s_own - m_fin)
        denom = jnp.sum(w * l_ref[...], axis=0) + w_own
        gate_ref[...] = w
        numer = lax.fori_loop(0, n_blocks, lambda i, acc: acc + _over_dims(gate_ref[i]) * pv_ref[i],
                              jnp.zeros(PAGE_SHAPE, F32))
        first_lane = lambda t: _over_dims(t)[:, :, 0:1]
        o_ref[...] = (_lane_sum(numer) + first_lane(w_own) * vown_ref[...]) / first_lane(denom)


def _moba_decode(q, k_own, v_own, cache_k, cache_v, page_table, layer):
    batch = q.shape[0]
    n_pages = page_table.shape[1]
    assert PAGES_PER_STEP % PAGES_PER_BLOCK == 0 and n_pages % PAGES_PER_STEP == 0
    n_blocks = n_pages // PAGES_PER_BLOCK
    col, pages = _decode_specs(n_pages // PAGES_PER_STEP, layer, False)
    ck, cv = _page_view(cache_k), _page_view(cache_v)
    stat = pltpu.VMEM((n_blocks,) + HEAD_ROW, F32)
    out = pl.pallas_call(
        _moba_decode_kernel,
        out_shape=jax.ShapeDtypeStruct((batch, N_HEADS, HEAD_DIM, 1), F32),
        grid_spec=pltpu.PrefetchScalarGridSpec(
            num_scalar_prefetch=1,
            grid=(batch, n_pages // PAGES_PER_STEP),
            in_specs=[col, col, col] + pages + pages,
            out_specs=col,
            scratch_shapes=[pltpu.VMEM(PAGE_SHAPE, F32), stat, stat, stat,
                            pltpu.VMEM((n_blocks,) + PAGE_SHAPE, F32),
                            pltpu.VMEM(STEP_ROWS, F32), pltpu.VMEM(STEP_ROWS, F32)],
        ),
        compiler_params=_cparams("parallel", "arbitrary"),
        name="moba_decode",
    )(page_table, _head_columns(q), _head_columns(k_own), _head_columns(v_own),
      *([ck] * PAGES_PER_STEP), *([cv] * PAGES_PER_STEP))
    return out.reshape(batch, D_MODEL)


def kernel(x_prompt, x_sample, cache_sb_k, cache_sb_v, cache_moba_k, cache_moba_v, state_ffn_conv, page_table, p_prompt, p_sample, mix_norm, ffn_norm, ple_norm, sgu_w_in, sgu_v_norm, sgu_w_s, sgu_b_s, sgu_w_out, sb_w_qkv, sb_w_out, moba_w_qkv, moba_q_norm, moba_k_norm, moba_w_out, ffn_w_in, ffn_conv_w, ffn_conv_b, ffn_w_out, ple_w_in, ple_w_gate):
    bp, tp, _ = x_prompt.shape
    bs, ts, _ = x_sample.shape
    assert ts == 1 and tp % (2 * SGU_CHUNK) == 0 and tp % MOBA_BLOCK == 0
    depth = mix_norm.shape[0]
    past_len = page_table.shape[1] * PAGE_SIZE
    assert past_len % MOBA_BLOCK == 0 and past_len % SGU_CHUNK == 0
    xp = x_prompt.reshape(bp * tp, D_MODEL)
    xs = x_sample.reshape(bs, D_MODEL)
    bf = lambda w: w.astype(BF16)
    heads_p = lambda t: t.reshape(bp, tp, N_HEADS, HEAD_DIM)
    heads_s = lambda t: t.reshape(bs, ts, N_HEADS, HEAD_DIM)

    sb_kp, sb_vp, sb_ks, sb_vs = [], [], [], []
    mb_kp, mb_vp, mb_ks, mb_vs = [], [], [], []
    sgu_vs, conv_p, conv_s = [], [], []

    for i in range(depth):
        kind, j = i % N_MIXERS, i // N_MIXERS
        if kind == 0:
            w_in, w_out = bf(sgu_w_in[j]), bf(sgu_w_out[j])
            zs = _norm_matmul(xs, mix_norm[i], w_in, 2, act="gelu")
            xp = _sgu_prompt(xp, mix_norm[i], w_in, sgu_v_norm[j], sgu_w_s[j], sgu_b_s[j], w_out)
            xs, v_s = _sgu_sample(zs, sgu_v_norm[j], sgu_w_s[j], sgu_b_s[j], w_out, xs)
            sgu_vs.append(v_s.reshape(bs, ts, SGU_WIDTH))
        elif kind == 1:
            w_qkv, w_out = bf(sb_w_qkv[j]), bf(sb_w_out[j])
            qkv_p = _norm_matmul(xp, mix_norm[i], w_qkv, 3)
            qkv_s = _norm_matmul(xs, mix_norm[i], w_qkv, 3)
            xp = _proj_residual(_sb_prompt(qkv_p, bp, tp), w_out, xp)
            att = _sb_decode(qkv_s[0], qkv_s[1], qkv_s[2], cache_sb_k, cache_sb_v,
                             page_table, j, past_len, past_len)
            xs = _proj_residual(att, w_out, xs)
            sb_kp.append(heads_p(qkv_p[1]))
            sb_vp.append(heads_p(qkv_p[2]))
            sb_ks.append(heads_s(qkv_s[1]))
            sb_vs.append(heads_s(qkv_s[2]))
        else:
            w_qkv, w_out = bf(moba_w_qkv[j]), bf(moba_w_out[j])
            qkv_p = _norm_matmul(xp, mix_norm[i], w_qkv, 3)
            qkv_s = _norm_matmul(xs, mix_norm[i], w_qkv, 3)
            qk_p = _qknorm_rope(qkv_p, moba_q_norm[j], moba_k_norm[j], jnp.arange(tp, dtype=jnp.int32))
            qk_s = _qknorm_rope(qkv_s, moba_q_norm[j], moba_k_norm[j], jnp.full((bs,), past_len, jnp.int32))
            xp = _proj_residual(_moba_prompt(qk_p, qkv_p, bp, tp), w_out, xp)
            att = _moba_decode(qk_s[0], qk_s[1], qkv_s[2], cache_moba_k, cache_moba_v, page_table, j)
            xs = _proj_residual(att, w_out, xs)
            mb_kp.append(heads_p(qk_p[1]))
            mb_vp.append(heads_p(qkv_p[2]))
            mb_ks.append(heads_s(qk_s[1]))
            mb_vs.append(heads_s(qkv_s[2]))

        w_in, w_out = bf(ffn_w_in[i]), bf(ffn_w_out[i])
        gv_s = _norm_matmul(xs, ffn_norm[i], w_in, 2)
        xp, g_tails = _ffn_prompt(xp, ffn_norm[i], w_in, ffn_conv_w[i], ffn_conv_b[i], w_out, tp)
        xs = _ffn_sample(gv_s, state_ffn_conv[i], ffn_conv_w[i], ffn_conv_b[i], w_out, xs)
        conv_p.append(g_tails.reshape(bp, -1, SUBLANES, D_FF)[:, -1, SUBLANES - (CONV_W - 1):])
        conv_s.append(jnp.concatenate([state_ffn_conv[i][:, 1:], gv_s[0][:, None]], axis=1))

        w_gate, w_pin = bf(ple_w_gate[i]), bf(ple_w_in[i])
        xp = _ple(xp, p_prompt.reshape(depth, bp * tp, -1), i, ple_norm[i], w_gate, w_pin)
        xs = _ple(xs, p_sample.reshape(depth, bs, -1), i, ple_norm[i], w_gate, w_pin)

    return (xp.reshape(bp, tp, D_MODEL), xs.reshape(bs, ts, D_MODEL),
            jnp.stack(sb_kp, axis=1), jnp.stack(sb_vp, axis=1),
            jnp.stack(sb_ks, axis=1), jnp.stack(sb_vs, axis=1),
            jnp.stack(mb_kp, axis=1), jnp.stack(mb_vp, axis=1),
            jnp.stack(mb_ks, axis=1), jnp.stack(mb_vs, axis=1),
            jnp.stack(sgu_vs, axis=0), jnp.stack(conv_p, axis=0), jnp.stack(conv_s, axis=0))
```

```python
import functools

import jax
import jax.numpy as jnp
from jax import lax
from jax.experimental import pallas as pl
from jax.experimental.pallas import tpu as pltpu

F32 = jnp.float32
BF16 = jnp.bfloat16

D_MODEL = 1024
N_HEADS = 16
HEAD_DIM = 64
PAGE_SIZE = 128
SGU_CHUNK = 128
SGU_GROUPS = 8
SGU_WIDTH = 2 * D_MODEL
SGU_GROUP_DIM = SGU_WIDTH // SGU_GROUPS
MOBA_BLOCK = 256
MOBA_TOPK = 3
ROT_DIM = HEAD_DIM // 4
ROPE_THETA = 500000.0
D_FF = 2816
CONV_W = 3
EPS = 1e-6
N_MIXERS = 3

LANES = 128
HEADS_PER_LANE_TILE = LANES // HEAD_DIM
N_HEAD_PAIRS = D_MODEL // LANES
QK_SCALE = HEAD_DIM ** -0.5
MASK_BIAS = -1e30
EXP_UNDERFLOWS_BELOW = -104.0
VMEM_LIMIT = 48 * 1024 * 1024


def _cparams(*sem):
    return pltpu.CompilerParams(dimension_semantics=sem, vmem_limit_bytes=VMEM_LIMIT)


def _rms(x, g):
    return x * lax.rsqrt(jnp.mean(x * x, axis=-1, keepdims=True) + EPS) * g


def _dot(a, b):
    return jnp.dot(a, b, preferred_element_type=F32)


def _dot_nt(a, b):
    return lax.dot_general(a, b, (((1,), (1,)), ((), ())), preferred_element_type=F32)


def _split_bf16(x):
    hi = x.astype(BF16)
    lo = (x - hi.astype(F32)).astype(BF16)
    return jnp.concatenate([hi, lo], axis=1)


def _strict_upper_sum_matrix(n):
    j = lax.broadcasted_iota(jnp.int32, (n, n + LANES), 0)
    s = lax.broadcasted_iota(jnp.int32, (n, n + LANES), 1)
    return jnp.where((j > s) | (s >= n), 1.0, 0.0).astype(BF16)


def _log_sigmoid_pair(z):
    lsp = jnp.minimum(z, 0.0) - jnp.log(1.0 + jnp.exp(-jnp.abs(z)))
    return lsp, lsp - z


def _norm_matmul_kernel(x_ref, g_ref, w_ref, o_ref, xn_ref, *, act):
    @pl.when(pl.program_id(1) == 0)
    def _():
        xn_ref[...] = _rms(x_ref[...], g_ref[...]).astype(BF16)

    y = _dot(xn_ref[...], w_ref[...])
    if act == "gelu":
        y = jax.nn.gelu(y)
    o_ref[...] = y


def _lane_tile(n, cap):
    return max(t for t in range(LANES, cap + 1, LANES) if n % t == 0)


def _norm_matmul(x, gain, w, n_split, act=None, tm=512, tn_cap=1536):
    m, k = x.shape
    n_out = w.shape[1]
    per = n_out // n_split
    tm = min(tm, m)
    tn = _lane_tile(per, tn_cap)
    nj = per // tn
    assert m % tm == 0 and per % tn == 0
    return pl.pallas_call(
        functools.partial(_norm_matmul_kernel, act=act),
        out_shape=jax.ShapeDtypeStruct((n_split, m, per), F32),
        grid=(m // tm, n_out // tn),
        in_specs=[
            pl.BlockSpec((tm, k), lambda i, j: (i, 0)),
            pl.BlockSpec((1, k), lambda i, j: (0, 0)),
            pl.BlockSpec((k, tn), lambda i, j: (0, j)),
        ],
        out_specs=pl.BlockSpec((None, tm, tn), lambda i, j: (j // nj, i, j % nj)),
        scratch_shapes=[pltpu.VMEM((tm, k), BF16)],
        compiler_params=_cparams("parallel", "arbitrary"),
        name="norm_matmul",
    )(x, gain.reshape(1, k), w)


def _proj_residual_kernel(h_ref, w_ref, x_ref, o_ref):
    o_ref[...] = x_ref[...] + _dot(h_ref[...].astype(BF16), w_ref[...])


def _proj_residual(h, w, x, tm=1024):
    m, k = h.shape
    n = w.shape[1]
    tm = min(tm, m)
    return pl.pallas_call(
        _proj_residual_kernel,
        out_shape=jax.ShapeDtypeStruct((m, n), F32),
        grid=(m // tm,),
        in_specs=[
            pl.BlockSpec((tm, k), lambda i: (i, 0)),
            pl.BlockSpec((k, n), lambda i: (0, 0)),
            pl.BlockSpec((tm, n), lambda i: (i, 0)),
        ],
        out_specs=pl.BlockSpec((tm, n), lambda i: (i, 0)),
        compiler_params=_cparams("parallel"),
        name="proj_residual",
    )(h, w, x)


def _ple_kernel(x_ref, p_ref, g_ref, wg_ref, wp_ref, o_ref):
    x = x_ref[...]
    gate = jax.nn.sigmoid(_dot(_rms(x, g_ref[...]).astype(BF16), wg_ref[...]))
    o_ref[...] = x + gate * _dot(p_ref[...].astype(BF16), wp_ref[...])


def _ple(x, p_layers, layer, gain, w_gate, w_in, tm=512):
    m, d = x.shape
    pd = p_layers.shape[2]
    tm = min(tm, m)
    return pl.pallas_call(
        _ple_kernel,
        out_shape=jax.ShapeDtypeStruct((m, d), F32),
        grid=(m // tm,),
        in_specs=[
            pl.BlockSpec((tm, d), lambda i: (i, 0)),
            pl.BlockSpec((None, tm, pd), lambda i: (layer, i, 0)),
            pl.BlockSpec((1, d), lambda i: (0, 0)),
            pl.BlockSpec((d, d), lambda i: (0, 0)),
            pl.BlockSpec((pd, d), lambda i: (0, 0)),
        ],
        out_specs=pl.BlockSpec((tm, d), lambda i: (i, 0)),
        compiler_params=_cparams("parallel"),
        name="ple",
    )(x, p_layers, gain.reshape(1, d), w_gate, w_in)


def _sgu_prompt_kernel(x_ref, gain_ref, win_ref, vg_ref, ws_ref, bst_ref, wo_ref, o_ref, *, chunks):
    x = x_ref[...]
    xn = _rms(x, gain_ref[...]).astype(BF16)
    v = _rms(jax.nn.gelu(_dot(xn, win_ref[:, SGU_WIDTH:])), vg_ref[...]).astype(BF16)
    i = lax.broadcasted_iota(jnp.int32, (SGU_CHUNK, SGU_CHUNK), 0)
    j = lax.broadcasted_iota(jnp.int32, (SGU_CHUNK, SGU_CHUNK), 1)
    acc = x
    for g in range(SGU_GROUPS):
        cols = slice(g * SGU_GROUP_DIM, (g + 1) * SGU_GROUP_DIM)
        w = jnp.where(i >= j, ws_ref[g], 0.0).astype(BF16)
        bias = bst_ref[:, g:g + 1]
        mixed = jnp.concatenate(
            [_dot(w, v[c * SGU_CHUNK:(c + 1) * SGU_CHUNK, cols]) + bias for c in range(chunks)], axis=0)
        u = jax.nn.gelu(_dot(xn, win_ref[:, cols]))
        acc = acc + _dot((u * mixed).astype(BF16), wo_ref[cols, :])
    o_ref[...] = acc


def _sgu_prompt(x, gain, w_in, v_gain, w_s, b_s, w_out, chunks=4):
    m = x.shape[0]
    tm = chunks * SGU_CHUNK
    return pl.pallas_call(
        functools.partial(_sgu_prompt_kernel, chunks=chunks),
        out_shape=jax.ShapeDtypeStruct((m, D_MODEL), F32),
        grid=(m // tm,),
        in_specs=[
            pl.BlockSpec((tm, D_MODEL), lambda i: (i, 0)),
            _resident((1, D_MODEL)),
            _resident((D_MODEL, 2 * SGU_WIDTH)),
            _resident((1, SGU_WIDTH)),
            _resident((SGU_GROUPS, SGU_CHUNK, SGU_CHUNK)),
            _resident((SGU_CHUNK, SGU_GROUPS)),
            _resident((SGU_WIDTH, D_MODEL)),
        ],
        out_specs=pl.BlockSpec((tm, D_MODEL), lambda i: (i, 0)),
        compiler_params=_cparams("parallel"),
        name="sgu_prompt",
    )(x, gain.reshape(1, D_MODEL), w_in, v_gain.reshape(1, SGU_WIDTH), w_s, b_s.T, w_out)


def _sgu_sample_kernel(u_ref, v_ref, vg_ref, w0_ref, b0_ref, wo_ref, x_ref, o_ref, vn_ref):
    v = _rms(v_ref[...], vg_ref[...])
    vn_ref[...] = v
    gated = u_ref[...] * (w0_ref[...] * v.astype(BF16).astype(F32) + b0_ref[...])
    o_ref[...] = x_ref[...] + _dot(gated.astype(BF16), wo_ref[...])


def _sgu_sample(z2, v_gain, w_s, b_s, w_out, x):
    m = x.shape[0]
    w0 = jnp.repeat(w_s[:, 0, 0].astype(BF16).astype(F32), SGU_GROUP_DIM).reshape(1, SGU_WIDTH)
    b0 = jnp.repeat(b_s[:, 0], SGU_GROUP_DIM).reshape(1, SGU_WIDTH)
    row = lambda i: (0, 0)
    return pl.pallas_call(
        _sgu_sample_kernel,
        out_shape=(jax.ShapeDtypeStruct((m, D_MODEL), F32), jax.ShapeDtypeStruct((m, SGU_WIDTH), F32)),
        grid=(1,),
        in_specs=[
            pl.BlockSpec((None, m, SGU_WIDTH), lambda i: (0, 0, 0)),
            pl.BlockSpec((None, m, SGU_WIDTH), lambda i: (1, 0, 0)),
            pl.BlockSpec((1, SGU_WIDTH), row),
            pl.BlockSpec((1, SGU_WIDTH), row),
            pl.BlockSpec((1, SGU_WIDTH), row),
            pl.BlockSpec((SGU_WIDTH, D_MODEL), row),
            pl.BlockSpec((m, D_MODEL), row),
        ],
        out_specs=(pl.BlockSpec((m, D_MODEL), row), pl.BlockSpec((m, SGU_WIDTH), row)),
        compiler_params=_cparams("arbitrary"),
        name="sgu_sample",
    )(z2, z2, v_gain.reshape(1, SGU_WIDTH), w0, b0, w_out, x)


def _ffn_gate(c, val):
    return (jax.nn.silu(c) * val).astype(BF16)


SUBLANES = 8
FFN_CHUNK = 256


def _resident(shape):
    return pl.BlockSpec(shape, lambda i: (0,) * len(shape), pipeline_mode=pl.Buffered(1))


def _ffn_prompt_kernel(x_ref, gain_ref, win_ref, cw_ref, cb_ref, wo_ref, o_ref, tail_ref, prev_ref, *,
                       tiles_per_seq):
    x = x_ref[...]
    tm = x.shape[0]
    xn = _rms(x, gain_ref[...]).astype(BF16)
    @pl.when(pl.program_id(0) % tiles_per_seq == 0)
    def _():
        prev_ref[...] = jnp.zeros_like(prev_ref)

    row = lax.broadcasted_iota(jnp.int32, (tm, 1), 0)
    acc = x
    for c0 in range(0, D_FF, FFN_CHUNK):
        cols = slice(c0, c0 + FFN_CHUNK)
        g = _dot(xn, win_ref[:, cols])
        val = _dot(xn, win_ref[:, D_FF + c0:D_FF + c0 + FFN_CHUNK])
        prev = prev_ref[:, cols]
        g1 = jnp.where(row == 0, prev[7:8], pltpu.roll(g, 1, axis=0))
        g2 = jnp.where(row == 0, prev[6:7], jnp.where(row == 1, prev[7:8], pltpu.roll(g, 2, axis=0)))
        c = cb_ref[:, cols] + cw_ref[0:1, cols] * g2 + cw_ref[1:2, cols] * g1 + cw_ref[2:3, cols] * g
        acc = acc + _dot(_ffn_gate(c, val), wo_ref[cols, :])
        prev_ref[:, cols] = g[tm - SUBLANES:, :]
        tail_ref[:, cols] = g[tm - SUBLANES:, :]
    o_ref[...] = acc


def _ffn_prompt(x, gain, w_in, conv_w, conv_b, w_out, seq_len, tm=512):
    m = x.shape[0]
    assert seq_len % tm == 0 and D_FF % FFN_CHUNK == 0
    return pl.pallas_call(
        functools.partial(_ffn_prompt_kernel, tiles_per_seq=seq_len // tm),
        out_shape=(jax.ShapeDtypeStruct((m, D_MODEL), F32),
                   jax.ShapeDtypeStruct((m // tm, SUBLANES, D_FF), F32)),
        grid=(m // tm,),
        in_specs=[
            pl.BlockSpec((tm, D_MODEL), lambda i: (i, 0)),
            _resident((1, D_MODEL)),
            _resident((D_MODEL, 2 * D_FF)),
            _resident((CONV_W, D_FF)),
            _resident((1, D_FF)),
            _resident((D_FF, D_MODEL)),
        ],
        out_specs=(pl.BlockSpec((tm, D_MODEL), lambda i: (i, 0)),
                   pl.BlockSpec((None, SUBLANES, D_FF), lambda i: (i, 0, 0))),
        scratch_shapes=[pltpu.VMEM((SUBLANES, D_FF), F32)],
        compiler_params=_cparams("arbitrary"),
        name="ffn_prompt",
    )(x, gain.reshape(1, D_MODEL), w_in, conv_w, conv_b.reshape(1, D_FF), w_out)


def _ffn_sample_kernel(g_ref, val_ref, s0_ref, s1_ref, cw_ref, cb_ref, wo_ref, x_ref, o_ref):
    c = cb_ref[...] + cw_ref[0:1] * s0_ref[...] + cw_ref[1:2] * s1_ref[...] + cw_ref[2:3] * g_ref[...]
    o_ref[...] = x_ref[...] + _dot(_ffn_gate(c, val_ref[...]), wo_ref[...])


def _ffn_sample(gv2, state, conv_w, conv_b, w_out, x):
    m = x.shape[0]
    full = lambda i: (0, 0)
    return pl.pallas_call(
        _ffn_sample_kernel,
        out_shape=jax.ShapeDtypeStruct((m, D_MODEL), F32),
        grid=(1,),
        in_specs=[
            pl.BlockSpec((None, m, D_FF), lambda i: (0, 0, 0)),
            pl.BlockSpec((None, m, D_FF), lambda i: (1, 0, 0)),
            pl.BlockSpec((m, D_FF), full),
            pl.BlockSpec((m, D_FF), full),
            pl.BlockSpec((CONV_W, D_FF), full),
            pl.BlockSpec((1, D_FF), full),
            pl.BlockSpec((D_FF, D_MODEL), full),
            pl.BlockSpec((m, D_MODEL), full),
        ],
        out_specs=pl.BlockSpec((m, D_MODEL), full),
        compiler_params=_cparams("arbitrary"),
        name="ffn_sample",
    )(gv2, gv2, state[:, 0], state[:, 1], conv_w, conv_b.reshape(1, D_FF), w_out, x)


def _qknorm_rope_kernel(x_ref, gain_ref, c_ref, s1_ref, s2_ref, bd_ref, o_ref):
    x = x_ref[...]
    head_sumsq = _dot(_split_bf16(x * x), bd_ref[...])
    y = x * lax.rsqrt(head_sumsq * (1.0 / HEAD_DIM) + EPS) * gain_ref[...]
    reps = D_MODEL // LANES
    tile = lambda t: jnp.concatenate([t] * reps, axis=1)
    half = ROT_DIM // 2
    o_ref[...] = (y * tile(c_ref[...])
                  + pltpu.roll(y, D_MODEL - half, axis=1) * tile(s1_ref[...])
                  + pltpu.roll(y, half, axis=1) * tile(s2_ref[...]))


def _rope_tables(pos):
    half = ROT_DIM // 2
    inv_freq = ROPE_THETA ** (-jnp.arange(0, ROT_DIM, 2, dtype=F32) / ROT_DIM)
    ang = pos.astype(F32)[:, None] * inv_freq[None, :]
    cos, sin = jnp.cos(ang), jnp.sin(ang)
    n = pos.shape[0]
    pad = jnp.zeros((n, HEAD_DIM - ROT_DIM), F32)
    zeros = jnp.zeros((n, half), F32)
    c = jnp.concatenate([cos, cos, pad + 1.0], axis=1)
    s1 = jnp.concatenate([-sin, zeros, pad], axis=1)
    s2 = jnp.concatenate([zeros, sin, pad], axis=1)
    rep = lambda t: jnp.concatenate([t] * HEADS_PER_LANE_TILE, axis=1)
    return rep(c), rep(s1), rep(s2)


def _qknorm_rope(qkv3, q_gain, k_gain, pos, tm=512):
    m = qkv3.shape[1]
    t = pos.shape[0]
    tm = min(tm, m, t)
    tiles_per_seq = t // tm
    gains = jnp.stack([jnp.tile(q_gain, N_HEADS), jnp.tile(k_gain, N_HEADS)]).reshape(2, 1, D_MODEL)
    c, s1, s2 = _rope_tables(pos)
    head = jnp.arange(D_MODEL) // HEAD_DIM
    bd = (head[:, None] == head[None, :]).astype(BF16)
    bd2 = jnp.concatenate([bd, bd], axis=0)
    tab = pl.BlockSpec((tm, LANES), lambda s, i: (i % tiles_per_seq, 0))
    return pl.pallas_call(
        _qknorm_rope_kernel,
        out_shape=jax.ShapeDtypeStruct((2, m, D_MODEL), F32),
        grid=(2, m // tm),
        in_specs=[
            pl.BlockSpec((None, tm, D_MODEL), lambda s, i: (s, i, 0)),
            pl.BlockSpec((None, 1, D_MODEL), lambda s, i: (s, 0, 0)),
            tab, tab, tab,
            pl.BlockSpec((2 * D_MODEL, D_MODEL), lambda s, i: (0, 0)),
        ],
        out_specs=pl.BlockSpec((None, tm, D_MODEL), lambda s, i: (s, i, 0)),
        compiler_params=_cparams("parallel", "parallel"),
        name="qknorm_rope",
    )(qkv3, gains, c, s1, s2, bd2)


def _head_lane_masks():
    lane = lax.broadcasted_iota(jnp.int32, (1, LANES), 1)
    return [(lane // HEAD_DIM) == h for h in range(HEADS_PER_LANE_TILE)]


def _stack_heads(q):
    return jnp.concatenate([jnp.where(hm, q, 0.0) for hm in _head_lane_masks()], axis=0).astype(BF16)


def _unstack_heads(acc, tq):
    return sum(jnp.where(hm, acc[h * tq:(h + 1) * tq], 0.0) for h, hm in enumerate(_head_lane_masks()))


def _sb_prompt_kernel(q_ref, k_ref, v_ref, o_ref, r_ref, acc_ref, *, tq, tk):
    qi = pl.program_id(2)
    qs = _stack_heads(q_ref[...] * QK_SCALE)
    rows = qs.shape[0]
    u = _strict_upper_sum_matrix(tk)
    q_pos = lax.broadcasted_iota(jnp.int32, (rows, tq), 0) & (tq - 1)
    k_pos = lax.broadcasted_iota(jnp.int32, (rows, tq), 1)
    r_ref[...] = jnp.zeros_like(r_ref)
    acc_ref[...] = jnp.zeros_like(acc_ref)

    def chunk(c, diagonal):
        start = pl.multiple_of(c * tq, tq)
        lsp, lk = _log_sigmoid_pair(_dot_nt(qs, k_ref[pl.ds(start, tq), :].astype(BF16)))
        if diagonal:
            causal = k_pos < q_pos
            lsp, lk = jnp.where(causal, lsp, MASK_BIAS), jnp.where(causal, lk, 0.0)
        r = r_ref[...]
        weights = []
        for j in reversed(range(tq // tk)):
            cols = slice(j * tk, (j + 1) * tk)
            sums = _dot(lk[:, cols].astype(BF16), u)
            later = sums[:, :tk] + jnp.concatenate([r] * (tk // LANES), axis=1)
            weights.append(jnp.exp(lsp[:, cols] + later).astype(BF16))
            r = r + sums[:, tk:]
        r_ref[...] = r
        a = jnp.concatenate(weights[::-1], axis=1)
        acc_ref[...] += _dot(a, v_ref[pl.ds(start, tq), :].astype(BF16))
        return jnp.max(r)

    def older(state):
        c, _ = state
        return c - 1, chunk(c, False)

    lax.while_loop(lambda state: (state[0] >= 0) & (state[1] > EXP_UNDERFLOWS_BELOW), older,
                   (qi - 1, chunk(qi, True)))
    o_ref[...] = _unstack_heads(acc_ref[...], tq)


def _attention_specs(batch, seq_len, tq, q_sel, k_sel, v_sel):
    nq = seq_len // tq
    return dict(
        grid=(batch, N_HEAD_PAIRS, nq),
        in_specs=[
            pl.BlockSpec((None, tq, LANES), lambda b, hp, qi: (q_sel, b * nq + qi, hp)),
            pl.BlockSpec((None, seq_len, LANES), lambda b, hp, qi: (k_sel, b, hp)),
            pl.BlockSpec((None, seq_len, LANES), lambda b, hp, qi: (v_sel, b, hp)),
        ],
        out_specs=pl.BlockSpec((tq, LANES), lambda b, hp, qi: (b * nq + qi, hp)),
        out_shape=jax.ShapeDtypeStruct((batch * seq_len, D_MODEL), F32),
    )


def _sb_prompt(qkv3, batch, seq_len, tq=512, tk=256):
    tq = min(tq, seq_len)
    assert tq % tk == 0 and tq & (tq - 1) == 0
    return pl.pallas_call(
        functools.partial(_sb_prompt_kernel, tq=tq, tk=tk),
        scratch_shapes=[pltpu.VMEM((HEADS_PER_LANE_TILE * tq, LANES), F32)] * 2,
        compiler_params=_cparams("parallel", "parallel", "arbitrary"),
        name="sb_prompt",
        **_attention_specs(batch, seq_len, tq, 0, 1, 2),
    )(qkv3, qkv3, qkv3)


def _moba_prompt_kernel(q_ref, k_ref, v_ref, o_ref, kmean_ref, m_ref, l_ref, acc_ref, s0_ref, s1_ref, *,
                        n_blocks, group):
    qi = pl.program_id(2)
    blk = MOBA_BLOCK

    @pl.when(qi == 0)
    def _():
        kmean_ref[...] = jnp.zeros_like(kmean_ref)
        for n in range(n_blocks):
            kmean_ref[n:n + 1, :] = jnp.mean(k_ref[n * blk:(n + 1) * blk, :], axis=0, keepdims=True)

    qs = _stack_heads(q_ref[...] * QK_SCALE)
    rows = qs.shape[0]
    neg_inf = -jnp.inf
    lane = lax.broadcasted_iota(jnp.int32, (rows, LANES), 1)
    gate = jnp.where(lane < qi, _dot_nt(qs, kmean_ref[...].astype(BF16)), neg_inf)
    sel = jnp.zeros((rows, LANES), jnp.bool_)
    for r in range(MOBA_TOPK):
        best = jnp.max(gate, axis=-1, keepdims=True)
        idx = jnp.min(jnp.where(gate == best, lane, LANES), axis=-1, keepdims=True)
        pick = lane == idx
        sel = sel | (pick & (r < qi))
        gate = jnp.where(pick, neg_inf, gate)
    q_aug = jnp.concatenate([qs, jnp.where(sel, 0.0, MASK_BIAS).astype(BF16)], axis=1)

    q_pos = lax.broadcasted_iota(jnp.int32, (rows, blk), 0) & (blk - 1)
    k_pos = lax.broadcasted_iota(jnp.int32, (rows, blk), 1)
    start = pl.multiple_of(qi * blk, blk)
    def fold(scores, m, first_key):
        m_new = m
        for s in scores:
            m_new = jnp.maximum(m_new, jnp.max(s, axis=-1, keepdims=True))
        shift = jnp.concatenate([m_new] * (blk // LANES), axis=1)
        pv = 0.0
        for j, s in enumerate(scores):
            v = v_ref[pl.ds(pl.multiple_of(first_key + j * blk, blk), blk), :].astype(BF16)
            pv = pv + _dot(jnp.exp(s - shift).astype(BF16), jnp.concatenate([v, jnp.ones_like(v)], axis=1))
        return m_new, pv[:, :LANES], pv[:, LANES:]

    s = jnp.where(k_pos <= q_pos, _dot_nt(qs, k_ref[pl.ds(start, blk), :].astype(BF16)), neg_inf)
    m_ref[...], acc_ref[...], l_ref[...] = fold([s], jnp.full((rows, LANES), neg_inf, F32), start)

    k_lane = lax.broadcasted_iota(jnp.int32, (blk, LANES), 1)

    def score_group(i, s_ref):
        for j in range(group):
            keys = k_ref[pl.ds(pl.multiple_of((group * i + j) * blk, blk), blk), :].astype(BF16)
            one_hot = jnp.where(k_lane == group * i + j, 1.0, 0.0).astype(BF16)
            s_ref[j] = _dot_nt(q_aug, jnp.concatenate([keys, one_hot], axis=1))

    def fold_group(i, s_ref):
        m = m_ref[...]
        m_new, pv, p_sum = fold([s_ref[j] for j in range(group)], m, i * (group * blk))
        alpha = jnp.exp(m - m_new)
        m_ref[...] = m_new
        l_ref[...] = alpha * l_ref[...] + p_sum
        acc_ref[...] = alpha * acc_ref[...] + pv

    last_group = n_blocks // group - 1

    def group_pair(p, carry):
        score_group(2 * p + 1, s1_ref)
        fold_group(2 * p, s0_ref)
        score_group(jnp.minimum(2 * p + 2, last_group), s0_ref)
        fold_group(2 * p + 1, s1_ref)
        return carry

    score_group(0, s0_ref)
    lax.fori_loop(0, (qi + 2 * group - 1) // (2 * group), group_pair, 0)
    o_ref[...] = _unstack_heads(acc_ref[...] / l_ref[...], blk)


def _moba_prompt(qk2, qkv3, batch, seq_len, group=2):
    n_blocks = seq_len // MOBA_BLOCK
    assert n_blocks <= LANES and n_blocks % (2 * group) == 0
    specs = _attention_specs(batch, seq_len, MOBA_BLOCK, 0, 1, 2)
    rows = HEADS_PER_LANE_TILE * MOBA_BLOCK
    return pl.pallas_call(
        functools.partial(_moba_prompt_kernel, n_blocks=n_blocks, group=group),
        scratch_shapes=[pltpu.VMEM((LANES, LANES), F32)] + [pltpu.VMEM((rows, LANES), F32)] * 3
                       + [pltpu.VMEM((group, rows, MOBA_BLOCK), F32)] * 2,
        compiler_params=_cparams("parallel", "arbitrary", "arbitrary"),
        name="moba_prompt",
        **specs,
    )(qk2, qk2, qkv3)


PAGES_PER_STEP = 4
PAGES_PER_BLOCK = MOBA_BLOCK // PAGE_SIZE


PAGE_SHAPE = (N_HEADS, HEAD_DIM, PAGE_SIZE)
HEAD_ROW = (N_HEADS, PAGE_SIZE)
STEP_ROWS = (PAGES_PER_STEP * N_HEADS, PAGE_SIZE)


def _page_view(cache):
    return jnp.transpose(cache, (0, 1, 3, 4, 2))


def _head_columns(x):
    return x.reshape(x.shape[0], N_HEADS, HEAD_DIM, 1)


def _score_pages(z_ref, qb_ref, k_refs):
    for j, k_ref in enumerate(k_refs):
        for h in range(N_HEADS):
            row = j * N_HEADS + h
            z_ref[row:row + 1, :] = jnp.sum(k_ref[h] * qb_ref[h], axis=0, keepdims=True)


def _weigh_values(w_ref, v_refs, pages, h):
    return sum(w_ref[j * N_HEADS + h:j * N_HEADS + h + 1, :] * v_refs[j][h] for j in pages)


def _page_rows(x, j):
    return x[j * N_HEADS:(j + 1) * N_HEADS]


def _over_dims(x):
    return x[:, None, :]


def _lane_sum(x):
    return jnp.sum(x, axis=-1, keepdims=True)


def _suffix_sum_exclusive(x):
    lane = lax.broadcasted_iota(jnp.int32, x.shape, x.ndim - 1)
    inc = x
    d = 1
    while d < LANES:
        inc = inc + jnp.where(lane + d < LANES, pltpu.roll(inc, LANES - d, axis=x.ndim - 1), 0.0)
        d *= 2
    return inc - x


def _sb_decode_kernel(pt_ref, pos_ref, live_ref, q_ref, kown_ref, vown_ref, r_in_ref, part_in_ref, *refs,
                      newest):
    k_refs, v_refs = refs[:PAGES_PER_STEP], refs[PAGES_PER_STEP:2 * PAGES_PER_STEP]
    o_ref, r_out_ref, qb_ref, r_ref, acc_ref, z_ref, w_ref = refs[2 * PAGES_PER_STEP:]
    n = pl.program_id(1)

    @pl.when(n == 0)
    def _():
        q = q_ref[...] * QK_SCALE
        qb_ref[...] = jnp.broadcast_to(q, PAGE_SHAPE)
        if newest:
            valid = jnp.where(pos_ref[1] < pos_ref[0], 1.0, 0.0)
            lsp, lk = _log_sigmoid_pair(jnp.sum(q * kown_ref[...], axis=1))
            r_ref[...] = jnp.broadcast_to(lk * valid, HEAD_ROW)
            lane = lax.broadcasted_iota(jnp.int32, PAGE_SHAPE, 2)
            acc_ref[...] = jnp.where(lane == 0, _over_dims(jnp.exp(lsp) * valid) * vown_ref[...], 0.0)
        else:
            r_ref[...] = r_in_ref[...]
            acc_ref[...] = jnp.zeros_like(acc_ref)

    @pl.when(live_ref[pl.program_id(0)] != 0)
    def _():
        _score_pages(z_ref, qb_ref, k_refs)
        lsp, lk = _log_sigmoid_pair(z_ref[...])
        totals = _lane_sum(lk)
        r = r_ref[:, 0:1]
        later = [None] * PAGES_PER_STEP
        for j in reversed(range(PAGES_PER_STEP)):
            later[j] = r
            r = r + _page_rows(totals, j)
        r_ref[...] = jnp.broadcast_to(r, HEAD_ROW)
        w_ref[...] = jnp.exp(lsp + _suffix_sum_exclusive(lk) + jnp.concatenate(later, axis=0))
        for h in range(N_HEADS):
            acc_ref[h] += _weigh_values(w_ref, v_refs, range(PAGES_PER_STEP), h)

    @pl.when(n == pl.num_programs(1) - 1)
    def _():
        o_ref[...] = part_in_ref[...] + _lane_sum(acc_ref[...])
        r_out_ref[...] = r_ref[...]


def _decode_specs(n_steps, layer, newest_first):
    def page(j):
        def index_map(b, n, pt, *_):
            step = (n_steps - 1 - n) if newest_first else n
            return (pt[b, PAGES_PER_STEP * step + j], layer, 0, 0, 0)
        return pl.BlockSpec((None, None) + PAGE_SHAPE, index_map)

    col = pl.BlockSpec((None, N_HEADS, HEAD_DIM, 1), lambda b, n, *_: (b, 0, 0, 0))
    return col, [page(j) for j in range(PAGES_PER_STEP)]


def _sb_decode(q, k_own, v_own, cache_k, cache_v, page_table, layer, q_pos, k_own_pos):
    batch = q.shape[0]
    n_steps = page_table.shape[1] // PAGES_PER_STEP
    pos = jnp.array([q_pos, k_own_pos], jnp.int32)
    ck, cv = _page_view(cache_k), _page_view(cache_v)
    col = pl.BlockSpec((None, N_HEADS, HEAD_DIM, 1), lambda b, n, *_: (b, 0, 0, 0))
    row = pl.BlockSpec((None,) + HEAD_ROW, lambda b, n, *_: (b, 0, 0))
    cols = [_head_columns(t) for t in (q, k_own, v_own)]

    def walk(first_step, steps, live, r_in, part_in, newest):
        def page(j):
            def index_map(b, n, pt, pos, live):
                wanted = pt[b, PAGES_PER_STEP * (first_step - n) + j]
                return (jnp.where(live[b] != 0, wanted, pt[0, j]), layer, 0, 0, 0)
            return pl.BlockSpec((None, None) + PAGE_SHAPE, index_map)

        pages = [page(j) for j in range(PAGES_PER_STEP)]
        return pl.pallas_call(
            functools.partial(_sb_decode_kernel, newest=newest),
            out_shape=(jax.ShapeDtypeStruct((batch, N_HEADS, HEAD_DIM, 1), F32),
                       jax.ShapeDtypeStruct((batch,) + HEAD_ROW, F32)),
            grid_spec=pltpu.PrefetchScalarGridSpec(
                num_scalar_prefetch=3,
                grid=(batch, steps),
                in_specs=[col, col, col, row, col] + pages + pages,
                out_specs=(col, row),
                scratch_shapes=[pltpu.VMEM(PAGE_SHAPE, F32), pltpu.VMEM(HEAD_ROW, F32), pltpu.VMEM(PAGE_SHAPE, F32),
                                pltpu.VMEM(STEP_ROWS, F32), pltpu.VMEM(STEP_ROWS, F32)],
            ),
            compiler_params=_cparams("parallel", "arbitrary"),
            name="sb_decode",
        )(page_table, pos, live, *cols, r_in, part_in, *([ck] * PAGES_PER_STEP), *([cv] * PAGES_PER_STEP))

    zeros_r = jnp.zeros((batch,) + HEAD_ROW, F32)
    zeros_part = jnp.zeros((batch, N_HEADS, HEAD_DIM, 1), F32)
    part, r = walk(n_steps - 1, 1, jnp.ones((batch,), jnp.int32), zeros_r, zeros_part, True)
    if n_steps > 1:
        live = (jnp.max(r, axis=(1, 2)) > EXP_UNDERFLOWS_BELOW).astype(jnp.int32)
        part = lax.cond(jnp.any(live != 0),
                        lambda: walk(n_steps - 2, n_steps - 1, live, r, part, False)[0],
                        lambda: part)
    return part.reshape(batch, D_MODEL)


def _moba_decode_kernel(pt_ref, q_ref, kown_ref, vown_ref, *refs):
    k_refs, v_refs = refs[:PAGES_PER_STEP], refs[PAGES_PER_STEP:2 * PAGES_PER_STEP]
    o_ref, qb_ref, gate_ref, m_ref, l_ref, pv_ref, z_ref, w_ref = refs[2 * PAGES_PER_STEP:]
    n = pl.program_id(1)
    n_blocks = gate_ref.shape[0]

    @pl.when(n == 0)
    def _():
        qb_ref[...] = jnp.broadcast_to(q_ref[...] * QK_SCALE, PAGE_SHAPE)

    _score_pages(z_ref, qb_ref, k_refs)
    s = z_ref[...]
    page_max = jnp.max(s, axis=-1, keepdims=True)
    page_sum = _lane_sum(s)
    blocks = [range(b * PAGES_PER_BLOCK, (b + 1) * PAGES_PER_BLOCK) for b in range(PAGES_PER_STEP // PAGES_PER_BLOCK)]
    over_block = lambda x, pages, op: functools.reduce(op, [_page_rows(x, j) for j in pages])
    block_max = [over_block(page_max, pages, jnp.maximum) for pages in blocks]
    w_ref[...] = jnp.exp(s - jnp.concatenate([block_max[j // PAGES_PER_BLOCK] for j in range(PAGES_PER_STEP)], axis=0))
    weight_sum = _lane_sum(w_ref[...])
    for b, pages in enumerate(blocks):
        idx = n * len(blocks) + b
        gate_ref[idx] = jnp.broadcast_to(over_block(page_sum, pages, jnp.add) * (1.0 / MOBA_BLOCK), HEAD_ROW)
        m_ref[idx] = jnp.broadcast_to(block_max[b], HEAD_ROW)
        l_ref[idx] = jnp.broadcast_to(over_block(weight_sum, pages, jnp.add), HEAD_ROW)
        for h in range(N_HEADS):
            pv_ref[idx, h] = _weigh_values(w_ref, v_refs, pages, h)

    @pl.when(n == pl.num_programs(1) - 1)
    def _():
        shape = gate_ref.shape
        blk = lax.broadcasted_iota(jnp.int32, shape, 0)
        gate = gate_ref[...]
        sel = jnp.zeros(shape, jnp.bool_)
        for r in range(MOBA_TOPK):
            best = jnp.max(gate, axis=0, keepdims=True)
            idx = jnp.min(jnp.where(gate == best, blk, shape[0]), axis=0, keepdims=True)
            pick = blk == idx
            sel = sel | (pick & (r < n_blocks))
            gate = jnp.where(pick, -jnp.inf, gate)
        q = q_ref[...] * QK_SCALE
        s_own = jnp.broadcast_to(jnp.sum(q * kown_ref[...], axis=1), HEAD_ROW)
        m_all = m_ref[...]
        m_fin = jnp.maximum(jnp.max(jnp.where(sel, m_all, -jnp.inf), axis=0), s_own)
        w = jnp.where(sel, jnp.exp(m_all - m_fin[None]), 0.0)
        w_own = jnp.exp(s_own - m_fin)
        denom = jnp.sum(w * l_ref[...], axis=0) + w_own
        gate_ref[...] = w
        numer = lax.fori_loop(0, n_blocks, lambda i, acc: acc + _over_dims(gate_ref[i]) * pv_ref[i],
                              jnp.zeros(PAGE_SHAPE, F32))
        first_lane = lambda t: _over_dims(t)[:, :, 0:1]
        o_ref[...] = (_lane_sum(numer) + first_lane(w_own) * vown_ref[...]) / first_lane(denom)


def _moba_decode(q, k_own, v_own, cache_k, cache_v, page_table, layer):
    batch = q.shape[0]
    n_pages = page_table.shape[1]
    assert PAGES_PER_STEP % PAGES_PER_BLOCK == 0 and n_pages % PAGES_PER_STEP == 0
    n_blocks = n_pages // PAGES_PER_BLOCK
    col, pages = _decode_specs(n_pages // PAGES_PER_STEP, layer, False)
    ck, cv = _page_view(cache_k), _page_view(cache_v)
    stat = pltpu.VMEM((n_blocks,) + HEAD_ROW, F32)
    out = pl.pallas_call(
        _moba_decode_kernel,
        out_shape=jax.ShapeDtypeStruct((batch, N_HEADS, HEAD_DIM, 1), F32),
        grid_spec=pltpu.PrefetchScalarGridSpec(
            num_scalar_prefetch=1,
            grid=(batch, n_pages // PAGES_PER_STEP),
            in_specs=[col, col, col] + pages + pages,
            out_specs=col,
            scratch_shapes=[pltpu.VMEM(PAGE_SHAPE, F32), stat, stat, stat,
                            pltpu.VMEM((n_blocks,) + PAGE_SHAPE, F32),
                            pltpu.VMEM(STEP_ROWS, F32), pltpu.VMEM(STEP_ROWS, F32)],
        ),
        compiler_params=_cparams("parallel", "arbitrary"),
        name="moba_decode",
    )(page_table, _head_columns(q), _head_columns(k_own), _head_columns(v_own),
      *([ck] * PAGES_PER_STEP), *([cv] * PAGES_PER_STEP))
    return out.reshape(batch, D_MODEL)


def kernel(x_prompt, x_sample, cache_sb_k, cache_sb_v, cache_moba_k, cache_moba_v, state_ffn_conv, page_table, p_prompt, p_sample, mix_norm, ffn_norm, ple_norm, sgu_w_in, sgu_v_norm, sgu_w_s, sgu_b_s, sgu_w_out, sb_w_qkv, sb_w_out, moba_w_qkv, moba_q_norm, moba_k_norm, moba_w_out, ffn_w_in, ffn_conv_w, ffn_conv_b, ffn_w_out, ple_w_in, ple_w_gate):
    bp, tp, _ = x_prompt.shape
    bs, ts, _ = x_sample.shape
    assert ts == 1 and tp % (2 * SGU_CHUNK) == 0 and tp % MOBA_BLOCK == 0
    depth = mix_norm.shape[0]
    past_len = page_table.shape[1] * PAGE_SIZE
    assert past_len % MOBA_BLOCK == 0 and past_len % SGU_CHUNK == 0
    xp = x_prompt.reshape(bp * tp, D_MODEL)
    xs = x_sample.reshape(bs, D_MODEL)
    bf = lambda w: w.astype(BF16)
    heads_p = lambda t: t.reshape(bp, tp, N_HEADS, HEAD_DIM)
    heads_s = lambda t: t.reshape(bs, ts, N_HEADS, HEAD_DIM)

    sb_kp, sb_vp, sb_ks, sb_vs = [], [], [], []
    mb_kp, mb_vp, mb_ks, mb_vs = [], [], [], []
    sgu_vs, conv_p, conv_s = [], [], []

    for i in range(depth):
        kind, j = i % N_MIXERS, i // N_MIXERS
        if kind == 0:
            w_in, w_out = bf(sgu_w_in[j]), bf(sgu_w_out[j])
            zs = _norm_matmul(xs, mix_norm[i], w_in, 2, act="gelu")
            xp = _sgu_prompt(xp, mix_norm[i], w_in, sgu_v_norm[j], sgu_w_s[j], sgu_b_s[j], w_out)
            xs, v_s = _sgu_sample(zs, sgu_v_norm[j], sgu_w_s[j], sgu_b_s[j], w_out, xs)
            sgu_vs.append(v_s.reshape(bs, ts, SGU_WIDTH))
        elif kind == 1:
            w_qkv, w_out = bf(sb_w_qkv[j]), bf(sb_w_out[j])
            qkv_p = _norm_matmul(xp, mix_norm[i], w_qkv, 3)
            qkv_s = _norm_matmul(xs, mix_norm[i], w_qkv, 3)
            xp = _proj_residual(_sb_prompt(qkv_p, bp, tp), w_out, xp)
            att = _sb_decode(qkv_s[0], qkv_s[1], qkv_s[2], cache_sb_k, cache_sb_v,
                             page_table, j, past_len, past_len)
            xs = _proj_residual(att, w_out, xs)
            sb_kp.append(heads_p(qkv_p[1]))
            sb_vp.append(heads_p(qkv_p[2]))
            sb_ks.append(heads_s(qkv_s[1]))
            sb_vs.append(heads_s(qkv_s[2]))
        else:
            w_qkv, w_out = bf(moba_w_qkv[j]), bf(moba_w_out[j])
            qkv_p = _norm_matmul(xp, mix_norm[i], w_qkv, 3)
            qkv_s = _norm_matmul(xs, mix_norm[i], w_qkv, 3)
            qk_p = _qknorm_rope(qkv_p, moba_q_norm[j], moba_k_norm[j], jnp.arange(tp, dtype=jnp.int32))
            qk_s = _qknorm_rope(qkv_s, moba_q_norm[j], moba_k_norm[j], jnp.full((bs,), past_len, jnp.int32))
            xp = _proj_residual(_moba_prompt(qk_p, qkv_p, bp, tp), w_out, xp)
            att = _moba_decode(qk_s[0], qk_s[1], qkv_s[2], cache_moba_k, cache_moba_v, page_table, j)
            xs = _proj_residual(att, w_out, xs)
            mb_kp.append(heads_p(qk_p[1]))
            mb_vp.append(heads_p(qkv_p[2]))
            mb_ks.append(heads_s(qk_s[1]))
            mb_vs.append(heads_s(qkv_s[2]))

        w_in, w_out = bf(ffn_w_in[i]), bf(ffn_w_out[i])
        gv_s = _norm_matmul(xs, ffn_norm[i], w_in, 2)
        xp, g_tails = _ffn_prompt(xp, ffn_norm[i], w_in, ffn_conv_w[i], ffn_conv_b[i], w_out, tp)
        xs = _ffn_sample(gv_s, state_ffn_conv[i], ffn_conv_w[i], ffn_conv_b[i], w_out, xs)
        conv_p.append(g_tails.reshape(bp, -1, SUBLANES, D_FF)[:, -1, SUBLANES - (CONV_W - 1):])
        conv_s.append(jnp.concatenate([state_ffn_conv[i][:, 1:], gv_s[0][:, None]], axis=1))

        w_gate, w_pin = bf(ple_w_gate[i]), bf(ple_w_in[i])
        xp = _ple(xp, p_prompt.reshape(depth, bp * tp, -1), i, ple_norm[i], w_gate, w_pin)
        xs = _ple(xs, p_sample.reshape(depth, bs, -1), i, ple_norm[i], w_gate, w_pin)

    return (xp.reshape(bp, tp, D_MODEL), xs.reshape(bs, ts, D_MODEL),
            jnp.stack(sb_kp, axis=1), jnp.stack(sb_vp, axis=1),
            jnp.stack(sb_ks, axis=1), jnp.stack(sb_vs, axis=1),
            jnp.stack(mb_kp, axis=1), jnp.stack(mb_vp, axis=1),
            jnp.stack(mb_ks, axis=1), jnp.stack(mb_vs, axis=1),
            jnp.stack(sgu_vs, axis=0), jnp.stack(conv_p, axis=0), jnp.stack(conv_s, axis=0))
```

```python
import functools

import jax
import jax.numpy as jnp
from jax import lax
from jax.experimental import pallas as pl
from jax.experimental.pallas import tpu as pltpu

F32 = jnp.float32
BF16 = jnp.bfloat16

D_MODEL = 1024
N_HEADS = 16
HEAD_DIM = 64
PAGE_SIZE = 128
SGU_CHUNK = 128
SGU_GROUPS = 8
SGU_WIDTH = 2 * D_MODEL
SGU_GROUP_DIM = SGU_WIDTH // SGU_GROUPS
MOBA_BLOCK = 256
MOBA_TOPK = 3
ROT_DIM = HEAD_DIM // 4
ROPE_THETA = 500000.0
D_FF = 2816
CONV_W = 3
EPS = 1e-6
N_MIXERS = 3

LANES = 128
HEADS_PER_LANE_TILE = LANES // HEAD_DIM
N_HEAD_PAIRS = D_MODEL // LANES
QK_SCALE = HEAD_DIM ** -0.5
MASK_BIAS = -1e30
EXP_UNDERFLOWS_BELOW = -104.0
VMEM_LIMIT = 48 * 1024 * 1024


def _cparams(*sem):
    return pltpu.CompilerParams(dimension_semantics=sem, vmem_limit_bytes=VMEM_LIMIT)


def _rms(x, g):
    return x * lax.rsqrt(jnp.mean(x * x, axis=-1, keepdims=True) + EPS) * g


def _dot(a, b):
    return jnp.dot(a, b, preferred_element_type=F32)


def _dot_nt(a, b):
    return lax.dot_general(a, b, (((1,), (1,)), ((), ())), preferred_element_type=F32)


def _split_bf16(x):
    hi = x.astype(BF16)
    lo = (x - hi.astype(F32)).astype(BF16)
    return jnp.concatenate([hi, lo], axis=1)


def _strict_upper_sum_matrix(n):
    j = lax.broadcasted_iota(jnp.int32, (n, n + LANES), 0)
    s = lax.broadcasted_iota(jnp.int32, (n, n + LANES), 1)
    return jnp.where((j > s) | (s >= n), 1.0, 0.0).astype(BF16)


def _log_sigmoid_pair(z):
    lsp = jnp.minimum(z, 0.0) - jnp.log(1.0 + jnp.exp(-jnp.abs(z)))
    return lsp, lsp - z


def _norm_matmul_kernel(x_ref, g_ref, w_ref, o_ref, xn_ref, *, act):
    @pl.when(pl.program_id(1) == 0)
    def _():
        xn_ref[...] = _rms(x_ref[...], g_ref[...]).astype(BF16)

    y = _dot(xn_ref[...], w_ref[...])
    if act == "gelu":
        y = jax.nn.gelu(y)
    o_ref[...] = y


def _lane_tile(n, cap):
    return max(t for t in range(LANES, cap + 1, LANES) if n % t == 0)


def _norm_matmul(x, gain, w, n_split, act=None, tm=512, tn_cap=1536):
    m, k = x.shape
    n_out = w.shape[1]
    per = n_out // n_split
    tm = min(tm, m)
    tn = _lane_tile(per, tn_cap)
    nj = per // tn
    assert m % tm == 0 and per % tn == 0
    return pl.pallas_call(
        functools.partial(_norm_matmul_kernel, act=act),
        out_shape=jax.ShapeDtypeStruct((n_split, m, per), F32),
        grid=(m // tm, n_out // tn),
        in_specs=[
            pl.BlockSpec((tm, k), lambda i, j: (i, 0)),
            pl.BlockSpec((1, k), lambda i, j: (0, 0)),
            pl.BlockSpec((k, tn), lambda i, j: (0, j)),
        ],
        out_specs=pl.BlockSpec((None, tm, tn), lambda i, j: (j // nj, i, j % nj)),
        scratch_shapes=[pltpu.VMEM((tm, k), BF16)],
        compiler_params=_cparams("parallel", "arbitrary"),
        name="norm_matmul",
    )(x, gain.reshape(1, k), w)


def _proj_residual_kernel(h_ref, w_ref, x_ref, o_ref):
    o_ref[...] = x_ref[...] + _dot(h_ref[...].astype(BF16), w_ref[...])


def _proj_residual(h, w, x, tm=1024):
    m, k = h.shape
    n = w.shape[1]
    tm = min(tm, m)
    return pl.pallas_call(
        _proj_residual_kernel,
        out_shape=jax.ShapeDtypeStruct((m, n), F32),
        grid=(m // tm,),
        in_specs=[
            pl.BlockSpec((tm, k), lambda i: (i, 0)),
            pl.BlockSpec((k, n), lambda i: (0, 0)),
            pl.BlockSpec((tm, n), lambda i: (i, 0)),
        ],
        out_specs=pl.BlockSpec((tm, n), lambda i: (i, 0)),
        compiler_params=_cparams("parallel"),
        name="proj_residual",
    )(h, w, x)


def _ple_kernel(x_ref, p_ref, g_ref, wg_ref, wp_ref, o_ref):
    x = x_ref[...]
    gate = jax.nn.sigmoid(_dot(_rms(x, g_ref[...]).astype(BF16), wg_ref[...]))
    o_ref[...] = x + gate * _dot(p_ref[...].astype(BF16), wp_ref[...])


def _ple(x, p_layers, layer, gain, w_gate, w_in, tm=512):
    m, d = x.shape
    pd = p_layers.shape[2]
    tm = min(tm, m)
    return pl.pallas_call(
        _ple_kernel,
        out_shape=jax.ShapeDtypeStruct((m, d), F32),
        grid=(m // tm,),
        in_specs=[
            pl.BlockSpec((tm, d), lambda i: (i, 0)),
            pl.BlockSpec((None, tm, pd), lambda i: (layer, i, 0)),
            pl.BlockSpec((1, d), lambda i: (0, 0)),
            pl.BlockSpec((d, d), lambda i: (0, 0)),
            pl.BlockSpec((pd, d), lambda i: (0, 0)),
        ],
        out_specs=pl.BlockSpec((tm, d), lambda i: (i, 0)),
        compiler_params=_cparams("parallel"),
        name="ple",
    )(x, p_layers, gain.reshape(1, d), w_gate, w_in)


def _sgu_prompt_kernel(x_ref, gain_ref, win_ref, vg_ref, ws_ref, bst_ref, wo_ref, o_ref, *, chunks):
    x = x_ref[...]
    xn = _rms(x, gain_ref[...]).astype(BF16)
    v = _rms(jax.nn.gelu(_dot(xn, win_ref[:, SGU_WIDTH:])), vg_ref[...]).astype(BF16)
    i = lax.broadcasted_iota(jnp.int32, (SGU_CHUNK, SGU_CHUNK), 0)
    j = lax.broadcasted_iota(jnp.int32, (SGU_CHUNK, SGU_CHUNK), 1)
    acc = x
    for g in range(SGU_GROUPS):
        cols = slice(g * SGU_GROUP_DIM, (g + 1) * SGU_GROUP_DIM)
        w = jnp.where(i >= j, ws_ref[g], 0.0).astype(BF16)
        bias = bst_ref[:, g:g + 1]
        mixed = jnp.concatenate(
            [_dot(w, v[c * SGU_CHUNK:(c + 1) * SGU_CHUNK, cols]) + bias for c in range(chunks)], axis=0)
        u = jax.nn.gelu(_dot(xn, win_ref[:, cols]))
        acc = acc + _dot((u * mixed).astype(BF16), wo_ref[cols, :])
    o_ref[...] = acc


def _sgu_prompt(x, gain, w_in, v_gain, w_s, b_s, w_out, chunks=8):
    m = x.shape[0]
    tm = chunks * SGU_CHUNK
    return pl.pallas_call(
        functools.partial(_sgu_prompt_kernel, chunks=chunks),
        out_shape=jax.ShapeDtypeStruct((m, D_MODEL), F32),
        grid=(m // tm,),
        in_specs=[
            pl.BlockSpec((tm, D_MODEL), lambda i: (i, 0)),
            _resident((1, D_MODEL)),
            _resident((D_MODEL, 2 * SGU_WIDTH)),
            _resident((1, SGU_WIDTH)),
            _resident((SGU_GROUPS, SGU_CHUNK, SGU_CHUNK)),
            _resident((SGU_CHUNK, SGU_GROUPS)),
            _resident((SGU_WIDTH, D_MODEL)),
        ],
        out_specs=pl.BlockSpec((tm, D_MODEL), lambda i: (i, 0)),
        compiler_params=_cparams("parallel"),
        name="sgu_prompt",
    )(x, gain.reshape(1, D_MODEL), w_in, v_gain.reshape(1, SGU_WIDTH), w_s, b_s.T, w_out)


def _sgu_sample_kernel(u_ref, v_ref, vg_ref, w0_ref, b0_ref, wo_ref, x_ref, o_ref, vn_ref):
    v = _rms(v_ref[...], vg_ref[...])
    vn_ref[...] = v
    gated = u_ref[...] * (w0_ref[...] * v.astype(BF16).astype(F32) + b0_ref[...])
    o_ref[...] = x_ref[...] + _dot(gated.astype(BF16), wo_ref[...])


def _sgu_sample(z2, v_gain, w_s, b_s, w_out, x):
    m = x.shape[0]
    w0 = jnp.repeat(w_s[:, 0, 0].astype(BF16).astype(F32), SGU_GROUP_DIM).reshape(1, SGU_WIDTH)
    b0 = jnp.repeat(b_s[:, 0], SGU_GROUP_DIM).reshape(1, SGU_WIDTH)
    row = lambda i: (0, 0)
    return pl.pallas_call(
        _sgu_sample_kernel,
        out_shape=(jax.ShapeDtypeStruct((m, D_MODEL), F32), jax.ShapeDtypeStruct((m, SGU_WIDTH), F32)),
        grid=(1,),
        in_specs=[
            pl.BlockSpec((None, m, SGU_WIDTH), lambda i: (0, 0, 0)),
            pl.BlockSpec((None, m, SGU_WIDTH), lambda i: (1, 0, 0)),
            pl.BlockSpec((1, SGU_WIDTH), row),
            pl.BlockSpec((1, SGU_WIDTH), row),
            pl.BlockSpec((1, SGU_WIDTH), row),
            pl.BlockSpec((SGU_WIDTH, D_MODEL), row),
            pl.BlockSpec((m, D_MODEL), row),
        ],
        out_specs=(pl.BlockSpec((m, D_MODEL), row), pl.BlockSpec((m, SGU_WIDTH), row)),
        compiler_params=_cparams("arbitrary"),
        name="sgu_sample",
    )(z2, z2, v_gain.reshape(1, SGU_WIDTH), w0, b0, w_out, x)


def _ffn_gate(c, val):
    return (jax.nn.silu(c) * val).astype(BF16)


SUBLANES = 8
FFN_CHUNK = 256


def _resident(shape):
    return pl.BlockSpec(shape, lambda i: (0,) * len(shape), pipeline_mode=pl.Buffered(1))


def _ffn_prompt_kernel(x_ref, gain_ref, win_ref, cw_ref, cb_ref, wo_ref, p_ref, pgain_ref, wg_ref, wp_ref,
                       o_ref, tail_ref, prev_ref, *, tiles_per_seq):
    x = x_ref[...]
    tm = x.shape[0]
    xn = _rms(x, gain_ref[...]).astype(BF16)
    @pl.when(pl.program_id(0) % tiles_per_seq == 0)
    def _():
        prev_ref[...] = jnp.zeros_like(prev_ref)

    row = lax.broadcasted_iota(jnp.int32, (tm, 1), 0)
    acc = x
    for c0 in range(0, D_FF, FFN_CHUNK):
        cols = slice(c0, c0 + FFN_CHUNK)
        g = _dot(xn, win_ref[:, cols])
        val = _dot(xn, win_ref[:, D_FF + c0:D_FF + c0 + FFN_CHUNK])
        prev = prev_ref[:, cols]
        g1 = jnp.where(row == 0, prev[7:8], pltpu.roll(g, 1, axis=0))
        g2 = jnp.where(row == 0, prev[6:7], jnp.where(row == 1, prev[7:8], pltpu.roll(g, 2, axis=0)))
        c = cb_ref[:, cols] + cw_ref[0:1, cols] * g2 + cw_ref[1:2, cols] * g1 + cw_ref[2:3, cols] * g
        acc = acc + _dot(_ffn_gate(c, val), wo_ref[cols, :])
        prev_ref[:, cols] = g[tm - SUBLANES:, :]
        tail_ref[:, cols] = g[tm - SUBLANES:, :]
    gate = jax.nn.sigmoid(_dot(_rms(acc, pgain_ref[...]).astype(BF16), wg_ref[...]))
    o_ref[...] = acc + gate * _dot(p_ref[...].astype(BF16), wp_ref[...])


def _ffn_prompt(x, gain, w_in, conv_w, conv_b, w_out, seq_len, p_layers, layer, ple_gain, w_gate, w_pin, tm=1024):
    m = x.shape[0]
    pd = p_layers.shape[2]
    assert seq_len % tm == 0 and D_FF % FFN_CHUNK == 0
    return pl.pallas_call(
        functools.partial(_ffn_prompt_kernel, tiles_per_seq=seq_len // tm),
        out_shape=(jax.ShapeDtypeStruct((m, D_MODEL), F32),
                   jax.ShapeDtypeStruct((m // tm, SUBLANES, D_FF), F32)),
        grid=(m // tm,),
        in_specs=[
            pl.BlockSpec((tm, D_MODEL), lambda i: (i, 0)),
            _resident((1, D_MODEL)),
            _resident((D_MODEL, 2 * D_FF)),
            _resident((CONV_W, D_FF)),
            _resident((1, D_FF)),
            _resident((D_FF, D_MODEL)),
            pl.BlockSpec((None, tm, pd), lambda i: (layer, i, 0)),
            _resident((1, D_MODEL)),
            _resident((D_MODEL, D_MODEL)),
            _resident((pd, D_MODEL)),
        ],
        out_specs=(pl.BlockSpec((tm, D_MODEL), lambda i: (i, 0)),
                   pl.BlockSpec((None, SUBLANES, D_FF), lambda i: (i, 0, 0))),
        scratch_shapes=[pltpu.VMEM((SUBLANES, D_FF), F32)],
        compiler_params=_cparams("arbitrary"),
        name="ffn_prompt",
    )(x, gain.reshape(1, D_MODEL), w_in, conv_w, conv_b.reshape(1, D_FF), w_out,
      p_layers, ple_gain.reshape(1, D_MODEL), w_gate, w_pin)


def _ffn_sample_kernel(g_ref, val_ref, s0_ref, s1_ref, cw_ref, cb_ref, wo_ref, x_ref, o_ref):
    c = cb_ref[...] + cw_ref[0:1] * s0_ref[...] + cw_ref[1:2] * s1_ref[...] + cw_ref[2:3] * g_ref[...]
    o_ref[...] = x_ref[...] + _dot(_ffn_gate(c, val_ref[...]), wo_ref[...])


def _ffn_sample(gv2, state, conv_w, conv_b, w_out, x):
    m = x.shape[0]
    full = lambda i: (0, 0)
    return pl.pallas_call(
        _ffn_sample_kernel,
        out_shape=jax.ShapeDtypeStruct((m, D_MODEL), F32),
        grid=(1,),
        in_specs=[
            pl.BlockSpec((None, m, D_FF), lambda i: (0, 0, 0)),
            pl.BlockSpec((None, m, D_FF), lambda i: (1, 0, 0)),
            pl.BlockSpec((m, D_FF), full),
            pl.BlockSpec((m, D_FF), full),
            pl.BlockSpec((CONV_W, D_FF), full),
            pl.BlockSpec((1, D_FF), full),
            pl.BlockSpec((D_FF, D_MODEL), full),
            pl.BlockSpec((m, D_MODEL), full),
        ],
        out_specs=pl.BlockSpec((m, D_MODEL), full),
        compiler_params=_cparams("arbitrary"),
        name="ffn_sample",
    )(gv2, gv2, state[:, 0], state[:, 1], conv_w, conv_b.reshape(1, D_FF), w_out, x)


def _qknorm_rope_kernel(x_ref, gain_ref, c_ref, s1_ref, s2_ref, bd_ref, o_ref):
    x = x_ref[...]
    head_sumsq = _dot(_split_bf16(x * x), bd_ref[...])
    y = x * lax.rsqrt(head_sumsq * (1.0 / HEAD_DIM) + EPS) * gain_ref[...]
    reps = D_MODEL // LANES
    tile = lambda t: jnp.concatenate([t] * reps, axis=1)
    half = ROT_DIM // 2
    o_ref[...] = (y * tile(c_ref[...])
                  + pltpu.roll(y, D_MODEL - half, axis=1) * tile(s1_ref[...])
                  + pltpu.roll(y, half, axis=1) * tile(s2_ref[...]))


def _rope_tables(pos):
    half = ROT_DIM // 2
    inv_freq = ROPE_THETA ** (-jnp.arange(0, ROT_DIM, 2, dtype=F32) / ROT_DIM)
    ang = pos.astype(F32)[:, None] * inv_freq[None, :]
    cos, sin = jnp.cos(ang), jnp.sin(ang)
    n = pos.shape[0]
    pad = jnp.zeros((n, HEAD_DIM - ROT_DIM), F32)
    zeros = jnp.zeros((n, half), F32)
    c = jnp.concatenate([cos, cos, pad + 1.0], axis=1)
    s1 = jnp.concatenate([-sin, zeros, pad], axis=1)
    s2 = jnp.concatenate([zeros, sin, pad], axis=1)
    rep = lambda t: jnp.concatenate([t] * HEADS_PER_LANE_TILE, axis=1)
    return rep(c), rep(s1), rep(s2)


def _qknorm_rope(qkv3, q_gain, k_gain, pos, tm=512):
    m = qkv3.shape[1]
    t = pos.shape[0]
    tm = min(tm, m, t)
    tiles_per_seq = t // tm
    gains = jnp.stack([jnp.tile(q_gain, N_HEADS), jnp.tile(k_gain, N_HEADS)]).reshape(2, 1, D_MODEL)
    c, s1, s2 = _rope_tables(pos)
    head = jnp.arange(D_MODEL) // HEAD_DIM
    bd = (head[:, None] == head[None, :]).astype(BF16)
    bd2 = jnp.concatenate([bd, bd], axis=0)
    tab = pl.BlockSpec((tm, LANES), lambda s, i: (i % tiles_per_seq, 0))
    return pl.pallas_call(
        _qknorm_rope_kernel,
        out_shape=jax.ShapeDtypeStruct((2, m, D_MODEL), F32),
        grid=(2, m // tm),
        in_specs=[
            pl.BlockSpec((None, tm, D_MODEL), lambda s, i: (s, i, 0)),
            pl.BlockSpec((None, 1, D_MODEL), lambda s, i: (s, 0, 0)),
            tab, tab, tab,
            pl.BlockSpec((2 * D_MODEL, D_MODEL), lambda s, i: (0, 0)),
        ],
        out_specs=pl.BlockSpec((None, tm, D_MODEL), lambda s, i: (s, i, 0)),
        compiler_params=_cparams("parallel", "parallel"),
        name="qknorm_rope",
    )(qkv3, gains, c, s1, s2, bd2)


def _transpose_kernel(x_ref, o_ref):
    o_ref[...] = x_ref[...].T


def _prompt_cache_rows(x3, sel, batch, seq_len, tm=512):
    tiles = seq_len // tm
    x_t = pl.pallas_call(
        _transpose_kernel,
        out_shape=jax.ShapeDtypeStruct((batch, D_MODEL, seq_len), F32),
        grid=(batch, tiles),
        in_specs=[pl.BlockSpec((None, tm, D_MODEL), lambda b, t: (sel, b * tiles + t, 0))],
        out_specs=pl.BlockSpec((None, D_MODEL, tm), lambda b, t: (b, 0, t)),
        compiler_params=_cparams("parallel", "parallel"),
        name="prompt_cache_rows",
    )(x3)
    return jnp.transpose(x_t.reshape(batch, 1, N_HEADS, HEAD_DIM, seq_len), (0, 1, 4, 2, 3))


def _head_lane_masks():
    lane = lax.broadcasted_iota(jnp.int32, (1, LANES), 1)
    return [(lane // HEAD_DIM) == h for h in range(HEADS_PER_LANE_TILE)]


def _stack_heads(q):
    return jnp.concatenate([jnp.where(hm, q, 0.0) for hm in _head_lane_masks()], axis=0).astype(BF16)


def _unstack_heads(acc, tq):
    return sum(jnp.where(hm, acc[h * tq:(h + 1) * tq], 0.0) for h, hm in enumerate(_head_lane_masks()))


def _sb_prompt_kernel(q_ref, k_ref, v_ref, o_ref, r_ref, acc_ref, *, tq, tk):
    qi = pl.program_id(2)
    qs = _stack_heads(q_ref[...] * QK_SCALE)
    rows = qs.shape[0]
    u = _strict_upper_sum_matrix(tk)
    q_pos = lax.broadcasted_iota(jnp.int32, (rows, tq), 0) & (tq - 1)
    k_pos = lax.broadcasted_iota(jnp.int32, (rows, tq), 1)
    r_ref[...] = jnp.zeros_like(r_ref)
    acc_ref[...] = jnp.zeros_like(acc_ref)

    def chunk(c, diagonal):
        start = pl.multiple_of(c * tq, tq)
        lsp, lk = _log_sigmoid_pair(_dot_nt(qs, k_ref[pl.ds(start, tq), :].astype(BF16)))
        if diagonal:
            causal = k_pos < q_pos
            lsp, lk = jnp.where(causal, lsp, MASK_BIAS), jnp.where(causal, lk, 0.0)
        r = r_ref[...]
        weights = []
        for j in reversed(range(tq // tk)):
            cols = slice(j * tk, (j + 1) * tk)
            sums = _dot(lk[:, cols].astype(BF16), u)
            later = sums[:, :tk] + jnp.concatenate([r] * (tk // LANES), axis=1)
            weights.append(jnp.exp(lsp[:, cols] + later).astype(BF16))
            r = r + sums[:, tk:]
        r_ref[...] = r
        a = jnp.concatenate(weights[::-1], axis=1)
        acc_ref[...] += _dot(a, v_ref[pl.ds(start, tq), :].astype(BF16))
        return jnp.max(r)

    def older(state):
        c, _ = state
        return c - 1, chunk(c, False)

    lax.while_loop(lambda state: (state[0] >= 0) & (state[1] > EXP_UNDERFLOWS_BELOW), older,
                   (qi - 1, chunk(qi, True)))
    o_ref[...] = _unstack_heads(acc_ref[...], tq)


def _attention_specs(batch, seq_len, tq, q_sel, k_sel, v_sel):
    nq = seq_len // tq
    return dict(
        grid=(batch, N_HEAD_PAIRS, nq),
        in_specs=[
            pl.BlockSpec((None, tq, LANES), lambda b, hp, qi: (q_sel, b * nq + qi, hp)),
            pl.BlockSpec((None, seq_len, LANES), lambda b, hp, qi: (k_sel, b, hp)),
            pl.BlockSpec((None, seq_len, LANES), lambda b, hp, qi: (v_sel, b, hp)),
        ],
        out_specs=pl.BlockSpec((tq, LANES), lambda b, hp, qi: (b * nq + qi, hp)),
        out_shape=jax.ShapeDtypeStruct((batch * seq_len, D_MODEL), F32),
    )


def _sb_prompt(qkv3, batch, seq_len, tq=512, tk=256):
    tq = min(tq, seq_len)
    assert tq % tk == 0 and tq & (tq - 1) == 0
    return pl.pallas_call(
        functools.partial(_sb_prompt_kernel, tq=tq, tk=tk),
        scratch_shapes=[pltpu.VMEM((HEADS_PER_LANE_TILE * tq, LANES), F32)] * 2,
        compiler_params=_cparams("parallel", "parallel", "arbitrary"),
        name="sb_prompt",
        **_attention_specs(batch, seq_len, tq, 0, 1, 2),
    )(qkv3, qkv3, qkv3)


def _moba_prompt_kernel(q_ref, k_ref, v_ref, o_ref, kmean_ref, m_ref, l_ref, acc_ref, s0_ref, s1_ref, *,
                        n_blocks, group):
    qi = pl.program_id(2)
    blk = MOBA_BLOCK

    @pl.when(qi == 0)
    def _():
        kmean_ref[...] = jnp.zeros_like(kmean_ref)
        for n in range(n_blocks):
            kmean_ref[n:n + 1, :] = jnp.mean(k_ref[n * blk:(n + 1) * blk, :], axis=0, keepdims=True)

    qs = _stack_heads(q_ref[...] * QK_SCALE)
    rows = qs.shape[0]
    neg_inf = -jnp.inf
    gate_rows = -(-n_blocks // SUBLANES) * SUBLANES
    block = lax.broadcasted_iota(jnp.int32, (gate_rows, rows), 0)
    gate = jnp.where(block < qi, _dot_nt(kmean_ref[:gate_rows, :].astype(BF16), qs), neg_inf)
    sel = jnp.zeros((gate_rows, rows), jnp.bool_)
    for r in range(MOBA_TOPK):
        best = jnp.max(gate, axis=0, keepdims=True)
        idx = jnp.min(jnp.where(gate == best, block, LANES), axis=0, keepdims=True)
        pick = block == idx
        sel = sel | (pick & (r < qi))
        gate = jnp.where(pick, neg_inf, gate)
    bias = jnp.concatenate([jnp.where(sel, 0.0, MASK_BIAS),
                            jnp.full((LANES - gate_rows, rows), MASK_BIAS, F32)], axis=0)
    q_aug = jnp.concatenate([qs, bias.T.astype(BF16)], axis=1)

    q_pos = lax.broadcasted_iota(jnp.int32, (rows, blk), 0) & (blk - 1)
    k_pos = lax.broadcasted_iota(jnp.int32, (rows, blk), 1)
    start = pl.multiple_of(qi * blk, blk)
    def fold(scores, m, first_key):
        m_new = m
        for s in scores:
            m_new = jnp.maximum(m_new, jnp.max(s, axis=-1, keepdims=True))
        shift = jnp.concatenate([m_new] * (blk // LANES), axis=1)
        pv = 0.0
        for j, s in enumerate(scores):
            v = v_ref[pl.ds(pl.multiple_of(first_key + j * blk, blk), blk), :].astype(BF16)
            pv = pv + _dot(jnp.exp(s - shift).astype(BF16), jnp.concatenate([v, jnp.ones_like(v)], axis=1))
        return m_new, pv[:, :LANES], pv[:, LANES:]

    s = jnp.where(k_pos <= q_pos, _dot_nt(qs, k_ref[pl.ds(start, blk), :].astype(BF16)), neg_inf)
    m_ref[...], acc_ref[...], l_ref[...] = fold([s], jnp.full((rows, LANES), neg_inf, F32), start)

    k_lane = lax.broadcasted_iota(jnp.int32, (blk, LANES), 1)

    def score_group(i, s_ref):
        for j in range(group):
            keys = k_ref[pl.ds(pl.multiple_of((group * i + j) * blk, blk), blk), :].astype(BF16)
            one_hot = jnp.where(k_lane == group * i + j, 1.0, 0.0).astype(BF16)
            s_ref[j] = _dot_nt(q_aug, jnp.concatenate([keys, one_hot], axis=1))

    def fold_group(i, s_ref):
        m = m_ref[...]
        m_new, pv, p_sum = fold([s_ref[j] for j in range(group)], m, i * (group * blk))
        alpha = jnp.exp(m - m_new)
        m_ref[...] = m_new
        l_ref[...] = alpha * l_ref[...] + p_sum
        acc_ref[...] = alpha * acc_ref[...] + pv

    last_group = n_blocks // group - 1

    def group_pair(p, carry):
        score_group(2 * p + 1, s1_ref)
        fold_group(2 * p, s0_ref)
        score_group(jnp.minimum(2 * p + 2, last_group), s0_ref)
        fold_group(2 * p + 1, s1_ref)
        return carry

    score_group(0, s0_ref)
    lax.fori_loop(0, (qi + 2 * group - 1) // (2 * group), group_pair, 0)
    o_ref[...] = _unstack_heads(acc_ref[...] / l_ref[...], blk)


def _moba_prompt(qk2, qkv3, batch, seq_len, group=2):
    n_blocks = seq_len // MOBA_BLOCK
    assert n_blocks <= LANES and n_blocks % (2 * group) == 0
    specs = _attention_specs(batch, seq_len, MOBA_BLOCK, 0, 1, 2)
    rows = HEADS_PER_LANE_TILE * MOBA_BLOCK
    return pl.pallas_call(
        functools.partial(_moba_prompt_kernel, n_blocks=n_blocks, group=group),
        scratch_shapes=[pltpu.VMEM((LANES, LANES), F32)] + [pltpu.VMEM((rows, LANES), F32)] * 3
                       + [pltpu.VMEM((group, rows, MOBA_BLOCK), F32)] * 2,
        compiler_params=_cparams("parallel", "arbitrary", "arbitrary"),
        name="moba_prompt",
        **specs,
    )(qk2, qk2, qkv3)


PAGES_PER_STEP = 4
PAGES_PER_BLOCK = MOBA_BLOCK // PAGE_SIZE


PAGE_SHAPE = (N_HEADS, HEAD_DIM, PAGE_SIZE)
HEAD_ROW = (N_HEADS, PAGE_SIZE)
STEP_ROWS = (PAGES_PER_STEP * N_HEADS, PAGE_SIZE)


def _page_view(cache):
    return jnp.transpose(cache, (0, 1, 3, 4, 2))


def _head_columns(x):
    return x.reshape(x.shape[0], N_HEADS, HEAD_DIM, 1)


def _score_pages(z_ref, qb_ref, k_refs):
    for j, k_ref in enumerate(k_refs):
        for h in range(N_HEADS):
            row = j * N_HEADS + h
            z_ref[row:row + 1, :] = jnp.sum(k_ref[h] * qb_ref[h], axis=0, keepdims=True)


def _weigh_values(w_ref, v_refs, pages, h):
    return sum(w_ref[j * N_HEADS + h:j * N_HEADS + h + 1, :] * v_refs[j][h] for j in pages)


def _page_rows(x, j):
    return x[j * N_HEADS:(j + 1) * N_HEADS]


def _over_dims(x):
    return x[:, None, :]


def _lane_sum(x):
    return jnp.sum(x, axis=-1, keepdims=True)


def _suffix_sum_exclusive(x):
    lane = lax.broadcasted_iota(jnp.int32, x.shape, x.ndim - 1)
    inc = x
    d = 1
    while d < LANES:
        inc = inc + jnp.where(lane + d < LANES, pltpu.roll(inc, LANES - d, axis=x.ndim - 1), 0.0)
        d *= 2
    return inc - x


def _sb_decode_kernel(pt_ref, pos_ref, live_ref, q_ref, kown_ref, vown_ref, r_in_ref, part_in_ref, *refs,
                      newest):
    k_refs, v_refs = refs[:PAGES_PER_STEP], refs[PAGES_PER_STEP:2 * PAGES_PER_STEP]
    o_ref, r_out_ref, qb_ref, r_ref, acc_ref, z_ref, w_ref = refs[2 * PAGES_PER_STEP:]
    n = pl.program_id(1)

    @pl.when(n == 0)
    def _():
        q = q_ref[...] * QK_SCALE
        qb_ref[...] = jnp.broadcast_to(q, PAGE_SHAPE)
        if newest:
            valid = jnp.where(pos_ref[1] < pos_ref[0], 1.0, 0.0)
            lsp, lk = _log_sigmoid_pair(jnp.sum(q * kown_ref[...], axis=1))
            r_ref[...] = jnp.broadcast_to(lk * valid, HEAD_ROW)
            lane = lax.broadcasted_iota(jnp.int32, PAGE_SHAPE, 2)
            acc_ref[...] = jnp.where(lane == 0, _over_dims(jnp.exp(lsp) * valid) * vown_ref[...], 0.0)
        else:
            r_ref[...] = r_in_ref[...]
            acc_ref[...] = jnp.zeros_like(acc_ref)

    @pl.when(live_ref[pl.program_id(0)] != 0)
    def _():
        _score_pages(z_ref, qb_ref, k_refs)
        lsp, lk = _log_sigmoid_pair(z_ref[...])
        totals = _lane_sum(lk)
        r = r_ref[:, 0:1]
        later = [None] * PAGES_PER_STEP
        for j in reversed(range(PAGES_PER_STEP)):
            later[j] = r
            r = r + _page_rows(totals, j)
        r_ref[...] = jnp.broadcast_to(r, HEAD_ROW)
        w_ref[...] = jnp.exp(lsp + _suffix_sum_exclusive(lk) + jnp.concatenate(later, axis=0))
        for h in range(N_HEADS):
            acc_ref[h] += _weigh_values(w_ref, v_refs, range(PAGES_PER_STEP), h)

    @pl.when(n == pl.num_programs(1) - 1)
    def _():
        o_ref[...] = part_in_ref[...] + _lane_sum(acc_ref[...])
        r_out_ref[...] = r_ref[...]


def _decode_specs(n_steps, layer, newest_first):
    def page(j):
        def index_map(b, n, pt, *_):
            step = (n_steps - 1 - n) if newest_first else n
            return (pt[b, PAGES_PER_STEP * step + j], layer, 0, 0, 0)
        return pl.BlockSpec((None, None) + PAGE_SHAPE, index_map)

    col = pl.BlockSpec((None, N_HEADS, HEAD_DIM, 1), lambda b, n, *_: (b, 0, 0, 0))
    return col, [page(j) for j in range(PAGES_PER_STEP)]


def _sb_decode(q, k_own, v_own, cache_k, cache_v, page_table, layer, q_pos, k_own_pos):
    batch = q.shape[0]
    n_steps = page_table.shape[1] // PAGES_PER_STEP
    pos = jnp.array([q_pos, k_own_pos], jnp.int32)
    ck, cv = _page_view(cache_k), _page_view(cache_v)
    col = pl.BlockSpec((None, N_HEADS, HEAD_DIM, 1), lambda b, n, *_: (b, 0, 0, 0))
    row = pl.BlockSpec((None,) + HEAD_ROW, lambda b, n, *_: (b, 0, 0))
    cols = [_head_columns(t) for t in (q, k_own, v_own)]

    def walk(first_step, steps, live, r_in, part_in, newest):
        def page(j):
            def index_map(b, n, pt, pos, live):
                wanted = pt[b, PAGES_PER_STEP * (first_step - n) + j]
                return (jnp.where(live[b] != 0, wanted, pt[0, j]), layer, 0, 0, 0)
            return pl.BlockSpec((None, None) + PAGE_SHAPE, index_map)

        pages = [page(j) for j in range(PAGES_PER_STEP)]
        return pl.pallas_call(
            functools.partial(_sb_decode_kernel, newest=newest),
            out_shape=(jax.ShapeDtypeStruct((batch, N_HEADS, HEAD_DIM, 1), F32),
                       jax.ShapeDtypeStruct((batch,) + HEAD_ROW, F32)),
            grid_spec=pltpu.PrefetchScalarGridSpec(
                num_scalar_prefetch=3,
                grid=(batch, steps),
                in_specs=[col, col, col, row, col] + pages + pages,
                out_specs=(col, row),
                scratch_shapes=[pltpu.VMEM(PAGE_SHAPE, F32), pltpu.VMEM(HEAD_ROW, F32), pltpu.VMEM(PAGE_SHAPE, F32),
                                pltpu.VMEM(STEP_ROWS, F32), pltpu.VMEM(STEP_ROWS, F32)],
            ),
            compiler_params=_cparams("parallel", "arbitrary"),
            name="sb_decode",
        )(page_table, pos, live, *cols, r_in, part_in, *([ck] * PAGES_PER_STEP), *([cv] * PAGES_PER_STEP))

    zeros_r = jnp.zeros((batch,) + HEAD_ROW, F32)
    zeros_part = jnp.zeros((batch, N_HEADS, HEAD_DIM, 1), F32)
    part, r = walk(n_steps - 1, 1, jnp.ones((batch,), jnp.int32), zeros_r, zeros_part, True)
    if n_steps > 1:
        live = (jnp.max(r, axis=(1, 2)) > EXP_UNDERFLOWS_BELOW).astype(jnp.int32)
        part = lax.cond(jnp.any(live != 0),
                        lambda: walk(n_steps - 2, n_steps - 1, live, r, part, False)[0],
                        lambda: part)
    return part.reshape(batch, D_MODEL)


def _moba_decode_kernel(pt_ref, q_ref, kown_ref, vown_ref, *refs):
    k_refs, v_refs = refs[:PAGES_PER_STEP], refs[PAGES_PER_STEP:2 * PAGES_PER_STEP]
    o_ref, qb_ref, gate_ref, m_ref, l_ref, pv_ref, z_ref, w_ref = refs[2 * PAGES_PER_STEP:]
    n = pl.program_id(1)
    n_blocks = gate_ref.shape[0]

    @pl.when(n == 0)
    def _():
        qb_ref[...] = jnp.broadcast_to(q_ref[...] * QK_SCALE, PAGE_SHAPE)

    _score_pages(z_ref, qb_ref, k_refs)
    s = z_ref[...]
    page_max = jnp.max(s, axis=-1, keepdims=True)
    page_sum = _lane_sum(s)
    blocks = [range(b * PAGES_PER_BLOCK, (b + 1) * PAGES_PER_BLOCK) for b in range(PAGES_PER_STEP // PAGES_PER_BLOCK)]
    over_block = lambda x, pages, op: functools.reduce(op, [_page_rows(x, j) for j in pages])
    block_max = [over_block(page_max, pages, jnp.maximum) for pages in blocks]
    w_ref[...] = jnp.exp(s - jnp.concatenate([block_max[j // PAGES_PER_BLOCK] for j in range(PAGES_PER_STEP)], axis=0))
    weight_sum = _lane_sum(w_ref[...])
    for b, pages in enumerate(blocks):
        idx = n * len(blocks) + b
        gate_ref[idx] = jnp.broadcast_to(over_block(page_sum, pages, jnp.add) * (1.0 / MOBA_BLOCK), HEAD_ROW)
        m_ref[idx] = jnp.broadcast_to(block_max[b], HEAD_ROW)
        l_ref[idx] = jnp.broadcast_to(over_block(weight_sum, pages, jnp.add), HEAD_ROW)
        for h in range(N_HEADS):
            pv_ref[idx, h] = _weigh_values(w_ref, v_refs, pages, h)

    @pl.when(n == pl.num_programs(1) - 1)
    def _():
        shape = gate_ref.shape
        blk = lax.broadcasted_iota(jnp.int32, shape, 0)
        gate = gate_ref[...]
        sel = jnp.zeros(shape, jnp.bool_)
        for r in range(MOBA_TOPK):
            best = jnp.max(gate, axis=0, keepdims=True)
            idx = jnp.min(jnp.where(gate == best, blk, shape[0]), axis=0, keepdims=True)
            pick = blk == idx
            sel = sel | (pick & (r < n_blocks))
            gate = jnp.where(pick, -jnp.inf, gate)
        q = q_ref[...] * QK_SCALE
        s_own = jnp.broadcast_to(jnp.sum(q * kown_ref[...], axis=1), HEAD_ROW)
        m_all = m_ref[...]
        m_fin = jnp.maximum(jnp.max(jnp.where(sel, m_all, -jnp.inf), axis=0), s_own)
        w = jnp.where(sel, jnp.exp(m_all - m_fin[None]), 0.0)
        w_own = jnp.exp(s_own - m_fin)
        denom = jnp.sum(w * l_ref[...], axis=0) + w_own
        gate_ref[...] = w
        numer = lax.fori_loop(0, n_blocks, lambda i, acc: acc + _over_dims(gate_ref[i]) * pv_ref[i],
                              jnp.zeros(PAGE_SHAPE, F32))
        first_lane = lambda t: _over_dims(t)[:, :, 0:1]
        o_ref[...] = (_lane_sum(numer) + first_lane(w_own) * vown_ref[...]) / first_lane(denom)


def _moba_decode(q, k_own, v_own, cache_k, cache_v, page_table, layer):
    batch = q.shape[0]
    n_pages = page_table.shape[1]
    assert PAGES_PER_STEP % PAGES_PER_BLOCK == 0 and n_pages % PAGES_PER_STEP == 0
    n_blocks = n_pages // PAGES_PER_BLOCK
    col, pages = _decode_specs(n_pages // PAGES_PER_STEP, layer, False)
    ck, cv = _page_view(cache_k), _page_view(cache_v)
    stat = pltpu.VMEM((n_blocks,) + HEAD_ROW, F32)
    out = pl.pallas_call(
        _moba_decode_kernel,
        out_shape=jax.ShapeDtypeStruct((batch, N_HEADS, HEAD_DIM, 1), F32),
        grid_spec=pltpu.PrefetchScalarGridSpec(
            num_scalar_prefetch=1,
            grid=(batch, n_pages // PAGES_PER_STEP),
            in_specs=[col, col, col] + pages + pages,
            out_specs=col,
            scratch_shapes=[pltpu.VMEM(PAGE_SHAPE, F32), stat, stat, stat,
                            pltpu.VMEM((n_blocks,) + PAGE_SHAPE, F32),
                            pltpu.VMEM(STEP_ROWS, F32), pltpu.VMEM(STEP_ROWS, F32)],
        ),
        compiler_params=_cparams("parallel", "arbitrary"),
        name="moba_decode",
    )(page_table, _head_columns(q), _head_columns(k_own), _head_columns(v_own),
      *([ck] * PAGES_PER_STEP), *([cv] * PAGES_PER_STEP))
    return out.reshape(batch, D_MODEL)


def kernel(x_prompt, x_sample, cache_sb_k, cache_sb_v, cache_moba_k, cache_moba_v, state_ffn_conv, page_table, p_prompt, p_sample, mix_norm, ffn_norm, ple_norm, sgu_w_in, sgu_v_norm, sgu_w_s, sgu_b_s, sgu_w_out, sb_w_qkv, sb_w_out, moba_w_qkv, moba_q_norm, moba_k_norm, moba_w_out, ffn_w_in, ffn_conv_w, ffn_conv_b, ffn_w_out, ple_w_in, ple_w_gate):
    bp, tp, _ = x_prompt.shape
    bs, ts, _ = x_sample.shape
    assert ts == 1 and tp % (2 * SGU_CHUNK) == 0 and tp % MOBA_BLOCK == 0
    depth = mix_norm.shape[0]
    past_len = page_table.shape[1] * PAGE_SIZE
    assert past_len % MOBA_BLOCK == 0 and past_len % SGU_CHUNK == 0
    xp = x_prompt.reshape(bp * tp, D_MODEL)
    xs = x_sample.reshape(bs, D_MODEL)
    bf = lambda w: w.astype(BF16)
    heads_s = lambda t: t.reshape(bs, ts, N_HEADS, HEAD_DIM)

    sb_kp, sb_vp, sb_ks, sb_vs = [], [], [], []
    mb_kp, mb_vp, mb_ks, mb_vs = [], [], [], []
    sgu_vs, conv_p, conv_s = [], [], []

    for i in range(depth):
        kind, j = i % N_MIXERS, i // N_MIXERS
        if kind == 0:
            w_in, w_out = bf(sgu_w_in[j]), bf(sgu_w_out[j])
            zs = _norm_matmul(xs, mix_norm[i], w_in, 2, act="gelu")
            xp = _sgu_prompt(xp, mix_norm[i], w_in, sgu_v_norm[j], sgu_w_s[j], sgu_b_s[j], w_out)
            xs, v_s = _sgu_sample(zs, sgu_v_norm[j], sgu_w_s[j], sgu_b_s[j], w_out, xs)
            sgu_vs.append(v_s.reshape(bs, ts, SGU_WIDTH))
        elif kind == 1:
            w_qkv, w_out = bf(sb_w_qkv[j]), bf(sb_w_out[j])
            qkv_p = _norm_matmul(xp, mix_norm[i], w_qkv, 3)
            qkv_s = _norm_matmul(xs, mix_norm[i], w_qkv, 3)
            xp = _proj_residual(_sb_prompt(qkv_p, bp, tp), w_out, xp)
            att = _sb_decode(qkv_s[0], qkv_s[1], qkv_s[2], cache_sb_k, cache_sb_v,
                             page_table, j, past_len, past_len)
            xs = _proj_residual(att, w_out, xs)
            sb_kp.append(_prompt_cache_rows(qkv_p, 1, bp, tp))
            sb_vp.append(_prompt_cache_rows(qkv_p, 2, bp, tp))
            sb_ks.append(heads_s(qkv_s[1]))
            sb_vs.append(heads_s(qkv_s[2]))
        else:
            w_qkv, w_out = bf(moba_w_qkv[j]), bf(moba_w_out[j])
            qkv_p = _norm_matmul(xp, mix_norm[i], w_qkv, 3)
            qkv_s = _norm_matmul(xs, mix_norm[i], w_qkv, 3)
            qk_p = _qknorm_rope(qkv_p, moba_q_norm[j], moba_k_norm[j], jnp.arange(tp, dtype=jnp.int32))
            qk_s = _qknorm_rope(qkv_s, moba_q_norm[j], moba_k_norm[j], jnp.full((bs,), past_len, jnp.int32))
            xp = _proj_residual(_moba_prompt(qk_p, qkv_p, bp, tp), w_out, xp)
            att = _moba_decode(qk_s[0], qk_s[1], qkv_s[2], cache_moba_k, cache_moba_v, page_table, j)
            xs = _proj_residual(att, w_out, xs)
            mb_kp.append(_prompt_cache_rows(qk_p, 1, bp, tp))
            mb_vp.append(_prompt_cache_rows(qkv_p, 2, bp, tp))
            mb_ks.append(heads_s(qk_s[1]))
            mb_vs.append(heads_s(qkv_s[2]))

        w_in, w_out = bf(ffn_w_in[i]), bf(ffn_w_out[i])
        w_gate, w_pin = bf(ple_w_gate[i]), bf(ple_w_in[i])
        gv_s = _norm_matmul(xs, ffn_norm[i], w_in, 2)
        xp, g_tails = _ffn_prompt(xp, ffn_norm[i], w_in, ffn_conv_w[i], ffn_conv_b[i], w_out, tp,
                                  p_prompt.reshape(depth, bp * tp, -1), i, ple_norm[i], w_gate, w_pin)
        xs = _ffn_sample(gv_s, state_ffn_conv[i], ffn_conv_w[i], ffn_conv_b[i], w_out, xs)
        conv_p.append(g_tails.reshape(bp, -1, SUBLANES, D_FF)[:, -1, SUBLANES - (CONV_W - 1):])
        conv_s.append(jnp.concatenate([state_ffn_conv[i][:, 1:], gv_s[0][:, None]], axis=1))
        xs = _ple(xs, p_sample.reshape(depth, bs, -1), i, ple_norm[i], w_gate, w_pin)

    return (xp.reshape(bp, tp, D_MODEL), xs.reshape(bs, ts, D_MODEL),
            jnp.concatenate(sb_kp, axis=1), jnp.concatenate(sb_vp, axis=1),
            jnp.stack(sb_ks, axis=1), jnp.stack(sb_vs, axis=1),
            jnp.concatenate(mb_kp, axis=1), jnp.concatenate(mb_vp, axis=1),
            jnp.stack(mb_ks, axis=1), jnp.stack(mb_vs, axis=1),
            jnp.stack(sgu_vs, axis=0), jnp.stack(conv_p, axis=0), jnp.stack(conv_s, axis=0))
```

```python
import functools

import jax
import jax.numpy as jnp
from jax import lax
from jax.experimental import pallas as pl
from jax.experimental.pallas import tpu as pltpu

F32 = jnp.float32
BF16 = jnp.bfloat16

D_MODEL = 1024
N_HEADS = 16
HEAD_DIM = 64
PAGE_SIZE = 128
SGU_CHUNK = 128
SGU_GROUPS = 8
SGU_WIDTH = 2 * D_MODEL
SGU_GROUP_DIM = SGU_WIDTH // SGU_GROUPS
MOBA_BLOCK = 256
MOBA_TOPK = 3
ROT_DIM = HEAD_DIM // 4
ROPE_THETA = 500000.0
D_FF = 2816
CONV_W = 3
EPS = 1e-6
N_MIXERS = 3

LANES = 128
HEADS_PER_LANE_TILE = LANES // HEAD_DIM
N_HEAD_PAIRS = D_MODEL // LANES
QK_SCALE = HEAD_DIM ** -0.5
MASK_BIAS = -1e30
EXP_UNDERFLOWS_BELOW = -104.0
VMEM_LIMIT = 48 * 1024 * 1024


def _cparams(*sem):
    return pltpu.CompilerParams(dimension_semantics=sem, vmem_limit_bytes=VMEM_LIMIT)


def _rms(x, g):
    return x * lax.rsqrt(jnp.mean(x * x, axis=-1, keepdims=True) + EPS) * g


def _dot(a, b):
    return jnp.dot(a, b, preferred_element_type=F32)


def _dot_nt(a, b):
    return lax.dot_general(a, b, (((1,), (1,)), ((), ())), preferred_element_type=F32)


def _split_bf16(x):
    hi = x.astype(BF16)
    lo = (x - hi.astype(F32)).astype(BF16)
    return jnp.concatenate([hi, lo], axis=1)


def _strict_upper_sum_matrix(n):
    j = lax.broadcasted_iota(jnp.int32, (n, n + LANES), 0)
    s = lax.broadcasted_iota(jnp.int32, (n, n + LANES), 1)
    return jnp.where((j > s) | (s >= n), 1.0, 0.0).astype(BF16)


def _log_sigmoid_pair(z):
    lsp = jnp.minimum(z, 0.0) - jnp.log(1.0 + jnp.exp(-jnp.abs(z)))
    return lsp, lsp - z


def _norm_matmul_kernel(x_ref, g_ref, w_ref, o_ref, xn_ref, *, act):
    @pl.when(pl.program_id(1) == 0)
    def _():
        xn_ref[...] = _rms(x_ref[...], g_ref[...]).astype(BF16)

    y = _dot(xn_ref[...], w_ref[...])
    if act == "gelu":
        y = jax.nn.gelu(y)
    o_ref[...] = y


def _lane_tile(n, cap):
    return max(t for t in range(LANES, cap + 1, LANES) if n % t == 0)


def _norm_matmul(x, gain, w, n_split, act=None, tm=512, tn_cap=1536):
    m, k = x.shape
    n_out = w.shape[1]
    per = n_out // n_split
    tm = min(tm, m)
    tn = _lane_tile(per, tn_cap)
    nj = per // tn
    assert m % tm == 0 and per % tn == 0
    return pl.pallas_call(
        functools.partial(_norm_matmul_kernel, act=act),
        out_shape=jax.ShapeDtypeStruct((n_split, m, per), F32),
        grid=(m // tm, n_out // tn),
        in_specs=[
            pl.BlockSpec((tm, k), lambda i, j: (i, 0)),
            pl.BlockSpec((1, k), lambda i, j: (0, 0)),
            pl.BlockSpec((k, tn), lambda i, j: (0, j)),
        ],
        out_specs=pl.BlockSpec((None, tm, tn), lambda i, j: (j // nj, i, j % nj)),
        scratch_shapes=[pltpu.VMEM((tm, k), BF16)],
        compiler_params=_cparams("parallel", "arbitrary"),
        name="norm_matmul",
    )(x, gain.reshape(1, k), w)


def _any_of(j, values):
    return functools.reduce(jnp.logical_or, [j == v for v in values])


def _qkv_prompt_kernel(x_ref, g_ref, w_ref, o_ref, ot_ref, xn_ref, *, transposed):
    j = pl.program_id(1)

    @pl.when(j == 0)
    def _():
        xn_ref[...] = _rms(x_ref[...], g_ref[...]).astype(BF16)

    y = _dot(xn_ref[...], w_ref[...])
    o_ref[...] = y

    @pl.when(_any_of(j, transposed))
    def _():
        ot_ref[...] = y.T


def _qkv_prompt(x, gain, w, batch, seq_len, transposed, tm=512):
    m, k = x.shape
    tiles = seq_len // tm
    slot = lambda j: jnp.minimum(sum((j > s).astype(jnp.int32) for s in transposed), len(transposed) - 1)
    return pl.pallas_call(
        functools.partial(_qkv_prompt_kernel, transposed=transposed),
        out_shape=(jax.ShapeDtypeStruct((3, m, D_MODEL), F32),
                   jax.ShapeDtypeStruct((len(transposed), batch, D_MODEL, seq_len), F32)),
        grid=(m // tm, 3),
        in_specs=[
            pl.BlockSpec((tm, k), lambda i, j: (i, 0)),
            pl.BlockSpec((1, k), lambda i, j: (0, 0)),
            pl.BlockSpec((k, D_MODEL), lambda i, j: (0, j)),
        ],
        out_specs=(pl.BlockSpec((None, tm, D_MODEL), lambda i, j: (j, i, 0)),
                   pl.BlockSpec((None, None, D_MODEL, tm), lambda i, j: (slot(j), i // tiles, 0, i % tiles))),
        scratch_shapes=[pltpu.VMEM((tm, k), BF16)],
        compiler_params=_cparams("parallel", "arbitrary"),
        name="qkv_prompt",
    )(x, gain.reshape(1, k), w)


def _proj_residual_kernel(h_ref, w_ref, x_ref, o_ref):
    o_ref[...] = x_ref[...] + _dot(h_ref[...].astype(BF16), w_ref[...])


def _proj_residual(h, w, x, tm=1024):
    m, k = h.shape
    n = w.shape[1]
    tm = min(tm, m)
    return pl.pallas_call(
        _proj_residual_kernel,
        out_shape=jax.ShapeDtypeStruct((m, n), F32),
        grid=(m // tm,),
        in_specs=[
            pl.BlockSpec((tm, k), lambda i: (i, 0)),
            pl.BlockSpec((k, n), lambda i: (0, 0)),
            pl.BlockSpec((tm, n), lambda i: (i, 0)),
        ],
        out_specs=pl.BlockSpec((tm, n), lambda i: (i, 0)),
        compiler_params=_cparams("parallel"),
        name="proj_residual",
    )(h, w, x)


def _ple_kernel(x_ref, p_ref, g_ref, wg_ref, wp_ref, o_ref):
    x = x_ref[...]
    gate = jax.nn.sigmoid(_dot(_rms(x, g_ref[...]).astype(BF16), wg_ref[...]))
    o_ref[...] = x + gate * _dot(p_ref[...].astype(BF16), wp_ref[...])


def _ple(x, p_layers, layer, gain, w_gate, w_in, tm=512):
    m, d = x.shape
    pd = p_layers.shape[2]
    tm = min(tm, m)
    return pl.pallas_call(
        _ple_kernel,
        out_shape=jax.ShapeDtypeStruct((m, d), F32),
        grid=(m // tm,),
        in_specs=[
            pl.BlockSpec((tm, d), lambda i: (i, 0)),
            pl.BlockSpec((None, tm, pd), lambda i: (layer, i, 0)),
            pl.BlockSpec((1, d), lambda i: (0, 0)),
            pl.BlockSpec((d, d), lambda i: (0, 0)),
            pl.BlockSpec((pd, d), lambda i: (0, 0)),
        ],
        out_specs=pl.BlockSpec((tm, d), lambda i: (i, 0)),
        compiler_params=_cparams("parallel"),
        name="ple",
    )(x, p_layers, gain.reshape(1, d), w_gate, w_in)


def _sgu_prompt_kernel(x_ref, gain_ref, win_ref, vg_ref, ws_ref, bst_ref, wo_ref, o_ref, *, chunks):
    x = x_ref[...]
    xn = _rms(x, gain_ref[...]).astype(BF16)
    v = _rms(jax.nn.gelu(_dot(xn, win_ref[:, SGU_WIDTH:])), vg_ref[...]).astype(BF16)
    i = lax.broadcasted_iota(jnp.int32, (SGU_CHUNK, SGU_CHUNK), 0)
    j = lax.broadcasted_iota(jnp.int32, (SGU_CHUNK, SGU_CHUNK), 1)
    acc = x
    for g in range(SGU_GROUPS):
        cols = slice(g * SGU_GROUP_DIM, (g + 1) * SGU_GROUP_DIM)
        w = jnp.where(i >= j, ws_ref[g], 0.0).astype(BF16)
        bias = bst_ref[:, g:g + 1]
        mixed = jnp.concatenate(
            [_dot(w, v[c * SGU_CHUNK:(c + 1) * SGU_CHUNK, cols]) + bias for c in range(chunks)], axis=0)
        u = jax.nn.gelu(_dot(xn, win_ref[:, cols]))
        acc = acc + _dot((u * mixed).astype(BF16), wo_ref[cols, :])
    o_ref[...] = acc


def _sgu_prompt(x, gain, w_in, v_gain, w_s, b_s, w_out, chunks=8):
    m = x.shape[0]
    tm = chunks * SGU_CHUNK
    return pl.pallas_call(
        functools.partial(_sgu_prompt_kernel, chunks=chunks),
        out_shape=jax.ShapeDtypeStruct((m, D_MODEL), F32),
        grid=(m // tm,),
        in_specs=[
            pl.BlockSpec((tm, D_MODEL), lambda i: (i, 0)),
            _resident((1, D_MODEL)),
            _resident((D_MODEL, 2 * SGU_WIDTH)),
            _resident((1, SGU_WIDTH)),
            _resident((SGU_GROUPS, SGU_CHUNK, SGU_CHUNK)),
            _resident((SGU_CHUNK, SGU_GROUPS)),
            _resident((SGU_WIDTH, D_MODEL)),
        ],
        out_specs=pl.BlockSpec((tm, D_MODEL), lambda i: (i, 0)),
        compiler_params=_cparams("parallel"),
        name="sgu_prompt",
    )(x, gain.reshape(1, D_MODEL), w_in, v_gain.reshape(1, SGU_WIDTH), w_s, b_s.T, w_out)


def _sgu_sample_kernel(u_ref, v_ref, vg_ref, w0_ref, b0_ref, wo_ref, x_ref, o_ref, vn_ref):
    v = _rms(v_ref[...], vg_ref[...])
    vn_ref[...] = v
    gated = u_ref[...] * (w0_ref[...] * v.astype(BF16).astype(F32) + b0_ref[...])
    o_ref[...] = x_ref[...] + _dot(gated.astype(BF16), wo_ref[...])


def _sgu_sample(z2, v_gain, w_s, b_s, w_out, x):
    m = x.shape[0]
    w0 = jnp.repeat(w_s[:, 0, 0].astype(BF16).astype(F32), SGU_GROUP_DIM).reshape(1, SGU_WIDTH)
    b0 = jnp.repeat(b_s[:, 0], SGU_GROUP_DIM).reshape(1, SGU_WIDTH)
    row = lambda i: (0, 0)
    return pl.pallas_call(
        _sgu_sample_kernel,
        out_shape=(jax.ShapeDtypeStruct((m, D_MODEL), F32), jax.ShapeDtypeStruct((m, SGU_WIDTH), F32)),
        grid=(1,),
        in_specs=[
            pl.BlockSpec((None, m, SGU_WIDTH), lambda i: (0, 0, 0)),
            pl.BlockSpec((None, m, SGU_WIDTH), lambda i: (1, 0, 0)),
            pl.BlockSpec((1, SGU_WIDTH), row),
            pl.BlockSpec((1, SGU_WIDTH), row),
            pl.BlockSpec((1, SGU_WIDTH), row),
            pl.BlockSpec((SGU_WIDTH, D_MODEL), row),
            pl.BlockSpec((m, D_MODEL), row),
        ],
        out_specs=(pl.BlockSpec((m, D_MODEL), row), pl.BlockSpec((m, SGU_WIDTH), row)),
        compiler_params=_cparams("arbitrary"),
        name="sgu_sample",
    )(z2, z2, v_gain.reshape(1, SGU_WIDTH), w0, b0, w_out, x)


def _ffn_gate(c, val):
    return (jax.nn.silu(c) * val).astype(BF16)


SUBLANES = 8
FFN_CHUNK = 256


def _resident(shape):
    return pl.BlockSpec(shape, lambda i: (0,) * len(shape), pipeline_mode=pl.Buffered(1))


def _ffn_prompt_kernel(x_ref, gain_ref, win_ref, cw_ref, cb_ref, wo_ref, p_ref, pgain_ref, wg_ref, wp_ref,
                       o_ref, tail_ref, prev_ref, *, tiles_per_seq):
    x = x_ref[...]
    tm = x.shape[0]
    xn = _rms(x, gain_ref[...]).astype(BF16)
    @pl.when(pl.program_id(0) % tiles_per_seq == 0)
    def _():
        prev_ref[...] = jnp.zeros_like(prev_ref)

    row = lax.broadcasted_iota(jnp.int32, (tm, 1), 0)
    acc = x
    for c0 in range(0, D_FF, FFN_CHUNK):
        cols = slice(c0, c0 + FFN_CHUNK)
        g = _dot(xn, win_ref[:, cols])
        val = _dot(xn, win_ref[:, D_FF + c0:D_FF + c0 + FFN_CHUNK])
        prev = prev_ref[:, cols]
        g1 = jnp.where(row == 0, prev[7:8], pltpu.roll(g, 1, axis=0))
        g2 = jnp.where(row == 0, prev[6:7], jnp.where(row == 1, prev[7:8], pltpu.roll(g, 2, axis=0)))
        c = cb_ref[:, cols] + cw_ref[0:1, cols] * g2 + cw_ref[1:2, cols] * g1 + cw_ref[2:3, cols] * g
        acc = acc + _dot(_ffn_gate(c, val), wo_ref[cols, :])
        prev_ref[:, cols] = g[tm - SUBLANES:, :]
        tail_ref[:, cols] = g[tm - SUBLANES:, :]
    gate = jax.nn.sigmoid(_dot(_rms(acc, pgain_ref[...]).astype(BF16), wg_ref[...]))
    o_ref[...] = acc + gate * _dot(p_ref[...].astype(BF16), wp_ref[...])


def _ffn_prompt(x, gain, w_in, conv_w, conv_b, w_out, seq_len, p_layers, layer, ple_gain, w_gate, w_pin, tm=1024):
    m = x.shape[0]
    pd = p_layers.shape[2]
    assert seq_len % tm == 0 and D_FF % FFN_CHUNK == 0
    return pl.pallas_call(
        functools.partial(_ffn_prompt_kernel, tiles_per_seq=seq_len // tm),
        out_shape=(jax.ShapeDtypeStruct((m, D_MODEL), F32),
                   jax.ShapeDtypeStruct((m // tm, SUBLANES, D_FF), F32)),
        grid=(m // tm,),
        in_specs=[
            pl.BlockSpec((tm, D_MODEL), lambda i: (i, 0)),
            _resident((1, D_MODEL)),
            _resident((D_MODEL, 2 * D_FF)),
            _resident((CONV_W, D_FF)),
            _resident((1, D_FF)),
            _resident((D_FF, D_MODEL)),
            pl.BlockSpec((None, tm, pd), lambda i: (layer, i, 0)),
            _resident((1, D_MODEL)),
            _resident((D_MODEL, D_MODEL)),
            _resident((pd, D_MODEL)),
        ],
        out_specs=(pl.BlockSpec((tm, D_MODEL), lambda i: (i, 0)),
                   pl.BlockSpec((None, SUBLANES, D_FF), lambda i: (i, 0, 0))),
        scratch_shapes=[pltpu.VMEM((SUBLANES, D_FF), F32)],
        compiler_params=_cparams("arbitrary"),
        name="ffn_prompt",
    )(x, gain.reshape(1, D_MODEL), w_in, conv_w, conv_b.reshape(1, D_FF), w_out,
      p_layers, ple_gain.reshape(1, D_MODEL), w_gate, w_pin)


def _ffn_sample_kernel(g_ref, val_ref, s0_ref, s1_ref, cw_ref, cb_ref, wo_ref, x_ref, o_ref):
    c = cb_ref[...] + cw_ref[0:1] * s0_ref[...] + cw_ref[1:2] * s1_ref[...] + cw_ref[2:3] * g_ref[...]
    o_ref[...] = x_ref[...] + _dot(_ffn_gate(c, val_ref[...]), wo_ref[...])


def _ffn_sample(gv2, state, conv_w, conv_b, w_out, x):
    m = x.shape[0]
    full = lambda i: (0, 0)
    return pl.pallas_call(
        _ffn_sample_kernel,
        out_shape=jax.ShapeDtypeStruct((m, D_MODEL), F32),
        grid=(1,),
        in_specs=[
            pl.BlockSpec((None, m, D_FF), lambda i: (0, 0, 0)),
            pl.BlockSpec((None, m, D_FF), lambda i: (1, 0, 0)),
            pl.BlockSpec((m, D_FF), full),
            pl.BlockSpec((m, D_FF), full),
            pl.BlockSpec((CONV_W, D_FF), full),
            pl.BlockSpec((1, D_FF), full),
            pl.BlockSpec((D_FF, D_MODEL), full),
            pl.BlockSpec((m, D_MODEL), full),
        ],
        out_specs=pl.BlockSpec((m, D_MODEL), full),
        compiler_params=_cparams("arbitrary"),
        name="ffn_sample",
    )(gv2, gv2, state[:, 0], state[:, 1], conv_w, conv_b.reshape(1, D_FF), w_out, x)


def _qknorm_rope_kernel(x_ref, gain_ref, c_ref, s1_ref, s2_ref, bd_ref, o_ref, *maybe_kt_ref):
    x = x_ref[...]
    head_sumsq = _dot(_split_bf16(x * x), bd_ref[...])
    y = x * lax.rsqrt(head_sumsq * (1.0 / HEAD_DIM) + EPS) * gain_ref[...]
    reps = D_MODEL // LANES
    tile = lambda t: jnp.concatenate([t] * reps, axis=1)
    half = ROT_DIM // 2
    out = (y * tile(c_ref[...])
           + pltpu.roll(y, D_MODEL - half, axis=1) * tile(s1_ref[...])
           + pltpu.roll(y, half, axis=1) * tile(s2_ref[...]))
    o_ref[...] = out
    for kt_ref in maybe_kt_ref:
        @pl.when(pl.program_id(1) == 1)
        def _():
            kt_ref[...] = out.T


def _rope_tables(pos):
    half = ROT_DIM // 2
    inv_freq = ROPE_THETA ** (-jnp.arange(0, ROT_DIM, 2, dtype=F32) / ROT_DIM)
    ang = pos.astype(F32)[:, None] * inv_freq[None, :]
    cos, sin = jnp.cos(ang), jnp.sin(ang)
    n = pos.shape[0]
    pad = jnp.zeros((n, HEAD_DIM - ROT_DIM), F32)
    zeros = jnp.zeros((n, half), F32)
    c = jnp.concatenate([cos, cos, pad + 1.0], axis=1)
    s1 = jnp.concatenate([-sin, zeros, pad], axis=1)
    s2 = jnp.concatenate([zeros, sin, pad], axis=1)
    rep = lambda t: jnp.concatenate([t] * HEADS_PER_LANE_TILE, axis=1)
    return rep(c), rep(s1), rep(s2)


def _qknorm_rope(qkv3, q_gain, k_gain, pos, batch=None, tm=512):
    m = qkv3.shape[1]
    t = pos.shape[0]
    tm = min(tm, m, t)
    tiles_per_seq = t // tm
    gains = jnp.stack([jnp.tile(q_gain, N_HEADS), jnp.tile(k_gain, N_HEADS)]).reshape(2, 1, D_MODEL)
    c, s1, s2 = _rope_tables(pos)
    head = jnp.arange(D_MODEL) // HEAD_DIM
    bd = (head[:, None] == head[None, :]).astype(BF16)
    bd2 = jnp.concatenate([bd, bd], axis=0)
    tab = pl.BlockSpec((tm, LANES), lambda i, s: (i % tiles_per_seq, 0))
    out_shape = [jax.ShapeDtypeStruct((2, m, D_MODEL), F32)]
    out_specs = [pl.BlockSpec((None, tm, D_MODEL), lambda i, s: (s, i, 0))]
    if batch is not None:
        out_shape.append(jax.ShapeDtypeStruct((batch, D_MODEL, t), F32))
        out_specs.append(pl.BlockSpec((None, D_MODEL, tm), lambda i, s: (i // tiles_per_seq, 0, i % tiles_per_seq)))
    out = pl.pallas_call(
        _qknorm_rope_kernel,
        out_shape=tuple(out_shape),
        grid=(m // tm, 2),
        in_specs=[
            pl.BlockSpec((None, tm, D_MODEL), lambda i, s: (s, i, 0)),
            pl.BlockSpec((None, 1, D_MODEL), lambda i, s: (s, 0, 0)),
            tab, tab, tab,
            pl.BlockSpec((2 * D_MODEL, D_MODEL), lambda i, s: (0, 0)),
        ],
        out_specs=tuple(out_specs),
        compiler_params=_cparams("parallel", "arbitrary"),
        name="qknorm_rope",
    )(qkv3, gains, c, s1, s2, bd2)
    return out if batch is not None else out[0]


def _prompt_cache_rows(x_t):
    batch, _, seq_len = x_t.shape
    return jnp.transpose(x_t.reshape(batch, 1, N_HEADS, HEAD_DIM, seq_len), (0, 1, 4, 2, 3))


def _head_lane_masks():
    lane = lax.broadcasted_iota(jnp.int32, (1, LANES), 1)
    return [(lane // HEAD_DIM) == h for h in range(HEADS_PER_LANE_TILE)]


def _stack_heads(q):
    return jnp.concatenate([jnp.where(hm, q, 0.0) for hm in _head_lane_masks()], axis=0).astype(BF16)


def _unstack_heads(acc, tq):
    return sum(jnp.where(hm, acc[h * tq:(h + 1) * tq], 0.0) for h, hm in enumerate(_head_lane_masks()))


def _sb_prompt_kernel(q_ref, k_ref, v_ref, o_ref, r_ref, acc_ref, *, tq, tk):
    qi = pl.program_id(2)
    qs = _stack_heads(q_ref[...] * QK_SCALE)
    rows = qs.shape[0]
    u = _strict_upper_sum_matrix(tk)
    q_pos = lax.broadcasted_iota(jnp.int32, (rows, tq), 0) & (tq - 1)
    k_pos = lax.broadcasted_iota(jnp.int32, (rows, tq), 1)
    r_ref[...] = jnp.zeros_like(r_ref)
    acc_ref[...] = jnp.zeros_like(acc_ref)

    def chunk(c, diagonal):
        start = pl.multiple_of(c * tq, tq)
        lsp, lk = _log_sigmoid_pair(_dot_nt(qs, k_ref[pl.ds(start, tq), :].astype(BF16)))
        if diagonal:
            causal = k_pos < q_pos
            lsp, lk = jnp.where(causal, lsp, MASK_BIAS), jnp.where(causal, lk, 0.0)
        r = r_ref[...]
        weights = []
        for j in reversed(range(tq // tk)):
            cols = slice(j * tk, (j + 1) * tk)
            sums = _dot(lk[:, cols].astype(BF16), u)
            later = sums[:, :tk] + jnp.concatenate([r] * (tk // LANES), axis=1)
            weights.append(jnp.exp(lsp[:, cols] + later).astype(BF16))
            r = r + sums[:, tk:]
        r_ref[...] = r
        a = jnp.concatenate(weights[::-1], axis=1)
        acc_ref[...] += _dot(a, v_ref[pl.ds(start, tq), :].astype(BF16))
        return jnp.max(r)

    def older(state):
        c, _ = state
        return c - 1, chunk(c, False)

    lax.while_loop(lambda state: (state[0] >= 0) & (state[1] > EXP_UNDERFLOWS_BELOW), older,
                   (qi - 1, chunk(qi, True)))
    o_ref[...] = _unstack_heads(acc_ref[...], tq)


def _attention_specs(batch, seq_len, tq, q_sel, k_sel, v_sel):
    nq = seq_len // tq
    return dict(
        grid=(batch, N_HEAD_PAIRS, nq),
        in_specs=[
            pl.BlockSpec((None, tq, LANES), lambda b, hp, qi: (q_sel, b * nq + qi, hp)),
            pl.BlockSpec((None, seq_len, LANES), lambda b, hp, qi: (k_sel, b, hp)),
            pl.BlockSpec((None, seq_len, LANES), lambda b, hp, qi: (v_sel, b, hp)),
        ],
        out_specs=pl.BlockSpec((tq, LANES), lambda b, hp, qi: (b * nq + qi, hp)),
        out_shape=jax.ShapeDtypeStruct((batch * seq_len, D_MODEL), F32),
    )


def _sb_prompt(qkv3, batch, seq_len, tq=512, tk=256):
    tq = min(tq, seq_len)
    assert tq % tk == 0 and tq & (tq - 1) == 0
    return pl.pallas_call(
        functools.partial(_sb_prompt_kernel, tq=tq, tk=tk),
        scratch_shapes=[pltpu.VMEM((HEADS_PER_LANE_TILE * tq, LANES), F32)] * 2,
        compiler_params=_cparams("parallel", "parallel", "arbitrary"),
        name="sb_prompt",
        **_attention_specs(batch, seq_len, tq, 0, 1, 2),
    )(qkv3, qkv3, qkv3)


def _moba_prompt_kernel(q_ref, k_ref, v_ref, o_ref, kmean_ref, m_ref, l_ref, acc_ref, s0_ref, s1_ref, *,
                        n_blocks, group):
    qi = pl.program_id(2)
    blk = MOBA_BLOCK

    @pl.when(qi == 0)
    def _():
        kmean_ref[...] = jnp.zeros_like(kmean_ref)
        for n in range(n_blocks):
            kmean_ref[n:n + 1, :] = jnp.mean(k_ref[n * blk:(n + 1) * blk, :], axis=0, keepdims=True)

    qs = _stack_heads(q_ref[...] * QK_SCALE)
    rows = qs.shape[0]
    neg_inf = -jnp.inf
    gate_rows = -(-n_blocks // SUBLANES) * SUBLANES
    block = lax.broadcasted_iota(jnp.int32, (gate_rows, rows), 0)
    gate = jnp.where(block < qi, _dot_nt(kmean_ref[:gate_rows, :].astype(BF16), qs), neg_inf)
    sel = jnp.zeros((gate_rows, rows), jnp.bool_)
    for r in range(MOBA_TOPK):
        best = jnp.max(gate, axis=0, keepdims=True)
        idx = jnp.min(jnp.where(gate == best, block, LANES), axis=0, keepdims=True)
        pick = block == idx
        sel = sel | (pick & (r < qi))
        gate = jnp.where(pick, neg_inf, gate)
    bias = jnp.concatenate([jnp.where(sel, 0.0, MASK_BIAS),
                            jnp.full((LANES - gate_rows, rows), MASK_BIAS, F32)], axis=0)
    q_aug = jnp.concatenate([qs, bias.T.astype(BF16)], axis=1)

    q_pos = lax.broadcasted_iota(jnp.int32, (rows, blk), 0) & (blk - 1)
    k_pos = lax.broadcasted_iota(jnp.int32, (rows, blk), 1)
    start = pl.multiple_of(qi * blk, blk)
    def fold(scores, m, first_key):
        m_new = m
        for s in scores:
            m_new = jnp.maximum(m_new, jnp.max(s, axis=-1, keepdims=True))
        shift = jnp.concatenate([m_new] * (blk // LANES), axis=1)
        pv = 0.0
        for j, s in enumerate(scores):
            v = v_ref[pl.ds(pl.multiple_of(first_key + j * blk, blk), blk), :].astype(BF16)
            pv = pv + _dot(jnp.exp(s - shift).astype(BF16), jnp.concatenate([v, jnp.ones_like(v)], axis=1))
        return m_new, pv[:, :LANES], pv[:, LANES:]

    s = jnp.where(k_pos <= q_pos, _dot_nt(qs, k_ref[pl.ds(start, blk), :].astype(BF16)), neg_inf)
    m_ref[...], acc_ref[...], l_ref[...] = fold([s], jnp.full((rows, LANES), neg_inf, F32), start)

    k_lane = lax.broadcasted_iota(jnp.int32, (blk, LANES), 1)

    def score_group(i, s_ref):
        for j in range(group):
            keys = k_ref[pl.ds(pl.multiple_of((group * i + j) * blk, blk), blk), :].astype(BF16)
            one_hot = jnp.where(k_lane == group * i + j, 1.0, 0.0).astype(BF16)
            s_ref[j] = _dot_nt(q_aug, jnp.concatenate([keys, one_hot], axis=1))

    def fold_group(i, s_ref):
        m = m_ref[...]
        m_new, pv, p_sum = fold([s_ref[j] for j in range(group)], m, i * (group * blk))
        alpha = jnp.exp(m - m_new)
        m_ref[...] = m_new
        l_ref[...] = alpha * l_ref[...] + p_sum
        acc_ref[...] = alpha * acc_ref[...] + pv

    last_group = n_blocks // group - 1

    def group_pair(p, carry):
        score_group(2 * p + 1, s1_ref)
        fold_group(2 * p, s0_ref)
        score_group(jnp.minimum(2 * p + 2, last_group), s0_ref)
        fold_group(2 * p + 1, s1_ref)
        return carry

    score_group(0, s0_ref)
    lax.fori_loop(0, (qi + 2 * group - 1) // (2 * group), group_pair, 0)
    o_ref[...] = _unstack_heads(acc_ref[...] / l_ref[...], blk)


def _moba_prompt(qk2, qkv3, batch, seq_len, group=2):
    n_blocks = seq_len // MOBA_BLOCK
    assert n_blocks <= LANES and n_blocks % (2 * group) == 0
    specs = _attention_specs(batch, seq_len, MOBA_BLOCK, 0, 1, 2)
    rows = HEADS_PER_LANE_TILE * MOBA_BLOCK
    return pl.pallas_call(
        functools.partial(_moba_prompt_kernel, n_blocks=n_blocks, group=group),
        scratch_shapes=[pltpu.VMEM((LANES, LANES), F32)] + [pltpu.VMEM((rows, LANES), F32)] * 3
                       + [pltpu.VMEM((group, rows, MOBA_BLOCK), F32)] * 2,
        compiler_params=_cparams("parallel", "arbitrary", "arbitrary"),
        name="moba_prompt",
        **specs,
    )(qk2, qk2, qkv3)


PAGES_PER_STEP = 8
PAGES_PER_BLOCK = MOBA_BLOCK // PAGE_SIZE


PAGE_SHAPE = (N_HEADS, HEAD_DIM, PAGE_SIZE)
HEAD_ROW = (N_HEADS, PAGE_SIZE)
STEP_ROWS = (PAGES_PER_STEP * N_HEADS, PAGE_SIZE)


def _page_view(cache):
    return jnp.transpose(cache, (0, 1, 3, 4, 2))


def _head_columns(x):
    return x.reshape(x.shape[0], N_HEADS, HEAD_DIM, 1)


def _score_pages(z_ref, qb_ref, k_refs):
    for j, k_ref in enumerate(k_refs):
        for h in range(N_HEADS):
            row = j * N_HEADS + h
            z_ref[row:row + 1, :] = jnp.sum(k_ref[h] * qb_ref[h], axis=0, keepdims=True)


def _weigh_values(w_ref, v_refs, pages, h):
    return sum(w_ref[j * N_HEADS + h:j * N_HEADS + h + 1, :] * v_refs[j][h] for j in pages)


def _page_rows(x, j):
    return x[j * N_HEADS:(j + 1) * N_HEADS]


def _over_dims(x):
    return x[:, None, :]


def _lane_sum(x):
    return jnp.sum(x, axis=-1, keepdims=True)


def _suffix_sum_exclusive(x):
    lane = lax.broadcasted_iota(jnp.int32, x.shape, x.ndim - 1)
    inc = x
    d = 1
    while d < LANES:
        inc = inc + jnp.where(lane + d < LANES, pltpu.roll(inc, LANES - d, axis=x.ndim - 1), 0.0)
        d *= 2
    return inc - x


def _sb_decode_kernel(pt_ref, pos_ref, live_ref, q_ref, kown_ref, vown_ref, r_in_ref, part_in_ref, *refs,
                      newest):
    k_refs, v_refs = refs[:PAGES_PER_STEP], refs[PAGES_PER_STEP:2 * PAGES_PER_STEP]
    o_ref, r_out_ref, qb_ref, r_ref, acc_ref, z_ref, w_ref = refs[2 * PAGES_PER_STEP:]
    n = pl.program_id(1)

    @pl.when(n == 0)
    def _():
        q = q_ref[...] * QK_SCALE
        qb_ref[...] = jnp.broadcast_to(q, PAGE_SHAPE)
        if newest:
            valid = jnp.where(pos_ref[1] < pos_ref[0], 1.0, 0.0)
            lsp, lk = _log_sigmoid_pair(jnp.sum(q * kown_ref[...], axis=1))
            r_ref[...] = jnp.broadcast_to(lk * valid, HEAD_ROW)
            lane = lax.broadcasted_iota(jnp.int32, PAGE_SHAPE, 2)
            acc_ref[...] = jnp.where(lane == 0, _over_dims(jnp.exp(lsp) * valid) * vown_ref[...], 0.0)
        else:
            r_ref[...] = r_in_ref[...]
            acc_ref[...] = jnp.zeros_like(acc_ref)

    @pl.when(live_ref[pl.program_id(0)] != 0)
    def _():
        _score_pages(z_ref, qb_ref, k_refs)
        lsp, lk = _log_sigmoid_pair(z_ref[...])
        totals = _lane_sum(lk)
        r = r_ref[:, 0:1]
        later = [None] * PAGES_PER_STEP
        for j in reversed(range(PAGES_PER_STEP)):
            later[j] = r
            r = r + _page_rows(totals, j)
        r_ref[...] = jnp.broadcast_to(r, HEAD_ROW)
        w_ref[...] = jnp.exp(lsp + _suffix_sum_exclusive(lk) + jnp.concatenate(later, axis=0))
        for h in range(N_HEADS):
            acc_ref[h] += _weigh_values(w_ref, v_refs, range(PAGES_PER_STEP), h)

    @pl.when(n == pl.num_programs(1) - 1)
    def _():
        o_ref[...] = part_in_ref[...] + _lane_sum(acc_ref[...])
        r_out_ref[...] = r_ref[...]


def _decode_specs(n_steps, layer, newest_first):
    def page(j):
        def index_map(b, n, pt, *_):
            step = (n_steps - 1 - n) if newest_first else n
            return (pt[b, PAGES_PER_STEP * step + j], layer, 0, 0, 0)
        return pl.BlockSpec((None, None) + PAGE_SHAPE, index_map)

    col = pl.BlockSpec((None, N_HEADS, HEAD_DIM, 1), lambda b, n, *_: (b, 0, 0, 0))
    return col, [page(j) for j in range(PAGES_PER_STEP)]


def _sb_decode(q, k_own, v_own, cache_k, cache_v, page_table, layer, q_pos, k_own_pos):
    batch = q.shape[0]
    n_steps = page_table.shape[1] // PAGES_PER_STEP
    pos = jnp.array([q_pos, k_own_pos], jnp.int32)
    ck, cv = _page_view(cache_k), _page_view(cache_v)
    col = pl.BlockSpec((None, N_HEADS, HEAD_DIM, 1), lambda b, n, *_: (b, 0, 0, 0))
    row = pl.BlockSpec((None,) + HEAD_ROW, lambda b, n, *_: (b, 0, 0))
    cols = [_head_columns(t) for t in (q, k_own, v_own)]

    def walk(first_step, steps, live, r_in, part_in, newest):
        def page(j):
            def index_map(b, n, pt, pos, live):
                wanted = pt[b, PAGES_PER_STEP * (first_step - n) + j]
                return (jnp.where(live[b] != 0, wanted, pt[0, j]), layer, 0, 0, 0)
            return pl.BlockSpec((None, None) + PAGE_SHAPE, index_map)

        pages = [page(j) for j in range(PAGES_PER_STEP)]
        return pl.pallas_call(
            functools.partial(_sb_decode_kernel, newest=newest),
            out_shape=(jax.ShapeDtypeStruct((batch, N_HEADS, HEAD_DIM, 1), F32),
                       jax.ShapeDtypeStruct((batch,) + HEAD_ROW, F32)),
            grid_spec=pltpu.PrefetchScalarGridSpec(
                num_scalar_prefetch=3,
                grid=(batch, steps),
                in_specs=[col, col, col, row, col] + pages + pages,
                out_specs=(col, row),
                scratch_shapes=[pltpu.VMEM(PAGE_SHAPE, F32), pltpu.VMEM(HEAD_ROW, F32), pltpu.VMEM(PAGE_SHAPE, F32),
                                pltpu.VMEM(STEP_ROWS, F32), pltpu.VMEM(STEP_ROWS, F32)],
            ),
            compiler_params=_cparams("parallel", "arbitrary"),
            name="sb_decode",
        )(page_table, pos, live, *cols, r_in, part_in, *([ck] * PAGES_PER_STEP), *([cv] * PAGES_PER_STEP))

    zeros_r = jnp.zeros((batch,) + HEAD_ROW, F32)
    zeros_part = jnp.zeros((batch, N_HEADS, HEAD_DIM, 1), F32)
    part, r = walk(n_steps - 1, 1, jnp.ones((batch,), jnp.int32), zeros_r, zeros_part, True)
    if n_steps > 1:
        live = (jnp.max(r, axis=(1, 2)) > EXP_UNDERFLOWS_BELOW).astype(jnp.int32)
        part = lax.cond(jnp.any(live != 0),
                        lambda: walk(n_steps - 2, n_steps - 1, live, r, part, False)[0],
                        lambda: part)
    return part.reshape(batch, D_MODEL)


def _moba_decode_kernel(pt_ref, q_ref, kown_ref, vown_ref, *refs):
    k_refs, v_refs = refs[:PAGES_PER_STEP], refs[PAGES_PER_STEP:2 * PAGES_PER_STEP]
    o_ref, qb_ref, gate_ref, m_ref, l_ref, pv_ref, z_ref, w_ref = refs[2 * PAGES_PER_STEP:]
    n = pl.program_id(1)
    n_blocks = gate_ref.shape[0]

    @pl.when(n == 0)
    def _():
        qb_ref[...] = jnp.broadcast_to(q_ref[...] * QK_SCALE, PAGE_SHAPE)

    _score_pages(z_ref, qb_ref, k_refs)
    s = z_ref[...]
    page_max = jnp.max(s, axis=-1, keepdims=True)
    page_sum = _lane_sum(s)
    blocks = [range(b * PAGES_PER_BLOCK, (b + 1) * PAGES_PER_BLOCK) for b in range(PAGES_PER_STEP // PAGES_PER_BLOCK)]
    over_block = lambda x, pages, op: functools.reduce(op, [_page_rows(x, j) for j in pages])
    block_max = [over_block(page_max, pages, jnp.maximum) for pages in blocks]
    w_ref[...] = jnp.exp(s - jnp.concatenate([block_max[j // PAGES_PER_BLOCK] for j in range(PAGES_PER_STEP)], axis=0))
    weight_sum = _lane_sum(w_ref[...])
    for b, pages in enumerate(blocks):
        idx = n * len(blocks) + b
        gate_ref[idx] = jnp.broadcast_to(over_block(page_sum, pages, jnp.add) * (1.0 / MOBA_BLOCK), HEAD_ROW)
        m_ref[idx] = jnp.broadcast_to(block_max[b], HEAD_ROW)
        l_ref[idx] = jnp.broadcast_to(over_block(weight_sum, pages, jnp.add), HEAD_ROW)
        for h in range(N_HEADS):
            pv_ref[idx, h] = _weigh_values(w_ref, v_refs, pages, h)

    @pl.when(n == pl.num_programs(1) - 1)
    def _():
        shape = gate_ref.shape
        blk = lax.broadcasted_iota(jnp.int32, shape, 0)
        gate = gate_ref[...]
        sel = jnp.zeros(shape, jnp.bool_)
        for r in range(MOBA_TOPK):
            best = jnp.max(gate, axis=0, keepdims=True)
            idx = jnp.min(jnp.where(gate == best, blk, shape[0]), axis=0, keepdims=True)
            pick = blk == idx
            sel = sel | (pick & (r < n_blocks))
            gate = jnp.where(pick, -jnp.inf, gate)
        q = q_ref[...] * QK_SCALE
        s_own = jnp.broadcast_to(jnp.sum(q * kown_ref[...], axis=1), HEAD_ROW)
        m_all = m_ref[...]
        m_fin = jnp.maximum(jnp.max(jnp.where(sel, m_all, -jnp.inf), axis=0), s_own)
        w = jnp.where(sel, jnp.exp(m_all - m_fin[None]), 0.0)
        w_own = jnp.exp(s_own - m_fin)
        denom = jnp.sum(w * l_ref[...], axis=0) + w_own
        gate_ref[...] = w
        numer = lax.fori_loop(0, n_blocks, lambda i, acc: acc + _over_dims(gate_ref[i]) * pv_ref[i],
                              jnp.zeros(PAGE_SHAPE, F32))
        first_lane = lambda t: _over_dims(t)[:, :, 0:1]
        o_ref[...] = (_lane_sum(numer) + first_lane(w_own) * vown_ref[...]) / first_lane(denom)


def _moba_decode(q, k_own, v_own, cache_k, cache_v, page_table, layer):
    batch = q.shape[0]
    n_pages = page_table.shape[1]
    assert PAGES_PER_STEP % PAGES_PER_BLOCK == 0 and n_pages % PAGES_PER_STEP == 0
    n_blocks = n_pages // PAGES_PER_BLOCK
    col, pages = _decode_specs(n_pages // PAGES_PER_STEP, layer, False)
    ck, cv = _page_view(cache_k), _page_view(cache_v)
    stat = pltpu.VMEM((n_blocks,) + HEAD_ROW, F32)
    out = pl.pallas_call(
        _moba_decode_kernel,
        out_shape=jax.ShapeDtypeStruct((batch, N_HEADS, HEAD_DIM, 1), F32),
        grid_spec=pltpu.PrefetchScalarGridSpec(
            num_scalar_prefetch=1,
            grid=(batch, n_pages // PAGES_PER_STEP),
            in_specs=[col, col, col] + pages + pages,
            out_specs=col,
            scratch_shapes=[pltpu.VMEM(PAGE_SHAPE, F32), stat, stat, stat,
                            pltpu.VMEM((n_blocks,) + PAGE_SHAPE, F32),
                            pltpu.VMEM(STEP_ROWS, F32), pltpu.VMEM(STEP_ROWS, F32)],
        ),
        compiler_params=_cparams("parallel", "arbitrary"),
        name="moba_decode",
    )(page_table, _head_columns(q), _head_columns(k_own), _head_columns(v_own),
      *([ck] * PAGES_PER_STEP), *([cv] * PAGES_PER_STEP))
    return out.reshape(batch, D_MODEL)


def kernel(x_prompt, x_sample, cache_sb_k, cache_sb_v, cache_moba_k, cache_moba_v, state_ffn_conv, page_table, p_prompt, p_sample, mix_norm, ffn_norm, ple_norm, sgu_w_in, sgu_v_norm, sgu_w_s, sgu_b_s, sgu_w_out, sb_w_qkv, sb_w_out, moba_w_qkv, moba_q_norm, moba_k_norm, moba_w_out, ffn_w_in, ffn_conv_w, ffn_conv_b, ffn_w_out, ple_w_in, ple_w_gate):
    bp, tp, _ = x_prompt.shape
    bs, ts, _ = x_sample.shape
    assert ts == 1 and tp % (2 * SGU_CHUNK) == 0 and tp % MOBA_BLOCK == 0
    depth = mix_norm.shape[0]
    past_len = page_table.shape[1] * PAGE_SIZE
    assert past_len % MOBA_BLOCK == 0 and past_len % SGU_CHUNK == 0
    xp = x_prompt.reshape(bp * tp, D_MODEL)
    xs = x_sample.reshape(bs, D_MODEL)
    bf = lambda w: w.astype(BF16)
    heads_s = lambda t: t.reshape(bs, ts, N_HEADS, HEAD_DIM)

    sb_kp, sb_vp, sb_ks, sb_vs = [], [], [], []
    mb_kp, mb_vp, mb_ks, mb_vs = [], [], [], []
    sgu_vs, conv_p, conv_s = [], [], []

    for i in range(depth):
        kind, j = i % N_MIXERS, i // N_MIXERS
        if kind == 0:
            w_in, w_out = bf(sgu_w_in[j]), bf(sgu_w_out[j])
            zs = _norm_matmul(xs, mix_norm[i], w_in, 2, act="gelu")
            xp = _sgu_prompt(xp, mix_norm[i], w_in, sgu_v_norm[j], sgu_w_s[j], sgu_b_s[j], w_out)
            xs, v_s = _sgu_sample(zs, sgu_v_norm[j], sgu_w_s[j], sgu_b_s[j], w_out, xs)
            sgu_vs.append(v_s.reshape(bs, ts, SGU_WIDTH))
        elif kind == 1:
            w_qkv, w_out = bf(sb_w_qkv[j]), bf(sb_w_out[j])
            qkv_p, kv_t = _qkv_prompt(xp, mix_norm[i], w_qkv, bp, tp, (1, 2))
            qkv_s = _norm_matmul(xs, mix_norm[i], w_qkv, 3)
            xp = _proj_residual(_sb_prompt(qkv_p, bp, tp), w_out, xp)
            att = _sb_decode(qkv_s[0], qkv_s[1], qkv_s[2], cache_sb_k, cache_sb_v,
                             page_table, j, past_len, past_len)
            xs = _proj_residual(att, w_out, xs)
            sb_kp.append(_prompt_cache_rows(kv_t[0]))
            sb_vp.append(_prompt_cache_rows(kv_t[1]))
            sb_ks.append(heads_s(qkv_s[1]))
            sb_vs.append(heads_s(qkv_s[2]))
        else:
            w_qkv, w_out = bf(moba_w_qkv[j]), bf(moba_w_out[j])
            qkv_p, v_t = _qkv_prompt(xp, mix_norm[i], w_qkv, bp, tp, (2,))
            qkv_s = _norm_matmul(xs, mix_norm[i], w_qkv, 3)
            qk_p, k_t = _qknorm_rope(qkv_p, moba_q_norm[j], moba_k_norm[j], jnp.arange(tp, dtype=jnp.int32), bp)
            qk_s = _qknorm_rope(qkv_s, moba_q_norm[j], moba_k_norm[j], jnp.full((bs,), past_len, jnp.int32))
            xp = _proj_residual(_moba_prompt(qk_p, qkv_p, bp, tp), w_out, xp)
            att = _moba_decode(qk_s[0], qk_s[1], qkv_s[2], cache_moba_k, cache_moba_v, page_table, j)
            xs = _proj_residual(att, w_out, xs)
            mb_kp.append(_prompt_cache_rows(k_t))
            mb_vp.append(_prompt_cache_rows(v_t[0]))
            mb_ks.append(heads_s(qk_s[1]))
            mb_vs.append(heads_s(qkv_s[2]))

        w_in, w_out = bf(ffn_w_in[i]), bf(ffn_w_out[i])
        w_gate, w_pin = bf(ple_w_gate[i]), bf(ple_w_in[i])
        gv_s = _norm_matmul(xs, ffn_norm[i], w_in, 2)
        xp, g_tails = _ffn_prompt(xp, ffn_norm[i], w_in, ffn_conv_w[i], ffn_conv_b[i], w_out, tp,
                                  p_prompt.reshape(depth, bp * tp, -1), i, ple_norm[i], w_gate, w_pin)
        xs = _ffn_sample(gv_s, state_ffn_conv[i], ffn_conv_w[i], ffn_conv_b[i], w_out, xs)
        conv_p.append(g_tails.reshape(bp, -1, SUBLANES, D_FF)[:, -1, SUBLANES - (CONV_W - 1):])
        conv_s.append(jnp.concatenate([state_ffn_conv[i][:, 1:], gv_s[0][:, None]], axis=1))
        xs = _ple(xs, p_sample.reshape(depth, bs, -1), i, ple_norm[i], w_gate, w_pin)

    return (xp.reshape(bp, tp, D_MODEL), xs.reshape(bs, ts, D_MODEL),
            jnp.concatenate(sb_kp, axis=1), jnp.concatenate(sb_vp, axis=1),
            jnp.stack(sb_ks, axis=1), jnp.stack(sb_vs, axis=1),
            jnp.concatenate(mb_kp, axis=1), jnp.concatenate(mb_vp, axis=1),
            jnp.stack(mb_ks, axis=1), jnp.stack(mb_vs, axis=1),
            jnp.stack(sgu_vs, axis=0), jnp.stack(conv_p, axis=0), jnp.stack(conv_s, axis=0))
```

```python
import functools

import jax
import jax.numpy as jnp
from jax import lax
from jax.experimental import pallas as pl
from jax.experimental.pallas import tpu as pltpu

F32 = jnp.float32
BF16 = jnp.bfloat16

D_MODEL = 1024
N_HEADS = 16
HEAD_DIM = 64
PAGE_SIZE = 128
SGU_CHUNK = 128
SGU_GROUPS = 8
SGU_WIDTH = 2 * D_MODEL
SGU_GROUP_DIM = SGU_WIDTH // SGU_GROUPS
MOBA_BLOCK = 256
MOBA_TOPK = 3
ROT_DIM = HEAD_DIM // 4
ROPE_THETA = 500000.0
D_FF = 2816
CONV_W = 3
EPS = 1e-6
N_MIXERS = 3

LANES = 128
HEADS_PER_LANE_TILE = LANES // HEAD_DIM
N_HEAD_PAIRS = D_MODEL // LANES
QK_SCALE = HEAD_DIM ** -0.5
MASK_BIAS = -1e30
EXP_UNDERFLOWS_BELOW = -104.0
VMEM_LIMIT = 48 * 1024 * 1024


def _cparams(*sem):
    return pltpu.CompilerParams(dimension_semantics=sem, vmem_limit_bytes=VMEM_LIMIT)


def _rms(x, g):
    return x * lax.rsqrt(jnp.mean(x * x, axis=-1, keepdims=True) + EPS) * g


def _dot(a, b):
    return jnp.dot(a, b, preferred_element_type=F32)


def _dot_nt(a, b):
    return lax.dot_general(a, b, (((1,), (1,)), ((), ())), preferred_element_type=F32)


def _split_bf16(x):
    hi = x.astype(BF16)
    lo = (x - hi.astype(F32)).astype(BF16)
    return jnp.concatenate([hi, lo], axis=1)


def _strict_upper_sum_matrix(n):
    j = lax.broadcasted_iota(jnp.int32, (n, n + LANES), 0)
    s = lax.broadcasted_iota(jnp.int32, (n, n + LANES), 1)
    return jnp.where((j > s) | (s >= n), 1.0, 0.0).astype(BF16)


def _log_sigmoid_pair(z):
    lsp = jnp.minimum(z, 0.0) - jnp.log(1.0 + jnp.exp(-jnp.abs(z)))
    return lsp, lsp - z


def _norm_matmul_kernel(x_ref, g_ref, w_ref, o_ref, xn_ref, *, act):
    @pl.when(pl.program_id(1) == 0)
    def _():
        xn_ref[...] = _rms(x_ref[...], g_ref[...]).astype(BF16)

    y = _dot(xn_ref[...], w_ref[...])
    if act == "gelu":
        y = jax.nn.gelu(y)
    o_ref[...] = y


def _lane_tile(n, cap):
    return max(t for t in range(LANES, cap + 1, LANES) if n % t == 0)


def _norm_matmul(x, gain, w, n_split, act=None, tm=512, tn_cap=1536):
    m, k = x.shape
    n_out = w.shape[1]
    per = n_out // n_split
    tm = min(tm, m)
    tn = _lane_tile(per, tn_cap)
    nj = per // tn
    assert m % tm == 0 and per % tn == 0
    return pl.pallas_call(
        functools.partial(_norm_matmul_kernel, act=act),
        out_shape=jax.ShapeDtypeStruct((n_split, m, per), F32),
        grid=(m // tm, n_out // tn),
        in_specs=[
            pl.BlockSpec((tm, k), lambda i, j: (i, 0)),
            pl.BlockSpec((1, k), lambda i, j: (0, 0)),
            pl.BlockSpec((k, tn), lambda i, j: (0, j)),
        ],
        out_specs=pl.BlockSpec((None, tm, tn), lambda i, j: (j // nj, i, j % nj)),
        scratch_shapes=[pltpu.VMEM((tm, k), BF16)],
        compiler_params=_cparams("parallel", "arbitrary"),
        name="norm_matmul",
    )(x, gain.reshape(1, k), w)


def _qkv_prompt_kernel(x_ref, g_ref, w_ref, o_ref, *refs, transposed):
    t_refs, xn_ref = refs[:-1], refs[-1]
    j = pl.program_id(1)

    @pl.when(j == 0)
    def _():
        xn_ref[...] = _rms(x_ref[...], g_ref[...]).astype(BF16)

    y = _dot(xn_ref[...], w_ref[...])
    o_ref[...] = y
    for split, t_ref in zip(transposed, t_refs):
        @pl.when(j == split)
        def _():
            t_ref[...] = y.T


def _qkv_prompt(x, gain, w, batch, seq_len, transposed, tm=512):
    m, k = x.shape
    tiles = seq_len // tm
    t_shape = jax.ShapeDtypeStruct((batch, D_MODEL, seq_len), F32)
    t_spec = pl.BlockSpec((None, D_MODEL, tm), lambda i, j: (i // tiles, 0, i % tiles))
    return pl.pallas_call(
        functools.partial(_qkv_prompt_kernel, transposed=transposed),
        out_shape=(jax.ShapeDtypeStruct((3, m, D_MODEL), F32),) + (t_shape,) * len(transposed),
        grid=(m // tm, 3),
        in_specs=[
            pl.BlockSpec((tm, k), lambda i, j: (i, 0)),
            pl.BlockSpec((1, k), lambda i, j: (0, 0)),
            pl.BlockSpec((k, D_MODEL), lambda i, j: (0, j)),
        ],
        out_specs=(pl.BlockSpec((None, tm, D_MODEL), lambda i, j: (j, i, 0)),) + (t_spec,) * len(transposed),
        scratch_shapes=[pltpu.VMEM((tm, k), BF16)],
        compiler_params=_cparams("parallel", "arbitrary"),
        name="qkv_prompt",
    )(x, gain.reshape(1, k), w)


def _proj_residual_kernel(h_ref, w_ref, x_ref, o_ref):
    o_ref[...] = x_ref[...] + _dot(h_ref[...].astype(BF16), w_ref[...])


def _proj_residual(h, w, x, tm=1024):
    m, k = h.shape
    n = w.shape[1]
    tm = min(tm, m)
    return pl.pallas_call(
        _proj_residual_kernel,
        out_shape=jax.ShapeDtypeStruct((m, n), F32),
        grid=(m // tm,),
        in_specs=[
            pl.BlockSpec((tm, k), lambda i: (i, 0)),
            pl.BlockSpec((k, n), lambda i: (0, 0)),
            pl.BlockSpec((tm, n), lambda i: (i, 0)),
        ],
        out_specs=pl.BlockSpec((tm, n), lambda i: (i, 0)),
        compiler_params=_cparams("parallel"),
        name="proj_residual",
    )(h, w, x)


def _ple_kernel(x_ref, p_ref, g_ref, wg_ref, wp_ref, o_ref):
    x = x_ref[...]
    gate = jax.nn.sigmoid(_dot(_rms(x, g_ref[...]).astype(BF16), wg_ref[...]))
    o_ref[...] = x + gate * _dot(p_ref[...].astype(BF16), wp_ref[...])


def _ple(x, p_layers, layer, gain, w_gate, w_in, tm=512):
    m, d = x.shape
    pd = p_layers.shape[2]
    tm = min(tm, m)
    return pl.pallas_call(
        _ple_kernel,
        out_shape=jax.ShapeDtypeStruct((m, d), F32),
        grid=(m // tm,),
        in_specs=[
            pl.BlockSpec((tm, d), lambda i: (i, 0)),
            pl.BlockSpec((None, tm, pd), lambda i: (layer, i, 0)),
            pl.BlockSpec((1, d), lambda i: (0, 0)),
            pl.BlockSpec((d, d), lambda i: (0, 0)),
            pl.BlockSpec((pd, d), lambda i: (0, 0)),
        ],
        out_specs=pl.BlockSpec((tm, d), lambda i: (i, 0)),
        compiler_params=_cparams("parallel"),
        name="ple",
    )(x, p_layers, gain.reshape(1, d), w_gate, w_in)


def _sgu_prompt_kernel(x_ref, gain_ref, win_ref, vg_ref, ws_ref, bst_ref, wo_ref, o_ref, *, chunks):
    x = x_ref[...]
    xn = _rms(x, gain_ref[...]).astype(BF16)
    v = _rms(jax.nn.gelu(_dot(xn, win_ref[:, SGU_WIDTH:])), vg_ref[...]).astype(BF16)
    i = lax.broadcasted_iota(jnp.int32, (SGU_CHUNK, SGU_CHUNK), 0)
    j = lax.broadcasted_iota(jnp.int32, (SGU_CHUNK, SGU_CHUNK), 1)
    gated = []
    for g in range(SGU_GROUPS):
        cols = slice(g * SGU_GROUP_DIM, (g + 1) * SGU_GROUP_DIM)
        w = jnp.where(i >= j, ws_ref[g], 0.0).astype(BF16)
        bias = bst_ref[:, g:g + 1]
        mixed = jnp.concatenate(
            [_dot(w, v[c * SGU_CHUNK:(c + 1) * SGU_CHUNK, cols]) + bias for c in range(chunks)], axis=0)
        u = jax.nn.gelu(_dot(xn, win_ref[:, cols]))
        gated.append((u * mixed).astype(BF16))
    o_ref[...] = x + _dot(jnp.concatenate(gated, axis=1), wo_ref[...])


def _sgu_prompt(x, gain, w_in, v_gain, w_s, b_s, w_out, chunks=8):
    m = x.shape[0]
    tm = chunks * SGU_CHUNK
    return pl.pallas_call(
        functools.partial(_sgu_prompt_kernel, chunks=chunks),
        out_shape=jax.ShapeDtypeStruct((m, D_MODEL), F32),
        grid=(m // tm,),
        in_specs=[
            pl.BlockSpec((tm, D_MODEL), lambda i: (i, 0)),
            _resident((1, D_MODEL)),
            _resident((D_MODEL, 2 * SGU_WIDTH)),
            _resident((1, SGU_WIDTH)),
            _resident((SGU_GROUPS, SGU_CHUNK, SGU_CHUNK)),
            _resident((SGU_CHUNK, SGU_GROUPS)),
            _resident((SGU_WIDTH, D_MODEL)),
        ],
        out_specs=pl.BlockSpec((tm, D_MODEL), lambda i: (i, 0)),
        compiler_params=_cparams("parallel"),
        name="sgu_prompt",
    )(x, gain.reshape(1, D_MODEL), w_in, v_gain.reshape(1, SGU_WIDTH), w_s, b_s.T, w_out)


def _sgu_sample_kernel(u_ref, v_ref, vg_ref, w0_ref, b0_ref, wo_ref, x_ref, o_ref, vn_ref):
    v = _rms(v_ref[...], vg_ref[...])
    vn_ref[...] = v
    gated = u_ref[...] * (w0_ref[...] * v.astype(BF16).astype(F32) + b0_ref[...])
    o_ref[...] = x_ref[...] + _dot(gated.astype(BF16), wo_ref[...])


def _sgu_sample(z2, v_gain, w_s, b_s, w_out, x):
    m = x.shape[0]
    w0 = jnp.repeat(w_s[:, 0, 0].astype(BF16).astype(F32), SGU_GROUP_DIM).reshape(1, SGU_WIDTH)
    b0 = jnp.repeat(b_s[:, 0], SGU_GROUP_DIM).reshape(1, SGU_WIDTH)
    row = lambda i: (0, 0)
    return pl.pallas_call(
        _sgu_sample_kernel,
        out_shape=(jax.ShapeDtypeStruct((m, D_MODEL), F32), jax.ShapeDtypeStruct((m, SGU_WIDTH), F32)),
        grid=(1,),
        in_specs=[
            pl.BlockSpec((None, m, SGU_WIDTH), lambda i: (0, 0, 0)),
            pl.BlockSpec((None, m, SGU_WIDTH), lambda i: (1, 0, 0)),
            pl.BlockSpec((1, SGU_WIDTH), row),
            pl.BlockSpec((1, SGU_WIDTH), row),
            pl.BlockSpec((1, SGU_WIDTH), row),
            pl.BlockSpec((SGU_WIDTH, D_MODEL), row),
            pl.BlockSpec((m, D_MODEL), row),
        ],
        out_specs=(pl.BlockSpec((m, D_MODEL), row), pl.BlockSpec((m, SGU_WIDTH), row)),
        compiler_params=_cparams("arbitrary"),
        name="sgu_sample",
    )(z2, z2, v_gain.reshape(1, SGU_WIDTH), w0, b0, w_out, x)


def _ffn_gate(c, val):
    return (jax.nn.silu(c) * val).astype(BF16)


SUBLANES = 8
FFN_CHUNK = 256


def _resident(shape):
    return pl.BlockSpec(shape, lambda i: (0,) * len(shape), pipeline_mode=pl.Buffered(1))


def _ffn_prompt_kernel(x_ref, gain_ref, win_ref, cw_ref, cb_ref, wo_ref, p_ref, pgain_ref, wg_ref, wp_ref,
                       o_ref, tail_ref, prev_ref, *, tiles_per_seq):
    x = x_ref[...]
    tm = x.shape[0]
    xn = _rms(x, gain_ref[...]).astype(BF16)
    @pl.when(pl.program_id(0) % tiles_per_seq == 0)
    def _():
        prev_ref[...] = jnp.zeros_like(prev_ref)

    row = lax.broadcasted_iota(jnp.int32, (tm, 1), 0)
    hidden = []
    for c0 in range(0, D_FF, FFN_CHUNK):
        cols = slice(c0, c0 + FFN_CHUNK)
        g = _dot(xn, win_ref[:, cols])
        val = _dot(xn, win_ref[:, D_FF + c0:D_FF + c0 + FFN_CHUNK])
        prev = prev_ref[:, cols]
        g1 = jnp.where(row == 0, prev[7:8], pltpu.roll(g, 1, axis=0))
        g2 = jnp.where(row == 0, prev[6:7], jnp.where(row == 1, prev[7:8], pltpu.roll(g, 2, axis=0)))
        c = cb_ref[:, cols] + cw_ref[0:1, cols] * g2 + cw_ref[1:2, cols] * g1 + cw_ref[2:3, cols] * g
        hidden.append(_ffn_gate(c, val))
        prev_ref[:, cols] = g[tm - SUBLANES:, :]
        tail_ref[:, cols] = g[tm - SUBLANES:, :]
    acc = x + _dot(jnp.concatenate(hidden, axis=1), wo_ref[...])
    gate = jax.nn.sigmoid(_dot(_rms(acc, pgain_ref[...]).astype(BF16), wg_ref[...]))
    o_ref[...] = acc + gate * _dot(p_ref[...].astype(BF16), wp_ref[...])


def _ffn_prompt(x, gain, w_in, conv_w, conv_b, w_out, seq_len, p_layers, layer, ple_gain, w_gate, w_pin, tm=1024):
    m = x.shape[0]
    pd = p_layers.shape[2]
    assert seq_len % tm == 0 and D_FF % FFN_CHUNK == 0
    return pl.pallas_call(
        functools.partial(_ffn_prompt_kernel, tiles_per_seq=seq_len // tm),
        out_shape=(jax.ShapeDtypeStruct((m, D_MODEL), F32),
                   jax.ShapeDtypeStruct((m // tm, SUBLANES, D_FF), F32)),
        grid=(m // tm,),
        in_specs=[
            pl.BlockSpec((tm, D_MODEL), lambda i: (i, 0)),
            _resident((1, D_MODEL)),
            _resident((D_MODEL, 2 * D_FF)),
            _resident((CONV_W, D_FF)),
            _resident((1, D_FF)),
            _resident((D_FF, D_MODEL)),
            pl.BlockSpec((None, tm, pd), lambda i: (layer, i, 0)),
            _resident((1, D_MODEL)),
            _resident((D_MODEL, D_MODEL)),
            _resident((pd, D_MODEL)),
        ],
        out_specs=(pl.BlockSpec((tm, D_MODEL), lambda i: (i, 0)),
                   pl.BlockSpec((None, SUBLANES, D_FF), lambda i: (i, 0, 0))),
        scratch_shapes=[pltpu.VMEM((SUBLANES, D_FF), F32)],
        compiler_params=_cparams("arbitrary"),
        name="ffn_prompt",
    )(x, gain.reshape(1, D_MODEL), w_in, conv_w, conv_b.reshape(1, D_FF), w_out,
      p_layers, ple_gain.reshape(1, D_MODEL), w_gate, w_pin)


def _ffn_sample_kernel(g_ref, val_ref, s0_ref, s1_ref, cw_ref, cb_ref, wo_ref, x_ref, o_ref):
    c = cb_ref[...] + cw_ref[0:1] * s0_ref[...] + cw_ref[1:2] * s1_ref[...] + cw_ref[2:3] * g_ref[...]
    o_ref[...] = x_ref[...] + _dot(_ffn_gate(c, val_ref[...]), wo_ref[...])


def _ffn_sample(gv2, state, conv_w, conv_b, w_out, x):
    m = x.shape[0]
    full = lambda i: (0, 0)
    return pl.pallas_call(
        _ffn_sample_kernel,
        out_shape=jax.ShapeDtypeStruct((m, D_MODEL), F32),
        grid=(1,),
        in_specs=[
            pl.BlockSpec((None, m, D_FF), lambda i: (0, 0, 0)),
            pl.BlockSpec((None, m, D_FF), lambda i: (1, 0, 0)),
            pl.BlockSpec((m, D_FF), full),
            pl.BlockSpec((m, D_FF), full),
            pl.BlockSpec((CONV_W, D_FF), full),
            pl.BlockSpec((1, D_FF), full),
            pl.BlockSpec((D_FF, D_MODEL), full),
            pl.BlockSpec((m, D_MODEL), full),
        ],
        out_specs=pl.BlockSpec((m, D_MODEL), full),
        compiler_params=_cparams("arbitrary"),
        name="ffn_sample",
    )(gv2, gv2, state[:, 0], state[:, 1], conv_w, conv_b.reshape(1, D_FF), w_out, x)


def _qknorm_rope_kernel(x_ref, gain_ref, c_ref, s1_ref, s2_ref, bd_ref, o_ref, *maybe_kt_ref):
    x = x_ref[...]
    head_sumsq = _dot(_split_bf16(x * x), bd_ref[...])
    y = x * lax.rsqrt(head_sumsq * (1.0 / HEAD_DIM) + EPS) * gain_ref[...]
    reps = D_MODEL // LANES
    tile = lambda t: jnp.concatenate([t] * reps, axis=1)
    half = ROT_DIM // 2
    out = (y * tile(c_ref[...])
           + pltpu.roll(y, D_MODEL - half, axis=1) * tile(s1_ref[...])
           + pltpu.roll(y, half, axis=1) * tile(s2_ref[...]))
    o_ref[...] = out
    for kt_ref in maybe_kt_ref:
        @pl.when(pl.program_id(1) == 1)
        def _():
            kt_ref[...] = out.T


def _rope_tables(pos):
    half = ROT_DIM // 2
    inv_freq = ROPE_THETA ** (-jnp.arange(0, ROT_DIM, 2, dtype=F32) / ROT_DIM)
    ang = pos.astype(F32)[:, None] * inv_freq[None, :]
    cos, sin = jnp.cos(ang), jnp.sin(ang)
    n = pos.shape[0]
    pad = jnp.zeros((n, HEAD_DIM - ROT_DIM), F32)
    zeros = jnp.zeros((n, half), F32)
    c = jnp.concatenate([cos, cos, pad + 1.0], axis=1)
    s1 = jnp.concatenate([-sin, zeros, pad], axis=1)
    s2 = jnp.concatenate([zeros, sin, pad], axis=1)
    rep = lambda t: jnp.concatenate([t] * HEADS_PER_LANE_TILE, axis=1)
    return rep(c), rep(s1), rep(s2)


def _qknorm_rope(qkv3, q_gain, k_gain, pos, batch=None, tm=512):
    m = qkv3.shape[1]
    t = pos.shape[0]
    tm = min(tm, m, t)
    tiles_per_seq = t // tm
    gains = jnp.stack([jnp.tile(q_gain, N_HEADS), jnp.tile(k_gain, N_HEADS)]).reshape(2, 1, D_MODEL)
    c, s1, s2 = _rope_tables(pos)
    head = jnp.arange(D_MODEL) // HEAD_DIM
    bd = (head[:, None] == head[None, :]).astype(BF16)
    bd2 = jnp.concatenate([bd, bd], axis=0)
    tab = pl.BlockSpec((tm, LANES), lambda i, s: (i % tiles_per_seq, 0))
    out_shape = [jax.ShapeDtypeStruct((2, m, D_MODEL), F32)]
    out_specs = [pl.BlockSpec((None, tm, D_MODEL), lambda i, s: (s, i, 0))]
    if batch is not None:
        out_shape.append(jax.ShapeDtypeStruct((batch, D_MODEL, t), F32))
        out_specs.append(pl.BlockSpec((None, D_MODEL, tm), lambda i, s: (i // tiles_per_seq, 0, i % tiles_per_seq)))
    out = pl.pallas_call(
        _qknorm_rope_kernel,
        out_shape=tuple(out_shape),
        grid=(m // tm, 2),
        in_specs=[
            pl.BlockSpec((None, tm, D_MODEL), lambda i, s: (s, i, 0)),
            pl.BlockSpec((None, 1, D_MODEL), lambda i, s: (s, 0, 0)),
            tab, tab, tab,
            pl.BlockSpec((2 * D_MODEL, D_MODEL), lambda i, s: (0, 0)),
        ],
        out_specs=tuple(out_specs),
        compiler_params=_cparams("parallel", "arbitrary"),
        name="qknorm_rope",
    )(qkv3, gains, c, s1, s2, bd2)
    return out if batch is not None else out[0]


def _prompt_cache_rows(x_t):
    batch, _, seq_len = x_t.shape
    return jnp.transpose(x_t.reshape(batch, 1, N_HEADS, HEAD_DIM, seq_len), (0, 1, 4, 2, 3))


def _head_lane_masks():
    lane = lax.broadcasted_iota(jnp.int32, (1, LANES), 1)
    return [(lane // HEAD_DIM) == h for h in range(HEADS_PER_LANE_TILE)]


def _stack_heads(q):
    return jnp.concatenate([jnp.where(hm, q, 0.0) for hm in _head_lane_masks()], axis=0).astype(BF16)


def _unstack_heads(acc, tq):
    return sum(jnp.where(hm, acc[h * tq:(h + 1) * tq], 0.0) for h, hm in enumerate(_head_lane_masks()))


def _sb_prompt_kernel(q_ref, k_ref, v_ref, o_ref, r_ref, acc_ref, *, tq, tk):
    qi = pl.program_id(2)
    qs = _stack_heads(q_ref[...] * QK_SCALE)
    rows = qs.shape[0]
    u = _strict_upper_sum_matrix(tk)
    q_pos = lax.broadcasted_iota(jnp.int32, (rows, tq), 0) & (tq - 1)
    k_pos = lax.broadcasted_iota(jnp.int32, (rows, tq), 1)
    r_ref[...] = jnp.zeros_like(r_ref)
    acc_ref[...] = jnp.zeros_like(acc_ref)

    def chunk(c, diagonal):
        start = pl.multiple_of(c * tq, tq)
        lsp, lk = _log_sigmoid_pair(_dot_nt(qs, k_ref[pl.ds(start, tq), :].astype(BF16)))
        if diagonal:
            causal = k_pos < q_pos
            lsp, lk = jnp.where(causal, lsp, MASK_BIAS), jnp.where(causal, lk, 0.0)
        r = r_ref[...]
        weights = []
        for j in reversed(range(tq // tk)):
            cols = slice(j * tk, (j + 1) * tk)
            sums = _dot(lk[:, cols].astype(BF16), u)
            later = sums[:, :tk] + jnp.concatenate([r] * (tk // LANES), axis=1)
            weights.append(jnp.exp(lsp[:, cols] + later).astype(BF16))
            r = r + sums[:, tk:]
        r_ref[...] = r
        a = jnp.concatenate(weights[::-1], axis=1)
        acc_ref[...] += _dot(a, v_ref[pl.ds(start, tq), :].astype(BF16))
        return jnp.max(r)

    def older(state):
        c, _ = state
        return c - 1, chunk(c, False)

    lax.while_loop(lambda state: (state[0] >= 0) & (state[1] > EXP_UNDERFLOWS_BELOW), older,
                   (qi - 1, chunk(qi, True)))
    o_ref[...] = _unstack_heads(acc_ref[...], tq)


def _attention_specs(batch, seq_len, tq, q_sel, k_sel, v_sel):
    nq = seq_len // tq
    return dict(
        grid=(batch, N_HEAD_PAIRS, nq),
        in_specs=[
            pl.BlockSpec((None, tq, LANES), lambda b, hp, qi: (q_sel, b * nq + qi, hp)),
            pl.BlockSpec((None, seq_len, LANES), lambda b, hp, qi: (k_sel, b, hp)),
            pl.BlockSpec((None, seq_len, LANES), lambda b, hp, qi: (v_sel, b, hp)),
        ],
        out_specs=pl.BlockSpec((tq, LANES), lambda b, hp, qi: (b * nq + qi, hp)),
        out_shape=jax.ShapeDtypeStruct((batch * seq_len, D_MODEL), F32),
    )


def _sb_prompt(qkv3, batch, seq_len, tq=512, tk=256):
    tq = min(tq, seq_len)
    assert tq % tk == 0 and tq & (tq - 1) == 0
    return pl.pallas_call(
        functools.partial(_sb_prompt_kernel, tq=tq, tk=tk),
        scratch_shapes=[pltpu.VMEM((HEADS_PER_LANE_TILE * tq, LANES), F32)] * 2,
        compiler_params=_cparams("parallel", "parallel", "arbitrary"),
        name="sb_prompt",
        **_attention_specs(batch, seq_len, tq, 0, 1, 2),
    )(qkv3, qkv3, qkv3)


def _moba_prompt_kernel(q_ref, k_ref, v_ref, o_ref, kmean_ref, m_ref, l_ref, acc_ref, s0_ref, s1_ref, *,
                        n_blocks, group):
    qi = pl.program_id(2)
    blk = MOBA_BLOCK

    @pl.when(qi == 0)
    def _():
        kmean_ref[...] = jnp.zeros_like(kmean_ref)
        for n in range(n_blocks):
            kmean_ref[n:n + 1, :] = jnp.mean(k_ref[n * blk:(n + 1) * blk, :], axis=0, keepdims=True)

    qs = _stack_heads(q_ref[...] * QK_SCALE)
    rows = qs.shape[0]
    neg_inf = -jnp.inf
    gate_rows = -(-n_blocks // SUBLANES) * SUBLANES
    block = lax.broadcasted_iota(jnp.int32, (gate_rows, rows), 0)
    gate = jnp.where(block < qi, _dot_nt(kmean_ref[:gate_rows, :].astype(BF16), qs), neg_inf)
    sel = jnp.zeros((gate_rows, rows), jnp.bool_)
    for r in range(MOBA_TOPK):
        best = jnp.max(gate, axis=0, keepdims=True)
        idx = jnp.min(jnp.where(gate == best, block, LANES), axis=0, keepdims=True)
        pick = block == idx
        sel = sel | (pick & (r < qi))
        gate = jnp.where(pick, neg_inf, gate)
    bias = jnp.concatenate([jnp.where(sel, 0.0, MASK_BIAS),
                            jnp.full((LANES - gate_rows, rows), MASK_BIAS, F32)], axis=0)
    q_aug = jnp.concatenate([qs, bias.T.astype(BF16)], axis=1)

    q_pos = lax.broadcasted_iota(jnp.int32, (rows, blk), 0) & (blk - 1)
    k_pos = lax.broadcasted_iota(jnp.int32, (rows, blk), 1)
    start = pl.multiple_of(qi * blk, blk)
    def fold(scores, m, first_key):
        m_new = m
        for s in scores:
            m_new = jnp.maximum(m_new, jnp.max(s, axis=-1, keepdims=True))
        shift = jnp.concatenate([m_new] * (blk // LANES), axis=1)
        pv = 0.0
        for j, s in enumerate(scores):
            v = v_ref[pl.ds(pl.multiple_of(first_key + j * blk, blk), blk), :].astype(BF16)
            pv = pv + _dot(jnp.exp(s - shift).astype(BF16), jnp.concatenate([v, jnp.ones_like(v)], axis=1))
        return m_new, pv[:, :LANES], pv[:, LANES:]

    s = jnp.where(k_pos <= q_pos, _dot_nt(qs, k_ref[pl.ds(start, blk), :].astype(BF16)), neg_inf)
    m_ref[...], acc_ref[...], l_ref[...] = fold([s], jnp.full((rows, LANES), neg_inf, F32), start)

    k_lane = lax.broadcasted_iota(jnp.int32, (blk, LANES), 1)

    def score_group(i, s_ref):
        for j in range(group):
            keys = k_ref[pl.ds(pl.multiple_of((group * i + j) * blk, blk), blk), :].astype(BF16)
            one_hot = jnp.where(k_lane == group * i + j, 1.0, 0.0).astype(BF16)
            s_ref[j] = _dot_nt(q_aug, jnp.concatenate([keys, one_hot], axis=1))

    def fold_group(i, s_ref):
        m = m_ref[...]
        m_new, pv, p_sum = fold([s_ref[j] for j in range(group)], m, i * (group * blk))
        alpha = jnp.exp(m - m_new)
        m_ref[...] = m_new
        l_ref[...] = alpha * l_ref[...] + p_sum
        acc_ref[...] = alpha * acc_ref[...] + pv

    last_group = n_blocks // group - 1

    def group_pair(p, carry):
        score_group(2 * p + 1, s1_ref)
        fold_group(2 * p, s0_ref)
        score_group(jnp.minimum(2 * p + 2, last_group), s0_ref)
        fold_group(2 * p + 1, s1_ref)
        return carry

    score_group(0, s0_ref)
    lax.fori_loop(0, (qi + 2 * group - 1) // (2 * group), group_pair, 0)
    o_ref[...] = _unstack_heads(acc_ref[...] / l_ref[...], blk)


def _moba_prompt(qk2, qkv3, batch, seq_len, group=2):
    n_blocks = seq_len // MOBA_BLOCK
    assert n_blocks <= LANES and n_blocks % (2 * group) == 0
    specs = _attention_specs(batch, seq_len, MOBA_BLOCK, 0, 1, 2)
    rows = HEADS_PER_LANE_TILE * MOBA_BLOCK
    return pl.pallas_call(
        functools.partial(_moba_prompt_kernel, n_blocks=n_blocks, group=group),
        scratch_shapes=[pltpu.VMEM((LANES, LANES), F32)] + [pltpu.VMEM((rows, LANES), F32)] * 3
                       + [pltpu.VMEM((group, rows, MOBA_BLOCK), F32)] * 2,
        compiler_params=_cparams("parallel", "arbitrary", "arbitrary"),
        name="moba_prompt",
        **specs,
    )(qk2, qk2, qkv3)


PAGES_PER_STEP = 8
PAGES_PER_BLOCK = MOBA_BLOCK // PAGE_SIZE


PAGE_SHAPE = (N_HEADS, HEAD_DIM, PAGE_SIZE)
HEAD_ROW = (N_HEADS, PAGE_SIZE)
STEP_ROWS = (PAGES_PER_STEP * N_HEADS, PAGE_SIZE)


def _page_view(cache):
    return jnp.transpose(cache, (0, 1, 3, 4, 2))


def _head_columns(x):
    return x.reshape(x.shape[0], N_HEADS, HEAD_DIM, 1)


def _score_pages(z_ref, qb_ref, k_refs):
    for j, k_ref in enumerate(k_refs):
        for h in range(N_HEADS):
            row = j * N_HEADS + h
            z_ref[row:row + 1, :] = jnp.sum(k_ref[h] * qb_ref[h], axis=0, keepdims=True)


def _weigh_values(w_ref, v_refs, pages, h):
    return sum(w_ref[j * N_HEADS + h:j * N_HEADS + h + 1, :] * v_refs[j][h] for j in pages)


def _page_rows(x, j):
    return x[j * N_HEADS:(j + 1) * N_HEADS]


def _over_dims(x):
    return x[:, None, :]


def _lane_sum(x):
    return jnp.sum(x, axis=-1, keepdims=True)


def _suffix_sum_exclusive(x):
    lane = lax.broadcasted_iota(jnp.int32, x.shape, x.ndim - 1)
    inc = x
    d = 1
    while d < LANES:
        inc = inc + jnp.where(lane + d < LANES, pltpu.roll(inc, LANES - d, axis=x.ndim - 1), 0.0)
        d *= 2
    return inc - x


def _sb_decode_kernel(pt_ref, pos_ref, live_ref, q_ref, kown_ref, vown_ref, r_in_ref, part_in_ref, *refs,
                      newest):
    k_refs, v_refs = refs[:PAGES_PER_STEP], refs[PAGES_PER_STEP:2 * PAGES_PER_STEP]
    o_ref, r_out_ref, qb_ref, r_ref, acc_ref, z_ref, w_ref = refs[2 * PAGES_PER_STEP:]
    n = pl.program_id(1)

    @pl.when(n == 0)
    def _():
        q = q_ref[...] * QK_SCALE
        qb_ref[...] = jnp.broadcast_to(q, PAGE_SHAPE)
        if newest:
            valid = jnp.where(pos_ref[1] < pos_ref[0], 1.0, 0.0)
            lsp, lk = _log_sigmoid_pair(jnp.sum(q * kown_ref[...], axis=1))
            r_ref[...] = jnp.broadcast_to(lk * valid, HEAD_ROW)
            lane = lax.broadcasted_iota(jnp.int32, PAGE_SHAPE, 2)
            acc_ref[...] = jnp.where(lane == 0, _over_dims(jnp.exp(lsp) * valid) * vown_ref[...], 0.0)
        else:
            r_ref[...] = r_in_ref[...]
            acc_ref[...] = jnp.zeros_like(acc_ref)

    @pl.when(live_ref[pl.program_id(0)] != 0)
    def _():
        _score_pages(z_ref, qb_ref, k_refs)
        lsp, lk = _log_sigmoid_pair(z_ref[...])
        totals = _lane_sum(lk)
        r = r_ref[:, 0:1]
        later = [None] * PAGES_PER_STEP
        for j in reversed(range(PAGES_PER_STEP)):
            later[j] = r
            r = r + _page_rows(totals, j)
        r_ref[...] = jnp.broadcast_to(r, HEAD_ROW)
        w_ref[...] = jnp.exp(lsp + _suffix_sum_exclusive(lk) + jnp.concatenate(later, axis=0))
        for h in range(N_HEADS):
            acc_ref[h] += _weigh_values(w_ref, v_refs, range(PAGES_PER_STEP), h)

    @pl.when(n == pl.num_programs(1) - 1)
    def _():
        o_ref[...] = part_in_ref[...] + _lane_sum(acc_ref[...])
        r_out_ref[...] = r_ref[...]


def _decode_specs(n_steps, layer, newest_first):
    def page(j):
        def index_map(b, n, pt, *_):
            step = (n_steps - 1 - n) if newest_first else n
            return (pt[b, PAGES_PER_STEP * step + j], layer, 0, 0, 0)
        return pl.BlockSpec((None, None) + PAGE_SHAPE, index_map)

    col = pl.BlockSpec((None, N_HEADS, HEAD_DIM, 1), lambda b, n, *_: (b, 0, 0, 0))
    return col, [page(j) for j in range(PAGES_PER_STEP)]


def _sb_decode(q, k_own, v_own, cache_k, cache_v, page_table, layer, q_pos, k_own_pos):
    batch = q.shape[0]
    n_steps = page_table.shape[1] // PAGES_PER_STEP
    pos = jnp.array([q_pos, k_own_pos], jnp.int32)
    ck, cv = _page_view(cache_k), _page_view(cache_v)
    col = pl.BlockSpec((None, N_HEADS, HEAD_DIM, 1), lambda b, n, *_: (b, 0, 0, 0))
    row = pl.BlockSpec((None,) + HEAD_ROW, lambda b, n, *_: (b, 0, 0))
    cols = [_head_columns(t) for t in (q, k_own, v_own)]

    def walk(first_step, steps, live, r_in, part_in, newest):
        def page(j):
            def index_map(b, n, pt, pos, live):
                wanted = pt[b, PAGES_PER_STEP * (first_step - n) + j]
                return (jnp.where(live[b] != 0, wanted, pt[0, j]), layer, 0, 0, 0)
            return pl.BlockSpec((None, None) + PAGE_SHAPE, index_map)

        pages = [page(j) for j in range(PAGES_PER_STEP)]
        return pl.pallas_call(
            functools.partial(_sb_decode_kernel, newest=newest),
            out_shape=(jax.ShapeDtypeStruct((batch, N_HEADS, HEAD_DIM, 1), F32),
                       jax.ShapeDtypeStruct((batch,) + HEAD_ROW, F32)),
            grid_spec=pltpu.PrefetchScalarGridSpec(
                num_scalar_prefetch=3,
                grid=(batch, steps),
                in_specs=[col, col, col, row, col] + pages + pages,
                out_specs=(col, row),
                scratch_shapes=[pltpu.VMEM(PAGE_SHAPE, F32), pltpu.VMEM(HEAD_ROW, F32), pltpu.VMEM(PAGE_SHAPE, F32),
                                pltpu.VMEM(STEP_ROWS, F32), pltpu.VMEM(STEP_ROWS, F32)],
            ),
            compiler_params=_cparams("parallel", "arbitrary"),
            name="sb_decode",
        )(page_table, pos, live, *cols, r_in, part_in, *([ck] * PAGES_PER_STEP), *([cv] * PAGES_PER_STEP))

    zeros_r = jnp.zeros((batch,) + HEAD_ROW, F32)
    zeros_part = jnp.zeros((batch, N_HEADS, HEAD_DIM, 1), F32)
    part, r = walk(n_steps - 1, 1, jnp.ones((batch,), jnp.int32), zeros_r, zeros_part, True)
    if n_steps > 1:
        live = (jnp.max(r, axis=(1, 2)) > EXP_UNDERFLOWS_BELOW).astype(jnp.int32)
        part = lax.cond(jnp.any(live != 0),
                        lambda: walk(n_steps - 2, n_steps - 1, live, r, part, False)[0],
                        lambda: part)
    return part.reshape(batch, D_MODEL)


def _moba_decode_kernel(pt_ref, q_ref, kown_ref, vown_ref, *refs):
    k_refs, v_refs = refs[:PAGES_PER_STEP], refs[PAGES_PER_STEP:2 * PAGES_PER_STEP]
    o_ref, qb_ref, gate_ref, m_ref, l_ref, pv_ref, z_ref, w_ref = refs[2 * PAGES_PER_STEP:]
    n = pl.program_id(1)
    n_blocks = gate_ref.shape[0]

    @pl.when(n == 0)
    def _():
        qb_ref[...] = jnp.broadcast_to(q_ref[...] * QK_SCALE, PAGE_SHAPE)

    _score_pages(z_ref, qb_ref, k_refs)
    s = z_ref[...]
    page_max = jnp.max(s, axis=-1, keepdims=True)
    page_sum = _lane_sum(s)
    blocks = [range(b * PAGES_PER_BLOCK, (b + 1) * PAGES_PER_BLOCK) for b in range(PAGES_PER_STEP // PAGES_PER_BLOCK)]
    over_block = lambda x, pages, op: functools.reduce(op, [_page_rows(x, j) for j in pages])
    block_max = [over_block(page_max, pages, jnp.maximum) for pages in blocks]
    w_ref[...] = jnp.exp(s - jnp.concatenate([block_max[j // PAGES_PER_BLOCK] for j in range(PAGES_PER_STEP)], axis=0))
    weight_sum = _lane_sum(w_ref[...])
    for b, pages in enumerate(blocks):
        idx = n * len(blocks) + b
        gate_ref[idx] = jnp.broadcast_to(over_block(page_sum, pages, jnp.add) * (1.0 / MOBA_BLOCK), HEAD_ROW)
        m_ref[idx] = jnp.broadcast_to(block_max[b], HEAD_ROW)
        l_ref[idx] = jnp.broadcast_to(over_block(weight_sum, pages, jnp.add), HEAD_ROW)
        for h in range(N_HEADS):
            pv_ref[idx, h] = _weigh_values(w_ref, v_refs, pages, h)

    @pl.when(n == pl.num_programs(1) - 1)
    def _():
        shape = gate_ref.shape
        blk = lax.broadcasted_iota(jnp.int32, shape, 0)
        gate = gate_ref[...]
        sel = jnp.zeros(shape, jnp.bool_)
        for r in range(MOBA_TOPK):
            best = jnp.max(gate, axis=0, keepdims=True)
            idx = jnp.min(jnp.where(gate == best, blk, shape[0]), axis=0, keepdims=True)
            pick = blk == idx
            sel = sel | (pick & (r < n_blocks))
            gate = jnp.where(pick, -jnp.inf, gate)
        q = q_ref[...] * QK_SCALE
        s_own = jnp.broadcast_to(jnp.sum(q * kown_ref[...], axis=1), HEAD_ROW)
        m_all = m_ref[...]
        m_fin = jnp.maximum(jnp.max(jnp.where(sel, m_all, -jnp.inf), axis=0), s_own)
        w = jnp.where(sel, jnp.exp(m_all - m_fin[None]), 0.0)
        w_own = jnp.exp(s_own - m_fin)
        denom = jnp.sum(w * l_ref[...], axis=0) + w_own
        gate_ref[...] = w
        numer = lax.fori_loop(0, n_blocks, lambda i, acc: acc + _over_dims(gate_ref[i]) * pv_ref[i],
                              jnp.zeros(PAGE_SHAPE, F32))
        first_lane = lambda t: _over_dims(t)[:, :, 0:1]
        o_ref[...] = (_lane_sum(numer) + first_lane(w_own) * vown_ref[...]) / first_lane(denom)


def _moba_decode(q, k_own, v_own, cache_k, cache_v, page_table, layer):
    batch = q.shape[0]
    n_pages = page_table.shape[1]
    assert PAGES_PER_STEP % PAGES_PER_BLOCK == 0 and n_pages % PAGES_PER_STEP == 0
    n_blocks = n_pages // PAGES_PER_BLOCK
    col, pages = _decode_specs(n_pages // PAGES_PER_STEP, layer, False)
    ck, cv = _page_view(cache_k), _page_view(cache_v)
    stat = pltpu.VMEM((n_blocks,) + HEAD_ROW, F32)
    out = pl.pallas_call(
        _moba_decode_kernel,
        out_shape=jax.ShapeDtypeStruct((batch, N_HEADS, HEAD_DIM, 1), F32),
        grid_spec=pltpu.PrefetchScalarGridSpec(
            num_scalar_prefetch=1,
            grid=(batch, n_pages // PAGES_PER_STEP),
            in_specs=[col, col, col] + pages + pages,
            out_specs=col,
            scratch_shapes=[pltpu.VMEM(PAGE_SHAPE, F32), stat, stat, stat,
                            pltpu.VMEM((n_blocks,) + PAGE_SHAPE, F32),
                            pltpu.VMEM(STEP_ROWS, F32), pltpu.VMEM(STEP_ROWS, F32)],
        ),
        compiler_params=_cparams("parallel", "arbitrary"),
        name="moba_decode",
    )(page_table, _head_columns(q), _head_columns(k_own), _head_columns(v_own),
      *([ck] * PAGES_PER_STEP), *([cv] * PAGES_PER_STEP))
    return out.reshape(batch, D_MODEL)


def kernel(x_prompt, x_sample, cache_sb_k, cache_sb_v, cache_moba_k, cache_moba_v, state_ffn_conv, page_table, p_prompt, p_sample, mix_norm, ffn_norm, ple_norm, sgu_w_in, sgu_v_norm, sgu_w_s, sgu_b_s, sgu_w_out, sb_w_qkv, sb_w_out, moba_w_qkv, moba_q_norm, moba_k_norm, moba_w_out, ffn_w_in, ffn_conv_w, ffn_conv_b, ffn_w_out, ple_w_in, ple_w_gate):
    bp, tp, _ = x_prompt.shape
    bs, ts, _ = x_sample.shape
    assert ts == 1 and tp % (2 * SGU_CHUNK) == 0 and tp % MOBA_BLOCK == 0
    depth = mix_norm.shape[0]
    past_len = page_table.shape[1] * PAGE_SIZE
    assert past_len % MOBA_BLOCK == 0 and past_len % SGU_CHUNK == 0
    xp = x_prompt.reshape(bp * tp, D_MODEL)
    xs = x_sample.reshape(bs, D_MODEL)
    bf = lambda w: w.astype(BF16)
    heads_s = lambda t: t.reshape(bs, ts, N_HEADS, HEAD_DIM)

    sb_kp, sb_vp, sb_ks, sb_vs = [], [], [], []
    mb_kp, mb_vp, mb_ks, mb_vs = [], [], [], []
    sgu_vs, conv_p, conv_s = [], [], []

    for i in range(depth):
        kind, j = i % N_MIXERS, i // N_MIXERS
        if kind == 0:
            w_in, w_out = bf(sgu_w_in[j]), bf(sgu_w_out[j])
            zs = _norm_matmul(xs, mix_norm[i], w_in, 2, act="gelu")
            xp = _sgu_prompt(xp, mix_norm[i], w_in, sgu_v_norm[j], sgu_w_s[j], sgu_b_s[j], w_out)
            xs, v_s = _sgu_sample(zs, sgu_v_norm[j], sgu_w_s[j], sgu_b_s[j], w_out, xs)
            sgu_vs.append(v_s.reshape(bs, ts, SGU_WIDTH))
        elif kind == 1:
            w_qkv, w_out = bf(sb_w_qkv[j]), bf(sb_w_out[j])
            qkv_p, k_t, v_t = _qkv_prompt(xp, mix_norm[i], w_qkv, bp, tp, (1, 2))
            qkv_s = _norm_matmul(xs, mix_norm[i], w_qkv, 3)
            xp = _proj_residual(_sb_prompt(qkv_p, bp, tp), w_out, xp)
            att = _sb_decode(qkv_s[0], qkv_s[1], qkv_s[2], cache_sb_k, cache_sb_v,
                             page_table, j, past_len, past_len)
            xs = _proj_residual(att, w_out, xs)
            sb_kp.append(_prompt_cache_rows(k_t))
            sb_vp.append(_prompt_cache_rows(v_t))
            sb_ks.append(heads_s(qkv_s[1]))
            sb_vs.append(heads_s(qkv_s[2]))
        else:
            w_qkv, w_out = bf(moba_w_qkv[j]), bf(moba_w_out[j])
            qkv_p, v_t = _qkv_prompt(xp, mix_norm[i], w_qkv, bp, tp, (2,))
            qkv_s = _norm_matmul(xs, mix_norm[i], w_qkv, 3)
            qk_p, k_t = _qknorm_rope(qkv_p, moba_q_norm[j], moba_k_norm[j], jnp.arange(tp, dtype=jnp.int32), bp)
            qk_s = _qknorm_rope(qkv_s, moba_q_norm[j], moba_k_norm[j], jnp.full((bs,), past_len, jnp.int32))
            xp = _proj_residual(_moba_prompt(qk_p, qkv_p, bp, tp), w_out, xp)
            att = _moba_decode(qk_s[0], qk_s[1], qkv_s[2], cache_moba_k, cache_moba_v, page_table, j)
            xs = _proj_residual(att, w_out, xs)
            mb_kp.append(_prompt_cache_rows(k_t))
            mb_vp.append(_prompt_cache_rows(v_t))
            mb_ks.append(heads_s(qk_s[1]))
            mb_vs.append(heads_s(qkv_s[2]))

        w_in, w_out = bf(ffn_w_in[i]), bf(ffn_w_out[i])
        w_gate, w_pin = bf(ple_w_gate[i]), bf(ple_w_in[i])
        gv_s = _norm_matmul(xs, ffn_norm[i], w_in, 2)
        xp, g_tails = _ffn_prompt(xp, ffn_norm[i], w_in, ffn_conv_w[i], ffn_conv_b[i], w_out, tp,
                                  p_prompt.reshape(depth, bp * tp, -1), i, ple_norm[i], w_gate, w_pin)
        xs = _ffn_sample(gv_s, state_ffn_conv[i], ffn_conv_w[i], ffn_conv_b[i], w_out, xs)
        conv_p.append(g_tails.reshape(bp, -1, SUBLANES, D_FF)[:, -1, SUBLANES - (CONV_W - 1):])
        conv_s.append(jnp.concatenate([state_ffn_conv[i][:, 1:], gv_s[0][:, None]], axis=1))
        xs = _ple(xs, p_sample.reshape(depth, bs, -1), i, ple_norm[i], w_gate, w_pin)

    return (xp.reshape(bp, tp, D_MODEL), xs.reshape(bs, ts, D_MODEL),
            jnp.concatenate(sb_kp, axis=1), jnp.concatenate(sb_vp, axis=1),
            jnp.stack(sb_ks, axis=1), jnp.stack(sb_vs, axis=1),
            jnp.concatenate(mb_kp, axis=1), jnp.concatenate(mb_vp, axis=1),
            jnp.stack(mb_ks, axis=1), jnp.stack(mb_vs, axis=1),
            jnp.stack(sgu_vs, axis=0), jnp.stack(conv_p, axis=0), jnp.stack(conv_s, axis=0))
```

```python
import functools

import jax
import jax.numpy as jnp
from jax import lax
from jax.experimental import pallas as pl
from jax.experimental.pallas import tpu as pltpu

F32 = jnp.float32
BF16 = jnp.bfloat16

D_MODEL = 1024
N_HEADS = 16
HEAD_DIM = 64
PAGE_SIZE = 128
SGU_CHUNK = 128
SGU_GROUPS = 8
SGU_WIDTH = 2 * D_MODEL
SGU_GROUP_DIM = SGU_WIDTH // SGU_GROUPS
MOBA_BLOCK = 256
MOBA_TOPK = 3
ROT_DIM = HEAD_DIM // 4
ROPE_THETA = 500000.0
D_FF = 2816
CONV_W = 3
EPS = 1e-6
N_MIXERS = 3

LANES = 128
HEADS_PER_LANE_TILE = LANES // HEAD_DIM
N_HEAD_PAIRS = D_MODEL // LANES
QK_SCALE = HEAD_DIM ** -0.5
MASK_BIAS = -1e30
EXP_UNDERFLOWS_BELOW = -104.0
VMEM_LIMIT = 48 * 1024 * 1024


def _cparams(*sem):
    return pltpu.CompilerParams(dimension_semantics=sem, vmem_limit_bytes=VMEM_LIMIT)


def _rms(x, g):
    return x * lax.rsqrt(jnp.mean(x * x, axis=-1, keepdims=True) + EPS) * g


def _dot(a, b):
    return jnp.dot(a, b, preferred_element_type=F32)


def _dot_nt(a, b):
    return lax.dot_general(a, b, (((1,), (1,)), ((), ())), preferred_element_type=F32)


def _split_bf16(x):
    hi = x.astype(BF16)
    lo = (x - hi.astype(F32)).astype(BF16)
    return jnp.concatenate([hi, lo], axis=1)


def _strict_upper_sum_matrix(n):
    j = lax.broadcasted_iota(jnp.int32, (n, n + LANES), 0)
    s = lax.broadcasted_iota(jnp.int32, (n, n + LANES), 1)
    return jnp.where((j > s) | (s >= n), 1.0, 0.0).astype(BF16)


def _log_sigmoid_pair(z):
    lsp = jnp.minimum(z, 0.0) - jnp.log(1.0 + jnp.exp(-jnp.abs(z)))
    return lsp, lsp - z


def _norm_matmul_kernel(x_ref, g_ref, w_ref, o_ref, xn_ref, *, act):
    @pl.when(pl.program_id(1) == 0)
    def _():
        xn_ref[...] = _rms(x_ref[...], g_ref[...]).astype(BF16)

    y = _dot(xn_ref[...], w_ref[...])
    if act == "gelu":
        y = jax.nn.gelu(y)
    o_ref[...] = y


def _lane_tile(n, cap):
    return max(t for t in range(LANES, cap + 1, LANES) if n % t == 0)


def _norm_matmul(x, gain, w, n_split, act=None, tm=512, tn_cap=1536):
    m, k = x.shape
    n_out = w.shape[1]
    per = n_out // n_split
    tm = min(tm, m)
    tn = _lane_tile(per, tn_cap)
    nj = per // tn
    assert m % tm == 0 and per % tn == 0
    return pl.pallas_call(
        functools.partial(_norm_matmul_kernel, act=act),
        out_shape=jax.ShapeDtypeStruct((n_split, m, per), F32),
        grid=(m // tm, n_out // tn),
        in_specs=[
            pl.BlockSpec((tm, k), lambda i, j: (i, 0)),
            pl.BlockSpec((1, k), lambda i, j: (0, 0)),
            pl.BlockSpec((k, tn), lambda i, j: (0, j)),
        ],
        out_specs=pl.BlockSpec((None, tm, tn), lambda i, j: (j // nj, i, j % nj)),
        scratch_shapes=[pltpu.VMEM((tm, k), BF16)],
        compiler_params=_cparams("parallel", "arbitrary"),
        name="norm_matmul",
    )(x, gain.reshape(1, k), w)


def _qkv_prompt_kernel(x_ref, g_ref, w_ref, o_ref, *refs, transposed):
    t_refs, xn_ref = refs[:-1], refs[-1]
    j = pl.program_id(1)

    @pl.when(j == 0)
    def _():
        xn_ref[...] = _rms(x_ref[...], g_ref[...]).astype(BF16)

    y = _dot(xn_ref[...], w_ref[...])
    o_ref[...] = y
    for split, t_ref in zip(transposed, t_refs):
        @pl.when(j == split)
        def _():
            t_ref[...] = y.T


def _qkv_prompt(x, gain, w, batch, seq_len, transposed, tm=512):
    m, k = x.shape
    tiles = seq_len // tm
    t_shape = jax.ShapeDtypeStruct((batch, D_MODEL, seq_len), F32)
    t_spec = pl.BlockSpec((None, D_MODEL, tm), lambda i, j: (i // tiles, 0, i % tiles))
    return pl.pallas_call(
        functools.partial(_qkv_prompt_kernel, transposed=transposed),
        out_shape=(jax.ShapeDtypeStruct((3, m, D_MODEL), F32),) + (t_shape,) * len(transposed),
        grid=(m // tm, 3),
        in_specs=[
            pl.BlockSpec((tm, k), lambda i, j: (i, 0)),
            pl.BlockSpec((1, k), lambda i, j: (0, 0)),
            pl.BlockSpec((k, D_MODEL), lambda i, j: (0, j)),
        ],
        out_specs=(pl.BlockSpec((None, tm, D_MODEL), lambda i, j: (j, i, 0)),) + (t_spec,) * len(transposed),
        scratch_shapes=[pltpu.VMEM((tm, k), BF16)],
        compiler_params=_cparams("parallel", "arbitrary"),
        name="qkv_prompt",
    )(x, gain.reshape(1, k), w)


def _proj_residual_kernel(h_ref, w_ref, x_ref, o_ref):
    o_ref[...] = x_ref[...] + _dot(h_ref[...].astype(BF16), w_ref[...])


def _proj_residual(h, w, x, tm=1024):
    m, k = h.shape
    n = w.shape[1]
    tm = min(tm, m)
    return pl.pallas_call(
        _proj_residual_kernel,
        out_shape=jax.ShapeDtypeStruct((m, n), F32),
        grid=(m // tm,),
        in_specs=[
            pl.BlockSpec((tm, k), lambda i: (i, 0)),
            pl.BlockSpec((k, n), lambda i: (0, 0)),
            pl.BlockSpec((tm, n), lambda i: (i, 0)),
        ],
        out_specs=pl.BlockSpec((tm, n), lambda i: (i, 0)),
        compiler_params=_cparams("parallel"),
        name="proj_residual",
    )(h, w, x)


def _ple_kernel(x_ref, p_ref, g_ref, wg_ref, wp_ref, o_ref):
    x = x_ref[...]
    gate = jax.nn.sigmoid(_dot(_rms(x, g_ref[...]).astype(BF16), wg_ref[...]))
    o_ref[...] = x + gate * _dot(p_ref[...].astype(BF16), wp_ref[...])


def _ple(x, p_layers, layer, gain, w_gate, w_in, tm=512):
    m, d = x.shape
    pd = p_layers.shape[2]
    tm = min(tm, m)
    return pl.pallas_call(
        _ple_kernel,
        out_shape=jax.ShapeDtypeStruct((m, d), F32),
        grid=(m // tm,),
        in_specs=[
            pl.BlockSpec((tm, d), lambda i: (i, 0)),
            pl.BlockSpec((None, tm, pd), lambda i: (layer, i, 0)),
            pl.BlockSpec((1, d), lambda i: (0, 0)),
            pl.BlockSpec((d, d), lambda i: (0, 0)),
            pl.BlockSpec((pd, d), lambda i: (0, 0)),
        ],
        out_specs=pl.BlockSpec((tm, d), lambda i: (i, 0)),
        compiler_params=_cparams("parallel"),
        name="ple",
    )(x, p_layers, gain.reshape(1, d), w_gate, w_in)


def _sgu_prompt_kernel(x_ref, gain_ref, win_ref, vg_ref, ws_ref, bst_ref, wo_ref, o_ref, *, chunks):
    x = x_ref[...]
    xn = _rms(x, gain_ref[...]).astype(BF16)
    v = _rms(jax.nn.gelu(_dot(xn, win_ref[:, SGU_WIDTH:])), vg_ref[...]).astype(BF16)
    i = lax.broadcasted_iota(jnp.int32, (SGU_CHUNK, SGU_CHUNK), 0)
    j = lax.broadcasted_iota(jnp.int32, (SGU_CHUNK, SGU_CHUNK), 1)
    gated = []
    for g in range(SGU_GROUPS):
        cols = slice(g * SGU_GROUP_DIM, (g + 1) * SGU_GROUP_DIM)
        w = jnp.where(i >= j, ws_ref[g], 0.0).astype(BF16)
        bias = bst_ref[:, g:g + 1]
        mixed = jnp.concatenate(
            [_dot(w, v[c * SGU_CHUNK:(c + 1) * SGU_CHUNK, cols]) + bias for c in range(chunks)], axis=0)
        u = jax.nn.gelu(_dot(xn, win_ref[:, cols]))
        gated.append((u * mixed).astype(BF16))
    o_ref[...] = x + _dot(jnp.concatenate(gated, axis=1), wo_ref[...])


def _sgu_prompt(x, gain, w_in, v_gain, w_s, b_s, w_out, chunks=8):
    m = x.shape[0]
    tm = chunks * SGU_CHUNK
    return pl.pallas_call(
        functools.partial(_sgu_prompt_kernel, chunks=chunks),
        out_shape=jax.ShapeDtypeStruct((m, D_MODEL), F32),
        grid=(m // tm,),
        in_specs=[
            pl.BlockSpec((tm, D_MODEL), lambda i: (i, 0)),
            _resident((1, D_MODEL)),
            _resident((D_MODEL, 2 * SGU_WIDTH)),
            _resident((1, SGU_WIDTH)),
            _resident((SGU_GROUPS, SGU_CHUNK, SGU_CHUNK)),
            _resident((SGU_CHUNK, SGU_GROUPS)),
            _resident((SGU_WIDTH, D_MODEL)),
        ],
        out_specs=pl.BlockSpec((tm, D_MODEL), lambda i: (i, 0)),
        compiler_params=_cparams("parallel"),
        name="sgu_prompt",
    )(x, gain.reshape(1, D_MODEL), w_in, v_gain.reshape(1, SGU_WIDTH), w_s, b_s.T, w_out)


def _sgu_sample_kernel(u_ref, v_ref, vg_ref, w0_ref, b0_ref, wo_ref, x_ref, o_ref, vn_ref):
    v = _rms(v_ref[...], vg_ref[...])
    vn_ref[...] = v
    gated = u_ref[...] * (w0_ref[...] * v.astype(BF16).astype(F32) + b0_ref[...])
    o_ref[...] = x_ref[...] + _dot(gated.astype(BF16), wo_ref[...])


def _sgu_sample(z2, v_gain, w_s, b_s, w_out, x):
    m = x.shape[0]
    w0 = jnp.repeat(w_s[:, 0, 0].astype(BF16).astype(F32), SGU_GROUP_DIM).reshape(1, SGU_WIDTH)
    b0 = jnp.repeat(b_s[:, 0], SGU_GROUP_DIM).reshape(1, SGU_WIDTH)
    row = lambda i: (0, 0)
    return pl.pallas_call(
        _sgu_sample_kernel,
        out_shape=(jax.ShapeDtypeStruct((m, D_MODEL), F32), jax.ShapeDtypeStruct((m, SGU_WIDTH), F32)),
        grid=(1,),
        in_specs=[
            pl.BlockSpec((None, m, SGU_WIDTH), lambda i: (0, 0, 0)),
            pl.BlockSpec((None, m, SGU_WIDTH), lambda i: (1, 0, 0)),
            pl.BlockSpec((1, SGU_WIDTH), row),
            pl.BlockSpec((1, SGU_WIDTH), row),
            pl.BlockSpec((1, SGU_WIDTH), row),
            pl.BlockSpec((SGU_WIDTH, D_MODEL), row),
            pl.BlockSpec((m, D_MODEL), row),
        ],
        out_specs=(pl.BlockSpec((m, D_MODEL), row), pl.BlockSpec((m, SGU_WIDTH), row)),
        compiler_params=_cparams("arbitrary"),
        name="sgu_sample",
    )(z2, z2, v_gain.reshape(1, SGU_WIDTH), w0, b0, w_out, x)


def _ffn_gate(c, val):
    return (jax.nn.silu(c) * val).astype(BF16)


SUBLANES = 8
FFN_CHUNK = 256


def _resident(shape):
    return pl.BlockSpec(shape, lambda i: (0,) * len(shape), pipeline_mode=pl.Buffered(1))


def _ffn_prompt_kernel(x_ref, gain_ref, win_ref, cw_ref, cb_ref, wo_ref, p_ref, pgain_ref, wg_ref, wp_ref,
                       o_ref, tail_ref, prev_ref, *, tiles_per_seq):
    x = x_ref[...]
    tm = x.shape[0]
    xn = _rms(x, gain_ref[...]).astype(BF16)
    @pl.when(pl.program_id(0) % tiles_per_seq == 0)
    def _():
        prev_ref[...] = jnp.zeros_like(prev_ref)

    row = lax.broadcasted_iota(jnp.int32, (tm, 1), 0)
    hidden = []
    for c0 in range(0, D_FF, FFN_CHUNK):
        cols = slice(c0, c0 + FFN_CHUNK)
        g = _dot(xn, win_ref[:, cols])
        val = _dot(xn, win_ref[:, D_FF + c0:D_FF + c0 + FFN_CHUNK])
        prev = prev_ref[:, cols]
        g1 = jnp.where(row == 0, prev[7:8], pltpu.roll(g, 1, axis=0))
        g2 = jnp.where(row == 0, prev[6:7], jnp.where(row == 1, prev[7:8], pltpu.roll(g, 2, axis=0)))
        c = cb_ref[:, cols] + cw_ref[0:1, cols] * g2 + cw_ref[1:2, cols] * g1 + cw_ref[2:3, cols] * g
        hidden.append(_ffn_gate(c, val))
        prev_ref[:, cols] = g[tm - SUBLANES:, :]
        tail_ref[:, cols] = g[tm - SUBLANES:, :]
    acc = x + _dot(jnp.concatenate(hidden, axis=1), wo_ref[...])
    gate = jax.nn.sigmoid(_dot(_rms(acc, pgain_ref[...]).astype(BF16), wg_ref[...]))
    o_ref[...] = acc + gate * _dot(p_ref[...].astype(BF16), wp_ref[...])


def _ffn_prompt(x, gain, w_in, conv_w, conv_b, w_out, seq_len, p_layers, layer, ple_gain, w_gate, w_pin, tm=1024):
    m = x.shape[0]
    pd = p_layers.shape[2]
    assert seq_len % tm == 0 and D_FF % FFN_CHUNK == 0
    return pl.pallas_call(
        functools.partial(_ffn_prompt_kernel, tiles_per_seq=seq_len // tm),
        out_shape=(jax.ShapeDtypeStruct((m, D_MODEL), F32),
                   jax.ShapeDtypeStruct((m // tm, SUBLANES, D_FF), F32)),
        grid=(m // tm,),
        in_specs=[
            pl.BlockSpec((tm, D_MODEL), lambda i: (i, 0)),
            _resident((1, D_MODEL)),
            _resident((D_MODEL, 2 * D_FF)),
            _resident((CONV_W, D_FF)),
            _resident((1, D_FF)),
            _resident((D_FF, D_MODEL)),
            pl.BlockSpec((None, tm, pd), lambda i: (layer, i, 0)),
            _resident((1, D_MODEL)),
            _resident((D_MODEL, D_MODEL)),
            _resident((pd, D_MODEL)),
        ],
        out_specs=(pl.BlockSpec((tm, D_MODEL), lambda i: (i, 0)),
                   pl.BlockSpec((None, SUBLANES, D_FF), lambda i: (i, 0, 0))),
        scratch_shapes=[pltpu.VMEM((SUBLANES, D_FF), F32)],
        compiler_params=_cparams("arbitrary"),
        name="ffn_prompt",
    )(x, gain.reshape(1, D_MODEL), w_in, conv_w, conv_b.reshape(1, D_FF), w_out,
      p_layers, ple_gain.reshape(1, D_MODEL), w_gate, w_pin)


def _ffn_sample_kernel(g_ref, val_ref, s0_ref, s1_ref, cw_ref, cb_ref, wo_ref, x_ref, o_ref):
    c = cb_ref[...] + cw_ref[0:1] * s0_ref[...] + cw_ref[1:2] * s1_ref[...] + cw_ref[2:3] * g_ref[...]
    o_ref[...] = x_ref[...] + _dot(_ffn_gate(c, val_ref[...]), wo_ref[...])


def _ffn_sample(gv2, state, conv_w, conv_b, w_out, x):
    m = x.shape[0]
    full = lambda i: (0, 0)
    return pl.pallas_call(
        _ffn_sample_kernel,
        out_shape=jax.ShapeDtypeStruct((m, D_MODEL), F32),
        grid=(1,),
        in_specs=[
            pl.BlockSpec((None, m, D_FF), lambda i: (0, 0, 0)),
            pl.BlockSpec((None, m, D_FF), lambda i: (1, 0, 0)),
            pl.BlockSpec((m, D_FF), full),
            pl.BlockSpec((m, D_FF), full),
            pl.BlockSpec((CONV_W, D_FF), full),
            pl.BlockSpec((1, D_FF), full),
            pl.BlockSpec((D_FF, D_MODEL), full),
            pl.BlockSpec((m, D_MODEL), full),
        ],
        out_specs=pl.BlockSpec((m, D_MODEL), full),
        compiler_params=_cparams("arbitrary"),
        name="ffn_sample",
    )(gv2, gv2, state[:, 0], state[:, 1], conv_w, conv_b.reshape(1, D_FF), w_out, x)


def _qknorm_rope_kernel(x_ref, gain_ref, c_ref, s1_ref, s2_ref, bd_ref, o_ref, *maybe_kt_ref):
    x = x_ref[...]
    head_sumsq = _dot(_split_bf16(x * x), bd_ref[...])
    y = x * lax.rsqrt(head_sumsq * (1.0 / HEAD_DIM) + EPS) * gain_ref[...]
    reps = D_MODEL // LANES
    tile = lambda t: jnp.concatenate([t] * reps, axis=1)
    half = ROT_DIM // 2
    out = (y * tile(c_ref[...])
           + pltpu.roll(y, D_MODEL - half, axis=1) * tile(s1_ref[...])
           + pltpu.roll(y, half, axis=1) * tile(s2_ref[...]))
    o_ref[...] = out
    for kt_ref in maybe_kt_ref:
        @pl.when(pl.program_id(1) == 1)
        def _():
            kt_ref[...] = out.T


def _rope_tables(pos):
    half = ROT_DIM // 2
    inv_freq = ROPE_THETA ** (-jnp.arange(0, ROT_DIM, 2, dtype=F32) / ROT_DIM)
    ang = pos.astype(F32)[:, None] * inv_freq[None, :]
    cos, sin = jnp.cos(ang), jnp.sin(ang)
    n = pos.shape[0]
    pad = jnp.zeros((n, HEAD_DIM - ROT_DIM), F32)
    zeros = jnp.zeros((n, half), F32)
    c = jnp.concatenate([cos, cos, pad + 1.0], axis=1)
    s1 = jnp.concatenate([-sin, zeros, pad], axis=1)
    s2 = jnp.concatenate([zeros, sin, pad], axis=1)
    rep = lambda t: jnp.concatenate([t] * HEADS_PER_LANE_TILE, axis=1)
    return rep(c), rep(s1), rep(s2)


def _qknorm_rope(qkv3, q_gain, k_gain, pos, batch=None, tm=512):
    m = qkv3.shape[1]
    t = pos.shape[0]
    tm = min(tm, m, t)
    tiles_per_seq = t // tm
    gains = jnp.stack([jnp.tile(q_gain, N_HEADS), jnp.tile(k_gain, N_HEADS)]).reshape(2, 1, D_MODEL)
    c, s1, s2 = _rope_tables(pos)
    head = jnp.arange(D_MODEL) // HEAD_DIM
    bd = (head[:, None] == head[None, :]).astype(BF16)
    bd2 = jnp.concatenate([bd, bd], axis=0)
    tab = pl.BlockSpec((tm, LANES), lambda i, s: (i % tiles_per_seq, 0))
    out_shape = [jax.ShapeDtypeStruct((2, m, D_MODEL), F32)]
    out_specs = [pl.BlockSpec((None, tm, D_MODEL), lambda i, s: (s, i, 0))]
    if batch is not None:
        out_shape.append(jax.ShapeDtypeStruct((batch, D_MODEL, t), F32))
        out_specs.append(pl.BlockSpec((None, D_MODEL, tm), lambda i, s: (i // tiles_per_seq, 0, i % tiles_per_seq)))
    out = pl.pallas_call(
        _qknorm_rope_kernel,
        out_shape=tuple(out_shape),
        grid=(m // tm, 2),
        in_specs=[
            pl.BlockSpec((None, tm, D_MODEL), lambda i, s: (s, i, 0)),
            pl.BlockSpec((None, 1, D_MODEL), lambda i, s: (s, 0, 0)),
            tab, tab, tab,
            pl.BlockSpec((2 * D_MODEL, D_MODEL), lambda i, s: (0, 0)),
        ],
        out_specs=tuple(out_specs),
        compiler_params=_cparams("parallel", "arbitrary"),
        name="qknorm_rope",
    )(qkv3, gains, c, s1, s2, bd2)
    return out if batch is not None else out[0]


def _prompt_cache_rows(x_t):
    batch, _, seq_len = x_t.shape
    return jnp.transpose(x_t.reshape(batch, 1, N_HEADS, HEAD_DIM, seq_len), (0, 1, 4, 2, 3))


def _head_lane_masks():
    lane = lax.broadcasted_iota(jnp.int32, (1, LANES), 1)
    return [(lane // HEAD_DIM) == h for h in range(HEADS_PER_LANE_TILE)]


def _stack_heads(q):
    return jnp.concatenate([jnp.where(hm, q, 0.0) for hm in _head_lane_masks()], axis=0).astype(BF16)


def _unstack_heads(acc, tq):
    return sum(jnp.where(hm, acc[h * tq:(h + 1) * tq], 0.0) for h, hm in enumerate(_head_lane_masks()))


def _sb_prompt_kernel(q_ref, k_ref, v_ref, o_ref, r_ref, acc_ref, *, tq, tk):
    qi = pl.program_id(2)
    qs = _stack_heads(q_ref[...] * QK_SCALE)
    rows = qs.shape[0]
    u = _strict_upper_sum_matrix(tk)
    q_pos = lax.broadcasted_iota(jnp.int32, (rows, tq), 0) & (tq - 1)
    k_pos = lax.broadcasted_iota(jnp.int32, (rows, tq), 1)
    r_ref[...] = jnp.zeros_like(r_ref)
    acc_ref[...] = jnp.zeros_like(acc_ref)

    def chunk(start, n_keys, diagonal):
        lsp, lk = _log_sigmoid_pair(_dot_nt(qs, k_ref[pl.ds(start, n_keys), :].astype(BF16)))
        if diagonal:
            causal = k_pos < q_pos
            lsp, lk = jnp.where(causal, lsp, MASK_BIAS), jnp.where(causal, lk, 0.0)
        r = r_ref[...]
        weights = []
        for j in reversed(range(n_keys // tk)):
            cols = slice(j * tk, (j + 1) * tk)
            sums = _dot(lk[:, cols].astype(BF16), u)
            later = sums[:, :tk] + jnp.concatenate([r] * (tk // LANES), axis=1)
            weights.append(jnp.exp(lsp[:, cols] + later).astype(BF16))
            r = r + sums[:, tk:]
        r_ref[...] = r
        a = jnp.concatenate(weights[::-1], axis=1)
        acc_ref[...] += _dot(a, v_ref[pl.ds(start, n_keys), :].astype(BF16))
        return jnp.max(r)

    def older(state):
        c, _ = state
        return c - 1, chunk(pl.multiple_of(c * tk, tk), tk, False)

    lax.while_loop(lambda state: (state[0] >= 0) & (state[1] > EXP_UNDERFLOWS_BELOW), older,
                   (qi * (tq // tk) - 1, chunk(pl.multiple_of(qi * tq, tq), tq, True)))
    o_ref[...] = _unstack_heads(acc_ref[...], tq)


def _attention_specs(batch, seq_len, tq, q_sel, k_sel, v_sel):
    nq = seq_len // tq
    return dict(
        grid=(batch, N_HEAD_PAIRS, nq),
        in_specs=[
            pl.BlockSpec((None, tq, LANES), lambda b, hp, qi: (q_sel, b * nq + qi, hp)),
            pl.BlockSpec((None, seq_len, LANES), lambda b, hp, qi: (k_sel, b, hp)),
            pl.BlockSpec((None, seq_len, LANES), lambda b, hp, qi: (v_sel, b, hp)),
        ],
        out_specs=pl.BlockSpec((tq, LANES), lambda b, hp, qi: (b * nq + qi, hp)),
        out_shape=jax.ShapeDtypeStruct((batch * seq_len, D_MODEL), F32),
    )


def _sb_prompt(qkv3, batch, seq_len, tq=512, tk=256):
    tq = min(tq, seq_len)
    assert tq % tk == 0 and tq & (tq - 1) == 0
    return pl.pallas_call(
        functools.partial(_sb_prompt_kernel, tq=tq, tk=tk),
        scratch_shapes=[pltpu.VMEM((HEADS_PER_LANE_TILE * tq, LANES), F32)] * 2,
        compiler_params=_cparams("parallel", "parallel", "arbitrary"),
        name="sb_prompt",
        **_attention_specs(batch, seq_len, tq, 0, 1, 2),
    )(qkv3, qkv3, qkv3)


def _moba_prompt_kernel(q_ref, k_ref, v_ref, o_ref, kmean_ref, m_ref, l_ref, acc_ref, s0_ref, s1_ref, *,
                        n_blocks, group):
    qi = pl.program_id(2)
    blk = MOBA_BLOCK

    @pl.when(qi == 0)
    def _():
        kmean_ref[...] = jnp.zeros_like(kmean_ref)
        for n in range(n_blocks):
            kmean_ref[n:n + 1, :] = jnp.mean(k_ref[n * blk:(n + 1) * blk, :], axis=0, keepdims=True)

    qs = _stack_heads(q_ref[...] * QK_SCALE)
    rows = qs.shape[0]
    neg_inf = -jnp.inf
    gate_rows = -(-n_blocks // SUBLANES) * SUBLANES
    block = lax.broadcasted_iota(jnp.int32, (gate_rows, rows), 0)
    gate = jnp.where(block < qi, _dot_nt(kmean_ref[:gate_rows, :].astype(BF16), qs), neg_inf)
    sel = jnp.zeros((gate_rows, rows), jnp.bool_)
    for r in range(MOBA_TOPK):
        best = jnp.max(gate, axis=0, keepdims=True)
        idx = jnp.min(jnp.where(gate == best, block, LANES), axis=0, keepdims=True)
        pick = block == idx
        sel = sel | (pick & (r < qi))
        gate = jnp.where(pick, neg_inf, gate)
    bias = jnp.concatenate([jnp.where(sel, 0.0, MASK_BIAS),
                            jnp.full((LANES - gate_rows, rows), MASK_BIAS, F32)], axis=0)
    q_aug = jnp.concatenate([qs, bias.T.astype(BF16)], axis=1)

    q_pos = lax.broadcasted_iota(jnp.int32, (rows, blk), 0) & (blk - 1)
    k_pos = lax.broadcasted_iota(jnp.int32, (rows, blk), 1)
    start = pl.multiple_of(qi * blk, blk)
    def fold(scores, m, first_key):
        m_new = m
        for s in scores:
            m_new = jnp.maximum(m_new, jnp.max(s, axis=-1, keepdims=True))
        shift = jnp.concatenate([m_new] * (blk // LANES), axis=1)
        p = jnp.concatenate([jnp.exp(s - shift).astype(BF16) for s in scores], axis=1)
        n_keys = len(scores) * blk
        v = v_ref[pl.ds(pl.multiple_of(first_key, blk), n_keys), :].astype(BF16)
        pv = _dot(p, jnp.concatenate([v, jnp.ones_like(v)], axis=1))
        return m_new, pv[:, :LANES], pv[:, LANES:]

    s = jnp.where(k_pos <= q_pos, _dot_nt(qs, k_ref[pl.ds(start, blk), :].astype(BF16)), neg_inf)
    m_ref[...], acc_ref[...], l_ref[...] = fold([s], jnp.full((rows, LANES), neg_inf, F32), start)

    k_lane = lax.broadcasted_iota(jnp.int32, (blk, LANES), 1)

    def score_group(i, s_ref):
        for j in range(group):
            keys = k_ref[pl.ds(pl.multiple_of((group * i + j) * blk, blk), blk), :].astype(BF16)
            one_hot = jnp.where(k_lane == group * i + j, 1.0, 0.0).astype(BF16)
            s_ref[j] = _dot_nt(q_aug, jnp.concatenate([keys, one_hot], axis=1))

    def fold_group(i, s_ref):
        m = m_ref[...]
        m_new, pv, p_sum = fold([s_ref[j] for j in range(group)], m, i * (group * blk))
        alpha = jnp.exp(m - m_new)
        m_ref[...] = m_new
        l_ref[...] = alpha * l_ref[...] + p_sum
        acc_ref[...] = alpha * acc_ref[...] + pv

    last_group = n_blocks // group - 1

    def group_pair(p, carry):
        score_group(2 * p + 1, s1_ref)
        fold_group(2 * p, s0_ref)
        score_group(jnp.minimum(2 * p + 2, last_group), s0_ref)
        fold_group(2 * p + 1, s1_ref)
        return carry

    score_group(0, s0_ref)
    lax.fori_loop(0, (qi + 2 * group - 1) // (2 * group), group_pair, 0)
    o_ref[...] = _unstack_heads(acc_ref[...] / l_ref[...], blk)


def _moba_prompt(qk2, qkv3, batch, seq_len, group=2):
    n_blocks = seq_len // MOBA_BLOCK
    assert n_blocks <= LANES and n_blocks % (2 * group) == 0
    specs = _attention_specs(batch, seq_len, MOBA_BLOCK, 0, 1, 2)
    rows = HEADS_PER_LANE_TILE * MOBA_BLOCK
    return pl.pallas_call(
        functools.partial(_moba_prompt_kernel, n_blocks=n_blocks, group=group),
        scratch_shapes=[pltpu.VMEM((LANES, LANES), F32)] + [pltpu.VMEM((rows, LANES), F32)] * 3
                       + [pltpu.VMEM((group, rows, MOBA_BLOCK), F32)] * 2,
        compiler_params=_cparams("parallel", "arbitrary", "arbitrary"),
        name="moba_prompt",
        **specs,
    )(qk2, qk2, qkv3)


PAGES_PER_STEP = 8
PAGES_PER_BLOCK = MOBA_BLOCK // PAGE_SIZE


PAGE_SHAPE = (N_HEADS, HEAD_DIM, PAGE_SIZE)
HEAD_ROW = (N_HEADS, PAGE_SIZE)
STEP_ROWS = (PAGES_PER_STEP * N_HEADS, PAGE_SIZE)


def _page_view(cache):
    return jnp.transpose(cache, (0, 1, 3, 4, 2))


def _head_columns(x):
    return x.reshape(x.shape[0], N_HEADS, HEAD_DIM, 1)


def _score_pages(z_ref, qb_ref, k_refs):
    for j, k_ref in enumerate(k_refs):
        for h in range(N_HEADS):
            row = j * N_HEADS + h
            z_ref[row:row + 1, :] = jnp.sum(k_ref[h] * qb_ref[h], axis=0, keepdims=True)


def _weigh_values(w_ref, v_refs, pages, h):
    return sum(w_ref[j * N_HEADS + h:j * N_HEADS + h + 1, :] * v_refs[j][h] for j in pages)


def _page_rows(x, j):
    return x[j * N_HEADS:(j + 1) * N_HEADS]


def _over_dims(x):
    return x[:, None, :]


def _lane_sum(x):
    return jnp.sum(x, axis=-1, keepdims=True)


def _suffix_sum_exclusive(x):
    lane = lax.broadcasted_iota(jnp.int32, x.shape, x.ndim - 1)
    inc = x
    d = 1
    while d < LANES:
        inc = inc + jnp.where(lane + d < LANES, pltpu.roll(inc, LANES - d, axis=x.ndim - 1), 0.0)
        d *= 2
    return inc - x


def _sb_decode_kernel(pt_ref, pos_ref, live_ref, q_ref, kown_ref, vown_ref, r_in_ref, part_in_ref, *refs,
                      newest):
    k_refs, v_refs = refs[:PAGES_PER_STEP], refs[PAGES_PER_STEP:2 * PAGES_PER_STEP]
    o_ref, r_out_ref, qb_ref, r_ref, acc_ref, z_ref, w_ref = refs[2 * PAGES_PER_STEP:]
    n = pl.program_id(1)

    @pl.when(n == 0)
    def _():
        q = q_ref[...] * QK_SCALE
        qb_ref[...] = jnp.broadcast_to(q, PAGE_SHAPE)
        if newest:
            valid = jnp.where(pos_ref[1] < pos_ref[0], 1.0, 0.0)
            lsp, lk = _log_sigmoid_pair(jnp.sum(q * kown_ref[...], axis=1))
            r_ref[...] = jnp.broadcast_to(lk * valid, HEAD_ROW)
            lane = lax.broadcasted_iota(jnp.int32, PAGE_SHAPE, 2)
            acc_ref[...] = jnp.where(lane == 0, _over_dims(jnp.exp(lsp) * valid) * vown_ref[...], 0.0)
        else:
            r_ref[...] = r_in_ref[...]
            acc_ref[...] = jnp.zeros_like(acc_ref)

    @pl.when(live_ref[pl.program_id(0)] != 0)
    def _():
        _score_pages(z_ref, qb_ref, k_refs)
        lsp, lk = _log_sigmoid_pair(z_ref[...])
        totals = _lane_sum(lk)
        r = r_ref[:, 0:1]
        later = [None] * PAGES_PER_STEP
        for j in reversed(range(PAGES_PER_STEP)):
            later[j] = r
            r = r + _page_rows(totals, j)
        r_ref[...] = jnp.broadcast_to(r, HEAD_ROW)
        w_ref[...] = jnp.exp(lsp + _suffix_sum_exclusive(lk) + jnp.concatenate(later, axis=0))
        for h in range(N_HEADS):
            acc_ref[h] += _weigh_values(w_ref, v_refs, range(PAGES_PER_STEP), h)

    @pl.when(n == pl.num_programs(1) - 1)
    def _():
        o_ref[...] = part_in_ref[...] + _lane_sum(acc_ref[...])
        r_out_ref[...] = r_ref[...]


def _decode_specs(n_steps, layer, newest_first):
    def page(j):
        def index_map(b, n, pt, *_):
            step = (n_steps - 1 - n) if newest_first else n
            return (pt[b, PAGES_PER_STEP * step + j], layer, 0, 0, 0)
        return pl.BlockSpec((None, None) + PAGE_SHAPE, index_map)

    col = pl.BlockSpec((None, N_HEADS, HEAD_DIM, 1), lambda b, n, *_: (b, 0, 0, 0))
    return col, [page(j) for j in range(PAGES_PER_STEP)]


def _sb_decode(q, k_own, v_own, cache_k, cache_v, page_table, layer, q_pos, k_own_pos):
    batch = q.shape[0]
    n_steps = page_table.shape[1] // PAGES_PER_STEP
    pos = jnp.array([q_pos, k_own_pos], jnp.int32)
    ck, cv = _page_view(cache_k), _page_view(cache_v)
    col = pl.BlockSpec((None, N_HEADS, HEAD_DIM, 1), lambda b, n, *_: (b, 0, 0, 0))
    row = pl.BlockSpec((None,) + HEAD_ROW, lambda b, n, *_: (b, 0, 0))
    cols = [_head_columns(t) for t in (q, k_own, v_own)]

    def walk(first_step, steps, live, r_in, part_in, newest):
        def page(j):
            def index_map(b, n, pt, pos, live):
                wanted = pt[b, PAGES_PER_STEP * (first_step - n) + j]
                return (jnp.where(live[b] != 0, wanted, pt[0, j]), layer, 0, 0, 0)
            return pl.BlockSpec((None, None) + PAGE_SHAPE, index_map)

        pages = [page(j) for j in range(PAGES_PER_STEP)]
        return pl.pallas_call(
            functools.partial(_sb_decode_kernel, newest=newest),
            out_shape=(jax.ShapeDtypeStruct((batch, N_HEADS, HEAD_DIM, 1), F32),
                       jax.ShapeDtypeStruct((batch,) + HEAD_ROW, F32)),
            grid_spec=pltpu.PrefetchScalarGridSpec(
                num_scalar_prefetch=3,
                grid=(batch, steps),
                in_specs=[col, col, col, row, col] + pages + pages,
                out_specs=(col, row),
                scratch_shapes=[pltpu.VMEM(PAGE_SHAPE, F32), pltpu.VMEM(HEAD_ROW, F32), pltpu.VMEM(PAGE_SHAPE, F32),
                                pltpu.VMEM(STEP_ROWS, F32), pltpu.VMEM(STEP_ROWS, F32)],
            ),
            compiler_params=_cparams("parallel", "arbitrary"),
            name="sb_decode",
        )(page_table, pos, live, *cols, r_in, part_in, *([ck] * PAGES_PER_STEP), *([cv] * PAGES_PER_STEP))

    zeros_r = jnp.zeros((batch,) + HEAD_ROW, F32)
    zeros_part = jnp.zeros((batch, N_HEADS, HEAD_DIM, 1), F32)
    part, r = walk(n_steps - 1, 1, jnp.ones((batch,), jnp.int32), zeros_r, zeros_part, True)
    if n_steps > 1:
        live = (jnp.max(r, axis=(1, 2)) > EXP_UNDERFLOWS_BELOW).astype(jnp.int32)
        part = lax.cond(jnp.any(live != 0),
                        lambda: walk(n_steps - 2, n_steps - 1, live, r, part, False)[0],
                        lambda: part)
    return part.reshape(batch, D_MODEL)


def _moba_decode_kernel(pt_ref, q_ref, kown_ref, vown_ref, *refs):
    k_refs, v_refs = refs[:PAGES_PER_STEP], refs[PAGES_PER_STEP:2 * PAGES_PER_STEP]
    o_ref, qb_ref, gate_ref, m_ref, l_ref, pv_ref, z_ref, w_ref = refs[2 * PAGES_PER_STEP:]
    n = pl.program_id(1)
    n_blocks = gate_ref.shape[0]

    @pl.when(n == 0)
    def _():
        qb_ref[...] = jnp.broadcast_to(q_ref[...] * QK_SCALE, PAGE_SHAPE)

    _score_pages(z_ref, qb_ref, k_refs)
    s = z_ref[...]
    page_max = jnp.max(s, axis=-1, keepdims=True)
    page_sum = _lane_sum(s)
    blocks = [range(b * PAGES_PER_BLOCK, (b + 1) * PAGES_PER_BLOCK) for b in range(PAGES_PER_STEP // PAGES_PER_BLOCK)]
    over_block = lambda x, pages, op: functools.reduce(op, [_page_rows(x, j) for j in pages])
    block_max = [over_block(page_max, pages, jnp.maximum) for pages in blocks]
    w_ref[...] = jnp.exp(s - jnp.concatenate([block_max[j // PAGES_PER_BLOCK] for j in range(PAGES_PER_STEP)], axis=0))
    weight_sum = _lane_sum(w_ref[...])
    for b, pages in enumerate(blocks):
        idx = n * len(blocks) + b
        gate_ref[idx] = jnp.broadcast_to(over_block(page_sum, pages, jnp.add) * (1.0 / MOBA_BLOCK), HEAD_ROW)
        m_ref[idx] = jnp.broadcast_to(block_max[b], HEAD_ROW)
        l_ref[idx] = jnp.broadcast_to(over_block(weight_sum, pages, jnp.add), HEAD_ROW)
        for h in range(N_HEADS):
            pv_ref[idx, h] = _weigh_values(w_ref, v_refs, pages, h)

    @pl.when(n == pl.num_programs(1) - 1)
    def _():
        shape = gate_ref.shape
        blk = lax.broadcasted_iota(jnp.int32, shape, 0)
        gate = gate_ref[...]
        sel = jnp.zeros(shape, jnp.bool_)
        for r in range(MOBA_TOPK):
            best = jnp.max(gate, axis=0, keepdims=True)
            idx = jnp.min(jnp.where(gate == best, blk, shape[0]), axis=0, keepdims=True)
            pick = blk == idx
            sel = sel | (pick & (r < n_blocks))
            gate = jnp.where(pick, -jnp.inf, gate)
        q = q_ref[...] * QK_SCALE
        s_own = jnp.broadcast_to(jnp.sum(q * kown_ref[...], axis=1), HEAD_ROW)
        m_all = m_ref[...]
        m_fin = jnp.maximum(jnp.max(jnp.where(sel, m_all, -jnp.inf), axis=0), s_own)
        w = jnp.where(sel, jnp.exp(m_all - m_fin[None]), 0.0)
        w_own = jnp.exp(s_own - m_fin)
        denom = jnp.sum(w * l_ref[...], axis=0) + w_own
        gate_ref[...] = w
        numer = lax.fori_loop(0, n_blocks, lambda i, acc: acc + _over_dims(gate_ref[i]) * pv_ref[i],
                              jnp.zeros(PAGE_SHAPE, F32))
        first_lane = lambda t: _over_dims(t)[:, :, 0:1]
        o_ref[...] = (_lane_sum(numer) + first_lane(w_own) * vown_ref[...]) / first_lane(denom)


def _moba_decode(q, k_own, v_own, cache_k, cache_v, page_table, layer):
    batch = q.shape[0]
    n_pages = page_table.shape[1]
    assert PAGES_PER_STEP % PAGES_PER_BLOCK == 0 and n_pages % PAGES_PER_STEP == 0
    n_blocks = n_pages // PAGES_PER_BLOCK
    col, pages = _decode_specs(n_pages // PAGES_PER_STEP, layer, False)
    ck, cv = _page_view(cache_k), _page_view(cache_v)
    stat = pltpu.VMEM((n_blocks,) + HEAD_ROW, F32)
    out = pl.pallas_call(
        _moba_decode_kernel,
        out_shape=jax.ShapeDtypeStruct((batch, N_HEADS, HEAD_DIM, 1), F32),
        grid_spec=pltpu.PrefetchScalarGridSpec(
            num_scalar_prefetch=1,
            grid=(batch, n_pages // PAGES_PER_STEP),
            in_specs=[col, col, col] + pages + pages,
            out_specs=col,
            scratch_shapes=[pltpu.VMEM(PAGE_SHAPE, F32), stat, stat, stat,
                            pltpu.VMEM((n_blocks,) + PAGE_SHAPE, F32),
                            pltpu.VMEM(STEP_ROWS, F32), pltpu.VMEM(STEP_ROWS, F32)],
        ),
        compiler_params=_cparams("parallel", "arbitrary"),
        name="moba_decode",
    )(page_table, _head_columns(q), _head_columns(k_own), _head_columns(v_own),
      *([ck] * PAGES_PER_STEP), *([cv] * PAGES_PER_STEP))
    return out.reshape(batch, D_MODEL)


def kernel(x_prompt, x_sample, cache_sb_k, cache_sb_v, cache_moba_k, cache_moba_v, state_ffn_conv, page_table, p_prompt, p_sample, mix_norm, ffn_norm, ple_norm, sgu_w_in, sgu_v_norm, sgu_w_s, sgu_b_s, sgu_w_out, sb_w_qkv, sb_w_out, moba_w_qkv, moba_q_norm, moba_k_norm, moba_w_out, ffn_w_in, ffn_conv_w, ffn_conv_b, ffn_w_out, ple_w_in, ple_w_gate):
    bp, tp, _ = x_prompt.shape
    bs, ts, _ = x_sample.shape
    assert ts == 1 and tp % (2 * SGU_CHUNK) == 0 and tp % MOBA_BLOCK == 0
    depth = mix_norm.shape[0]
    past_len = page_table.shape[1] * PAGE_SIZE
    assert past_len % MOBA_BLOCK == 0 and past_len % SGU_CHUNK == 0
    xp = x_prompt.reshape(bp * tp, D_MODEL)
    xs = x_sample.reshape(bs, D_MODEL)
    bf = lambda w: w.astype(BF16)
    heads_s = lambda t: t.reshape(bs, ts, N_HEADS, HEAD_DIM)

    sb_kp, sb_vp, sb_ks, sb_vs = [], [], [], []
    mb_kp, mb_vp, mb_ks, mb_vs = [], [], [], []
    sgu_vs, conv_p, conv_s = [], [], []

    for i in range(depth):
        kind, j = i % N_MIXERS, i // N_MIXERS
        if kind == 0:
            w_in, w_out = bf(sgu_w_in[j]), bf(sgu_w_out[j])
            zs = _norm_matmul(xs, mix_norm[i], w_in, 2, act="gelu")
            xp = _sgu_prompt(xp, mix_norm[i], w_in, sgu_v_norm[j], sgu_w_s[j], sgu_b_s[j], w_out)
            xs, v_s = _sgu_sample(zs, sgu_v_norm[j], sgu_w_s[j], sgu_b_s[j], w_out, xs)
            sgu_vs.append(v_s.reshape(bs, ts, SGU_WIDTH))
        elif kind == 1:
            w_qkv, w_out = bf(sb_w_qkv[j]), bf(sb_w_out[j])
            qkv_p, k_t, v_t = _qkv_prompt(xp, mix_norm[i], w_qkv, bp, tp, (1, 2))
            qkv_s = _norm_matmul(xs, mix_norm[i], w_qkv, 3)
            xp = _proj_residual(_sb_prompt(qkv_p, bp, tp), w_out, xp)
            att = _sb_decode(qkv_s[0], qkv_s[1], qkv_s[2], cache_sb_k, cache_sb_v,
                             page_table, j, past_len, past_len)
            xs = _proj_residual(att, w_out, xs)
            sb_kp.append(_prompt_cache_rows(k_t))
            sb_vp.append(_prompt_cache_rows(v_t))
            sb_ks.append(heads_s(qkv_s[1]))
            sb_vs.append(heads_s(qkv_s[2]))
        else:
            w_qkv, w_out = bf(moba_w_qkv[j]), bf(moba_w_out[j])
            qkv_p, v_t = _qkv_prompt(xp, mix_norm[i], w_qkv, bp, tp, (2,))
            qkv_s = _norm_matmul(xs, mix_norm[i], w_qkv, 3)
            qk_p, k_t = _qknorm_rope(qkv_p, moba_q_norm[j], moba_k_norm[j], jnp.arange(tp, dtype=jnp.int32), bp)
            qk_s = _qknorm_rope(qkv_s, moba_q_norm[j], moba_k_norm[j], jnp.full((bs,), past_len, jnp.int32))
            xp = _proj_residual(_moba_prompt(qk_p, qkv_p, bp, tp), w_out, xp)
            att = _moba_decode(qk_s[0], qk_s[1], qkv_s[2], cache_moba_k, cache_moba_v, page_table, j)
            xs = _proj_residual(att, w_out, xs)
            mb_kp.append(_prompt_cache_rows(k_t))
            mb_vp.append(_prompt_cache_rows(v_t))
            mb_ks.append(heads_s(qk_s[1]))
            mb_vs.append(heads_s(qkv_s[2]))

        w_in, w_out = bf(ffn_w_in[i]), bf(ffn_w_out[i])
        w_gate, w_pin = bf(ple_w_gate[i]), bf(ple_w_in[i])
        gv_s = _norm_matmul(xs, ffn_norm[i], w_in, 2)
        xp, g_tails = _ffn_prompt(xp, ffn_norm[i], w_in, ffn_conv_w[i], ffn_conv_b[i], w_out, tp,
                                  p_prompt.reshape(depth, bp * tp, -1), i, ple_norm[i], w_gate, w_pin)
        xs = _ffn_sample(gv_s, state_ffn_conv[i], ffn_conv_w[i], ffn_conv_b[i], w_out, xs)
        conv_p.append(g_tails.reshape(bp, -1, SUBLANES, D_FF)[:, -1, SUBLANES - (CONV_W - 1):])
        conv_s.append(jnp.concatenate([state_ffn_conv[i][:, 1:], gv_s[0][:, None]], axis=1))
        xs = _ple(xs, p_sample.reshape(depth, bs, -1), i, ple_norm[i], w_gate, w_pin)

    return (xp.reshape(bp, tp, D_MODEL), xs.reshape(bs, ts, D_MODEL),
            jnp.concatenate(sb_kp, axis=1), jnp.concatenate(sb_vp, axis=1),
            jnp.stack(sb_ks, axis=1), jnp.stack(sb_vs, axis=1),
            jnp.concatenate(mb_kp, axis=1), jnp.concatenate(mb_vp, axis=1),
            jnp.stack(mb_ks, axis=1), jnp.stack(mb_vs, axis=1),
            jnp.stack(sgu_vs, axis=0), jnp.stack(conv_p, axis=0), jnp.stack(conv_s, axis=0))
```
